```python
import math
import jax, jax.numpy as jnp
from jax import lax
import numpy as np

D_MODEL = 1024
BATCH = 4
SEQ = 4096
DEPTH = 2

N_META = 16
CHUNK = 128
PAD = (-N_META) % CHUNK
RET_HEADS = 4
RET_DK = 48
RET_DV = 96
RET_QK = RET_HEADS * RET_DK
RET_W = RET_HEADS * RET_DV
POOL_WINDOWS = (2, 4, 8, 16)
POOL_GROUP = 64
POOL_W = len(POOL_WINDOWS) * POOL_GROUP
FOX_HEADS = 6
FOX_DH = 64
FOX_W = FOX_HEADS * FOX_DH
D_MIX = RET_W + POOL_W + FOX_W
IN_SPLITS = (RET_QK, RET_QK, RET_W, RET_W, POOL_W, FOX_W, FOX_W, FOX_W, FOX_HEADS)
D_IN = sum(IN_SPLITS)
D_FF = -(-8 * D_MODEL // (3 * 256)) * 256
ROPE_BASE = 10000.0
LN_EPS = 1e-5
NEG_INF = -1e30
ALPHA = (2.0 * DEPTH) ** 0.25
BETA = (8.0 * DEPTH) ** -0.25

kernel_name = 'hymba_style_retention_pool_fox_deepnorm'


def layer_norm(x, g, b):
    xf = x.astype(jnp.float32)
    mu = jnp.mean(xf, axis=-1, keepdims=True)
    var = jnp.mean(jnp.square(xf - mu), axis=-1, keepdims=True)
    y = (xf - mu) * lax.rsqrt(var + LN_EPS)
    return (y * g + b).astype(x.dtype)


def pad_front(a):
    widths = [(0, 0), (PAD, 0)] + [(0, 0)] * (a.ndim - 2)
    return jnp.pad(a, widths)


def rotary(x, cos, sin):
    half = x.shape[-1] // 2
    x1, x2 = x[..., :half], x[..., half:]
    return jnp.concatenate([x1 * cos - x2 * sin, x1 * sin + x2 * cos], axis=-1)


def retention_chunkwise(q, k, v):
    Bsz, Lp, H, dk = q.shape
    dv = v.shape[-1]
    n = Lp // CHUNK
    gamma = 1.0 - 2.0 ** (-5.0 - jnp.arange(H, dtype=jnp.float32))
    lg = jnp.log(gamma)
    i = jnp.arange(CHUNK, dtype=jnp.float32)
    diff = i[:, None] - i[None, :]
    dmask = jnp.where(diff >= 0, jnp.exp(lg[:, None, None] * jnp.maximum(diff, 0.0)), 0.0)
    xi = jnp.exp(lg[:, None] * (i + 1.0))
    zeta = jnp.exp(lg[:, None] * (CHUNK - 1.0 - i))
    chunk_decay = jnp.exp(lg * CHUNK)
    qc = q.astype(jnp.float32).reshape(Bsz, n, CHUNK, H, dk)
    kc = k.astype(jnp.float32).reshape(Bsz, n, CHUNK, H, dk) * (dk ** -0.5)
    vc = v.astype(jnp.float32).reshape(Bsz, n, CHUNK, H, dv)
    scores = jnp.einsum('bnihd,bnjhd->bnhij', qc, kc) * dmask
    inner = jnp.einsum('bnhij,bnjhe->bnihe', scores, vc)
    kv = jnp.einsum('bnjhd,hj,bnjhe->nbhde', kc, zeta, vc)

    def step(state, kv_n):
        return state * chunk_decay[None, :, None, None] + kv_n, state

    _, prev = lax.scan(step, jnp.zeros_like(kv[0]), kv)
    cross = jnp.einsum('bnihd,nbhde,hi->bnihe', qc, prev, xi)
    return (inner + cross).reshape(Bsz, Lp, H, dv)


def multiscale_pool(u, pool_w, pool_scale):
    Bsz, L, _ = u.shape
    G = len(POOL_WINDOWS)
    uf = u.astype(jnp.float32).reshape(Bsz, L, G, POOL_GROUP)
    cs = jnp.concatenate([jnp.zeros((Bsz, 1, G, POOL_GROUP), jnp.float32),
                          jnp.cumsum(uf, axis=1)], axis=1)
    t = jnp.arange(L)
    outs = []
    for g, w in enumerate(POOL_WINDOWS):
        lo = jnp.maximum(t + 1 - w, 0)
        cnt = (t + 1 - lo).astype(jnp.float32)
        window_sum = cs[:, t + 1, g] - cs[:, lo, g]
        outs.append(window_sum / cnt[None, :, None] - uf[:, :, g])
    pooled = jnp.stack(outs, axis=2)
    y = jnp.einsum('blgc,gcd->blgd', pooled, pool_w.astype(jnp.float32)).reshape(Bsz, L, POOL_W)
    return y * pool_scale


def forgetting_attention(q, k, v, logf):
    Bsz, Lp, H, d = q.shape
    c = jnp.cumsum(logf.astype(jnp.float32), axis=1).transpose(0, 2, 1)
    qf = q.astype(jnp.float32) * (d ** -0.5)
    kf = k.astype(jnp.float32)
    vf = v.astype(jnp.float32)
    kpos = jnp.arange(Lp)
    key_ok = kpos >= PAD
    nb = Lp // CHUNK

    def block(ib):
        start = ib * CHUNK
        qb = lax.dynamic_slice_in_dim(qf, start, CHUNK, axis=1)
        cb = lax.dynamic_slice_in_dim(c, start, CHUNK, axis=2)
        qpos = start + jnp.arange(CHUNK)
        s = jnp.einsum('bihd,bjhd->bhij', qb, kf)
        s = s + cb[..., :, None] - c[..., None, :]
        mask = (kpos[None, :] <= qpos[:, None]) & key_ok[None, :]
        s = jnp.where(mask, s, NEG_INF)
        p = jax.nn.softmax(s, axis=-1)
        return jnp.einsum('bhij,bjhd->bihd', p, vf)

    out = lax.map(block, jnp.arange(nb))
    return out.transpose(1, 0, 2, 3, 4).reshape(Bsz, Lp, H, d)


def head_group_norm(o, g):
    of = o.astype(jnp.float32)
    mu = jnp.mean(of, axis=-1, keepdims=True)
    var = jnp.mean(jnp.square(of - mu), axis=-1, keepdims=True)
    y = ((of - mu) * lax.rsqrt(var + LN_EPS)).reshape(o.shape[0], o.shape[1], -1)
    return y * g


def hybrid_layer(h, w_in, b_f, ret_gn_g, pool_w, pool_scale, w_out,
                 ln1_g, ln1_b, w1, w3, w2, ln2_g, ln2_b, cos, sin):
    Bsz, L, _ = h.shape
    proj = jnp.einsum('bld,de->ble', h, w_in)
    parts = []
    off = 0
    for sz in IN_SPLITS:
        parts.append(proj[..., off:off + sz])
        off += sz
    q_r, k_r, v_r, g_r, u_p, q_f, k_f, v_f, f_logit = parts

    qr = rotary(pad_front(q_r.reshape(Bsz, L, RET_HEADS, RET_DK)), cos, sin)
    kr = rotary(pad_front(k_r.reshape(Bsz, L, RET_HEADS, RET_DK)), cos, sin)
    vr = pad_front(v_r.reshape(Bsz, L, RET_HEADS, RET_DV))
    o_r = retention_chunkwise(qr, kr, vr)[:, PAD:]
    o_r = jax.nn.silu(g_r.astype(jnp.float32)) * head_group_norm(o_r, ret_gn_g)

    o_p = multiscale_pool(u_p, pool_w, pool_scale)

    logf = jax.nn.log_sigmoid(f_logit.astype(jnp.float32) + b_f)
    o_f = forgetting_attention(pad_front(q_f.reshape(Bsz, L, FOX_HEADS, FOX_DH)),
                               pad_front(k_f.reshape(Bsz, L, FOX_HEADS, FOX_DH)),
                               pad_front(v_f.reshape(Bsz, L, FOX_HEADS, FOX_DH)),
                               pad_front(logf))[:, PAD:].reshape(Bsz, L, FOX_W)

    mix = jnp.concatenate([o_r, o_p, o_f], axis=-1)
    mix = jnp.einsum('ble,ed->bld', mix, w_out)
    h = layer_norm(ALPHA * h + mix, ln1_g, ln1_b)

    a = jnp.einsum('bld,df->blf', h, w1)
    b = jnp.einsum('bld,df->blf', h, w3)
    y = jnp.einsum('blf,fd->bld', jax.nn.silu(a) * b, w2)
    return layer_norm(ALPHA * h + y, ln2_g, ln2_b)


def setup_inputs(seed: int = 0) -> dict:
    key = jax.random.key(seed)
    ks = jax.random.split(key, 20)
    f32 = jnp.float32
    x = jax.random.normal(ks[0], (BATCH, SEQ, D_MODEL), f32)
    meta = jax.random.normal(ks[1], (N_META, D_MODEL), f32)
    ln_emb_g = 1.0 + 0.01 * jax.random.normal(ks[2], (D_MODEL,), f32)
    ln_emb_b = 0.01 * jax.random.normal(ks[3], (D_MODEL,), f32)
    w_in = jax.random.normal(ks[4], (DEPTH, D_MODEL, D_IN), f32) * D_MODEL ** -0.5
    b_f = jax.random.uniform(ks[5], (DEPTH, FOX_HEADS), f32, 1.0, 4.0)
    ret_gn_g = 1.0 + 0.01 * jax.random.normal(ks[6], (DEPTH, RET_W), f32)
    pool_w = jax.random.normal(ks[7], (DEPTH, len(POOL_WINDOWS), POOL_GROUP, POOL_GROUP), f32) * POOL_GROUP ** -0.5
    pool_scale = 1.0 + 0.1 * jax.random.normal(ks[8], (DEPTH, POOL_W), f32)
    w_out = jax.random.normal(ks[9], (DEPTH, D_MIX, D_MODEL), f32) * (D_MIX ** -0.5) * BETA
    ln1_g = 1.0 + 0.01 * jax.random.normal(ks[10], (DEPTH, D_MODEL), f32)
    ln1_b = 0.01 * jax.random.normal(ks[11], (DEPTH, D_MODEL), f32)
    w_ffn1 = jax.random.normal(ks[12], (DEPTH, D_MODEL, D_FF), f32) * D_MODEL ** -0.5
    w_ffn3 = jax.random.normal(ks[13], (DEPTH, D_MODEL, D_FF), f32) * D_MODEL ** -0.5
    w_ffn2 = jax.random.normal(ks[14], (DEPTH, D_FF, D_MODEL), f32) * (D_FF ** -0.5) * BETA
    ln2_g = 1.0 + 0.01 * jax.random.normal(ks[15], (DEPTH, D_MODEL), f32)
    ln2_b = 0.01 * jax.random.normal(ks[16], (DEPTH, D_MODEL), f32)
    return {'x': x, 'meta': meta, 'ln_emb_g': ln_emb_g, 'ln_emb_b': ln_emb_b,
            'w_in': w_in, 'b_f': b_f, 'ret_gn_g': ret_gn_g, 'pool_w': pool_w,
            'pool_scale': pool_scale, 'w_out': w_out, 'ln1_g': ln1_g, 'ln1_b': ln1_b,
            'w_ffn1': w_ffn1, 'w_ffn3': w_ffn3, 'w_ffn2': w_ffn2,
            'ln2_g': ln2_g, 'ln2_b': ln2_b}


def reference(x, meta, ln_emb_g, ln_emb_b, w_in, b_f, ret_gn_g, pool_w, pool_scale,
              w_out, ln1_g, ln1_b, w_ffn1, w_ffn3, w_ffn2, ln2_g, ln2_b):
    Bsz = x.shape[0]
    h = jnp.concatenate([jnp.broadcast_to(meta[None].astype(x.dtype), (Bsz, N_META, D_MODEL)), x], axis=1)
    h = layer_norm(h, ln_emb_g, ln_emb_b)
    L = h.shape[1]
    Lp = L + PAD
    pos = jnp.arange(Lp, dtype=jnp.float32) - PAD
    inv_freq = ROPE_BASE ** (-jnp.arange(RET_DK // 2, dtype=jnp.float32) / (RET_DK // 2))
    ang = pos[:, None] * inv_freq[None, :]
    cos = jnp.cos(ang)[:, None, :]
    sin = jnp.sin(ang)[:, None, :]
    for l in range(DEPTH):
        h = hybrid_layer(h, w_in[l], b_f[l], ret_gn_g[l], pool_w[l], pool_scale[l], w_out[l],
                         ln1_g[l], ln1_b[l], w_ffn1[l], w_ffn3[l], w_ffn2[l], ln2_g[l], ln2_b[l],
                         cos, sin)
    return h[:, N_META:]
```

```python
import functools

import numpy as np
import jax
import jax.numpy as jnp
from jax import lax
from jax.experimental import pallas as pl
from jax.experimental.pallas import tpu as pltpu

F32 = jnp.float32
BF16 = jnp.bfloat16

D_MODEL = 1024
N_META = 16
RET_HEADS = 4
RET_DK = 48
RET_HALF = RET_DK // 2
RET_DV = 96
RET_QK = RET_HEADS * RET_DK
RET_W = RET_HEADS * RET_DV
POOL_WINDOWS = (2, 4, 8, 16)
POOL_GROUP = 64
POOL_W = len(POOL_WINDOWS) * POOL_GROUP
FOX_HEADS = 6
FOX_DH = 64
FOX_W = FOX_HEADS * FOX_DH
D_FF = 2816
ROPE_BASE = 10000.0
LN_EPS = 1e-5
NEG_INF = -1e30
DEPTH = 2
ALPHA = (2.0 * DEPTH) ** 0.25

LANES = 128
MXU_DIM = 256
SEQ_BLOCK = 256
ROW_TILE = 512
VMEM_LIMIT = 56 * 1024 * 1024

RET_HEAD_PAD = LANES
QR_OFF = 0
KR_OFF = 2 * LANES
VR_OFF = 4 * LANES
GR_OFF = VR_OFF + RET_HEADS * RET_HEAD_PAD
UP_OFF = GR_OFF + RET_HEADS * RET_HEAD_PAD
QF_OFF = UP_OFF + POOL_W
KF_OFF = QF_OFF + FOX_W
VF_OFF = KF_OFF + FOX_W
FL_OFF = VF_OFF + FOX_W
N_PROJ = FL_OFF
N_PACK = FL_OFF + LANES
RETV_W = RET_HEADS * RET_HEAD_PAD
MIX_W = RETV_W + POOL_W + FOX_W
PROJ_CHUNK = 512
FF_CHUNK = 256


def _layer_norm(x, g, b):
    mu = jnp.mean(x, axis=-1, keepdims=True)
    d = x - mu
    var = jnp.mean(d * d, axis=-1, keepdims=True)
    return d * lax.rsqrt(var + LN_EPS) * g + b


def _sigmoid(x):
    return 1.0 / (1.0 + jnp.exp(-x))


def _embed_kernel(meta_ref, xm_ref, xe_ref, g_ref, b_ref, o_ref):
    j = pl.program_id(1)
    last = pl.num_programs(1) - 1
    top = jnp.where(j == 0, meta_ref[...], xe_ref[...])
    body = jnp.where(j == last, 0.0, xm_ref[0:SEQ_BLOCK - N_META, :])
    rows = jnp.concatenate([top, body], axis=0)
    o_ref[...] = _layer_norm(rows, g_ref[...], b_ref[...])


def _embed(x, meta, g, b, lp):
    bsz, seq, d = x.shape
    nblk = lp // SEQ_BLOCK
    n_xblk = seq // SEQ_BLOCK
    per = SEQ_BLOCK // N_META
    return pl.pallas_call(
        _embed_kernel,
        out_shape=jax.ShapeDtypeStruct((bsz, lp, d), F32),
        grid=(bsz, nblk),
        in_specs=[
            pl.BlockSpec((N_META, d), lambda bb, j: (0, 0)),
            pl.BlockSpec((None, SEQ_BLOCK, d), lambda bb, j: (bb, jnp.minimum(j, n_xblk - 1), 0)),
            pl.BlockSpec((None, N_META, d), lambda bb, j: (bb, jnp.maximum(per * j - 1, 0), 0)),
            pl.BlockSpec((1, d), lambda bb, j: (0, 0)),
            pl.BlockSpec((1, d), lambda bb, j: (0, 0)),
        ],
        out_specs=pl.BlockSpec((None, SEQ_BLOCK, d), lambda bb, j: (bb, j, 0)),
        compiler_params=pltpu.CompilerParams(dimension_semantics=("parallel", "arbitrary")),
        name="embed_ln",
    )(meta, x, x, g.reshape(1, d), b.reshape(1, d))


def _inproj_kernel(h_ref, w_ref, proj_ref, flog_ref):
    xb = h_ref[...].astype(BF16)
    for c0 in range(0, N_PACK, PROJ_CHUNK):
        r = jnp.dot(xb, w_ref[:, c0:c0 + PROJ_CHUNK], preferred_element_type=F32)
        if c0 + PROJ_CHUNK <= N_PROJ:
            proj_ref[:, c0:c0 + PROJ_CHUNK] = r.astype(BF16)
        else:
            proj_ref[:, c0:N_PROJ] = r[:, :N_PROJ - c0].astype(BF16)
            flog_ref[...] = r[:, N_PROJ - c0:]


def _inproj(h, w, tm):
    rows = h.shape[0]
    return pl.pallas_call(
        _inproj_kernel,
        out_shape=(jax.ShapeDtypeStruct((rows, N_PROJ), BF16),
                   jax.ShapeDtypeStruct((rows, LANES), F32)),
        grid=(rows // tm,),
        in_specs=[
            pl.BlockSpec((tm, D_MODEL), lambda i: (i, 0)),
            pl.BlockSpec((D_MODEL, N_PACK), lambda i: (0, 0)),
        ],
        out_specs=(pl.BlockSpec((tm, N_PROJ), lambda i: (i, 0)),
                   pl.BlockSpec((tm, LANES), lambda i: (i, 0))),
        compiler_params=pltpu.CompilerParams(dimension_semantics=("parallel",),
                                             vmem_limit_bytes=VMEM_LIMIT),
        name="in_proj",
    )(h, w)


def _seq_kernel(qk_ref, v_ref, g_ref, u_ref, fl_ref, cos_ref, sin_ref,
                xiq_ref, zk_ref, dm_ref, dec_ref, bm_ref, gng_ref, bf_ref, pw_ref, ps_ref,
                or_ref, op_ref, c_ref,
                state_sc, tail_sc, carry_sc):
    j = pl.program_id(1)
    ch = SEQ_BLOCK

    @pl.when(j == 0)
    def _():
        state_sc[...] = jnp.zeros_like(state_sc)
        tail_sc[...] = jnp.zeros_like(tail_sc)
        carry_sc[...] = jnp.zeros_like(carry_sc)

    qk = qk_ref[...].astype(F32)
    cs = cos_ref[...]
    sn = sin_ref[...]
    q1, q2 = qk[:, 0:LANES], qk[:, LANES:2 * LANES]
    k1, k2 = qk[:, 2 * LANES:3 * LANES], qk[:, 3 * LANES:4 * LANES]
    qr = jnp.concatenate([q1 * cs - q2 * sn, q1 * sn + q2 * cs], axis=-1)
    kr = jnp.concatenate([k1 * cs - k2 * sn, k1 * sn + k2 * cs], axis=-1) * (RET_DK ** -0.5)
    qb = qr.astype(BF16)
    qx = (qr * xiq_ref[...]).astype(BF16)
    kb = kr.astype(BF16)
    kz = (kr * zk_ref[...]).astype(BF16)
    v = v_ref[...]
    st = state_sc[...]
    cross = jnp.dot(qx, st.astype(BF16), preferred_element_type=F32)
    qlane = lax.broadcasted_iota(jnp.int32, (1, 2 * LANES), 1)
    qhead = jnp.where(qlane % LANES < RET_HEADS * RET_HALF, (qlane % LANES) // RET_HALF, RET_HEADS)
    inner = []
    for h in range(RET_HEADS):
        qh = jnp.where(qhead == h, qb, jnp.zeros_like(qb))
        s = lax.dot_general(qh, kb, (((1,), (1,)), ((), ())), preferred_element_type=F32)
        p = (s * dm_ref[h]).astype(BF16)
        inner.append(jnp.dot(p, v[:, h * LANES:(h + 1) * LANES], preferred_element_type=F32))
    o = jnp.concatenate(inner, axis=-1) + cross
    kv = lax.dot_general(kz, v, (((0,), (0,)), ((), ())), preferred_element_type=F32)
    state_sc[...] = st * dec_ref[...] + kv * bm_ref[...]

    vlane = lax.broadcasted_iota(jnp.int32, (1, LANES), 1) < RET_DV
    normed = []
    for h in range(RET_HEADS):
        xh = o[:, h * LANES:(h + 1) * LANES]
        mu = jnp.sum(xh, axis=-1, keepdims=True) * (1.0 / RET_DV)
        d = jnp.where(vlane, xh - mu, 0.0)
        var = jnp.sum(d * d, axis=-1, keepdims=True) * (1.0 / RET_DV)
        normed.append(d * lax.rsqrt(var + LN_EPS))
    y = jnp.concatenate(normed, axis=-1) * gng_ref[...]
    gate = g_ref[...].astype(F32)
    or_ref[...] = (gate * _sigmoid(gate) * y).astype(BF16)

    u = u_ref[...].astype(F32)
    tail_rows = tail_sc.shape[0]
    ext = jnp.concatenate([tail_sc[...], u], axis=0)
    tail_sc[...] = u[ch - tail_rows:, :]
    e2 = ext + pltpu.roll(ext, 1, 0)
    e4 = e2 + pltpu.roll(e2, 2, 0)
    e8 = e4 + pltpu.roll(e4, 4, 0)
    e16 = e8 + pltpu.roll(e8, 8, 0)
    glane = lax.broadcasted_iota(jnp.int32, (1, POOL_W), 1) // POOL_GROUP
    win = jnp.where(glane == 0, e2, jnp.where(glane == 1, e4, jnp.where(glane == 2, e8, e16)))
    win = win[tail_rows:, :]
    wlen = jnp.where(glane == 0, 2, jnp.where(glane == 1, 4, jnp.where(glane == 2, 8, 16)))
    pos = j * ch + lax.broadcasted_iota(jnp.int32, (ch, POOL_W), 0)
    cnt = jnp.minimum(pos + 1, wlen).astype(F32)
    pooled = (win / cnt - u).astype(BF16)
    yp = jnp.dot(pooled, pw_ref[...], preferred_element_type=F32) * ps_ref[...]
    op_ref[...] = yp.astype(BF16)

    z = fl_ref[...] + bf_ref[...]
    logf = jnp.minimum(z, 0.0) - jnp.log1p(jnp.exp(-jnp.abs(z)))
    row = lax.broadcasted_iota(jnp.int32, (ch, LANES), 0)
    sh = 1
    while sh < ch:
        logf = logf + jnp.where(row >= sh, pltpu.roll(logf, sh, 0), 0.0)
        sh *= 2
    c = logf + carry_sc[0:1, :]
    carry_sc[...] = jnp.broadcast_to(c[ch - 1:ch, :], carry_sc.shape)
    c_ref[...] = c.T[0:8, :]


def _seq_mix(proj, flog, consts, lw, bsz, lp):
    ch = SEQ_BLOCK
    nc = lp // ch
    rows = bsz * lp

    def rowblk(width, colblk):
        return pl.BlockSpec((ch, width), lambda bb, j: (bb * nc + j, colblk))

    def const(shape):
        nd = len(shape)
        return pl.BlockSpec(shape, lambda bb, j: (0,) * nd)

    in_specs = [
        rowblk(4 * LANES, QR_OFF // (4 * LANES)),
        rowblk(RETV_W, VR_OFF // RETV_W),
        rowblk(RETV_W, GR_OFF // RETV_W),
        rowblk(POOL_W, UP_OFF // POOL_W),
        pl.BlockSpec((ch, LANES), lambda bb, j: (bb * nc + j, 0)),
        pl.BlockSpec((ch, LANES), lambda bb, j: (j, 0)),
        pl.BlockSpec((ch, LANES), lambda bb, j: (j, 0)),
        const((ch, 2 * LANES)), const((ch, 2 * LANES)), const((RET_HEADS, ch, ch)),
        const((1, RETV_W)), const((2 * LANES, RETV_W)),
        const((1, RETV_W)), const((1, LANES)), const((POOL_W, POOL_W)), const((1, POOL_W)),
    ]
    out_shape = (jax.ShapeDtypeStruct((rows, RETV_W), BF16),
                 jax.ShapeDtypeStruct((rows, POOL_W), BF16),
                 jax.ShapeDtypeStruct((bsz, 8, lp), F32))
    out_specs = (pl.BlockSpec((ch, RETV_W), lambda bb, j: (bb * nc + j, 0)),
                 pl.BlockSpec((ch, POOL_W), lambda bb, j: (bb * nc + j, 0)),
                 pl.BlockSpec((None, 8, ch), lambda bb, j: (bb, 0, j)))
    return pl.pallas_call(
        _seq_kernel,
        out_shape=out_shape,
        grid=(bsz, nc),
        in_specs=in_specs,
        out_specs=out_specs,
        scratch_shapes=[pltpu.VMEM((2 * LANES, RETV_W), F32),
                        pltpu.VMEM((16, POOL_W), F32),
                        pltpu.VMEM((8, LANES), F32)],
        compiler_params=pltpu.CompilerParams(dimension_semantics=("parallel", "arbitrary"),
                                             vmem_limit_bytes=VMEM_LIMIT),
        name="seq_mixers",
    )(proj, proj, proj, proj, flog, consts["cos"], consts["sin"],
      consts["xiq"], consts["zk"], consts["dm"], consts["dec"], consts["bm"],
      lw["gn_g"], lw["b_f"], lw["pool_w"], lw["pool_scale"])


def _fox_kernel(q_ref, k_ref, v_ref, c_ref, o_ref, va_sc, vb_sc, m_sc, acc_sc):
    pair = pl.program_id(1)
    i = pl.program_id(2)
    tq = SEQ_BLOCK
    tk = SEQ_BLOCK
    half = lax.broadcasted_iota(jnp.int32, (1, LANES), 1) < FOX_DH

    @pl.when(i == 0)
    def _():
        vv = v_ref[...]
        one = jnp.ones_like(vv)
        va_sc[...] = jnp.where(half, vv, one)
        vb_sc[...] = jnp.where(half, one, vv)

    q = q_ref[...] * jnp.asarray(FOX_DH ** -0.5, BF16)
    zero = jnp.zeros_like(q)
    q_heads = (jnp.where(half, q, zero), jnp.where(half, zero, q))
    v_heads = (va_sc, vb_sc)
    m_sc[...] = jnp.full(m_sc.shape, NEG_INF, F32)
    acc_sc[...] = jnp.zeros_like(acc_sc)

    def step(jb, masked):
        ks = pl.multiple_of(jb * tk, tk)
        kblk = k_ref[pl.ds(ks, tk), :]
        for hh in range(2):
            s = lax.dot_general(q_heads[hh], kblk, (((1,), (1,)), ((), ())),
                                preferred_element_type=F32)
            s = s - c_ref[pl.ds(2 * pair + hh, 1), pl.ds(ks, tk)]
            if masked:
                r_id = lax.broadcasted_iota(jnp.int32, (tq, tk), 0)
                c_id = lax.broadcasted_iota(jnp.int32, (tq, tk), 1)
                s = jnp.where(c_id <= r_id, s, NEG_INF)
            m_prev = m_sc[hh]
            m_new = jnp.maximum(m_prev, jnp.max(s, axis=-1, keepdims=True))
            p = jnp.exp(s - jnp.concatenate([m_new] * (tk // LANES), axis=-1))
            pv = jnp.dot(p.astype(BF16), v_heads[hh][pl.ds(ks, tk), :], preferred_element_type=F32)
            acc_sc[hh] = jnp.exp(m_prev - m_new) * acc_sc[hh] + pv
            m_sc[hh] = m_new

    def body(jb, carry):
        step(jb, False)
        return carry

    lax.fori_loop(0, i, body, 0)
    step(i, True)

    acc_a = acc_sc[0]
    acc_b = acc_sc[1]
    out = jnp.where(half, acc_a / pltpu.roll(acc_a, FOX_DH, 1), acc_b / pltpu.roll(acc_b, FOX_DH, 1))
    o_ref[...] = out.astype(BF16)


def _fox(proj, cum, bsz, lp):
    tq = SEQ_BLOCK
    nq = lp // tq
    rows = bsz * lp
    npair = FOX_HEADS // 2
    return pl.pallas_call(
        _fox_kernel,
        out_shape=jax.ShapeDtypeStruct((rows, FOX_W), BF16),
        grid=(bsz, npair, nq),
        in_specs=[
            pl.BlockSpec((tq, LANES), lambda bb, p, i: (bb * nq + i, QF_OFF // LANES + p)),
            pl.BlockSpec((lp, LANES), lambda bb, p, i: (bb, KF_OFF // LANES + p)),
            pl.BlockSpec((lp, LANES), lambda bb, p, i: (bb, VF_OFF // LANES + p)),
            pl.BlockSpec((None, 8, lp), lambda bb, p, i: (bb, 0, 0)),
        ],
        out_specs=pl.BlockSpec((tq, LANES), lambda bb, p, i: (bb * nq + i, p)),
        scratch_shapes=[pltpu.VMEM((lp, LANES), BF16),
                        pltpu.VMEM((lp, LANES), BF16),
                        pltpu.VMEM((2, tq, LANES), F32),
                        pltpu.VMEM((2, tq, LANES), F32)],
        compiler_params=pltpu.CompilerParams(
            dimension_semantics=("parallel", "parallel", "arbitrary"),
            vmem_limit_bytes=VMEM_LIMIT),
        name="fox_attention",
    )(proj, proj, proj, cum)


def _outproj_kernel(or_ref, op_ref, of_ref, h_ref, w_ref, g_ref, b_ref, o_ref):
    mix = jnp.concatenate([or_ref[...], op_ref[...], of_ref[...]], axis=-1)
    y = ALPHA * h_ref[...] + jnp.dot(mix, w_ref[...], preferred_element_type=F32)
    o_ref[...] = _layer_norm(y, g_ref[...], b_ref[...])


def _outproj(o_r, o_p, o_f, h, w, g, b, tm):
    rows = h.shape[0]
    return pl.pallas_call(
        _outproj_kernel,
        out_shape=jax.ShapeDtypeStruct((rows, D_MODEL), F32),
        grid=(rows // tm,),
        in_specs=[
            pl.BlockSpec((tm, RETV_W), lambda i: (i, 0)),
            pl.BlockSpec((tm, POOL_W), lambda i: (i, 0)),
            pl.BlockSpec((tm, FOX_W), lambda i: (i, 0)),
            pl.BlockSpec((tm, D_MODEL), lambda i: (i, 0)),
            pl.BlockSpec((MIX_W, D_MODEL), lambda i: (0, 0)),
            pl.BlockSpec((1, D_MODEL), lambda i: (0, 0)),
            pl.BlockSpec((1, D_MODEL), lambda i: (0, 0)),
        ],
        out_specs=pl.BlockSpec((tm, D_MODEL), lambda i: (i, 0)),
        compiler_params=pltpu.CompilerParams(dimension_semantics=("parallel",),
                                             vmem_limit_bytes=VMEM_LIMIT),
        name="out_proj_ln",
    )(o_r, o_p, o_f, h, w, g, b)


def _ffn_kernel(h_ref, w1_ref, w3_ref, w2_ref, g_ref, b_ref, o_ref):
    h = h_ref[...]
    xb = h.astype(BF16)
    acc = None
    for c0 in range(0, D_FF, FF_CHUNK):
        a = jnp.dot(xb, w1_ref[:, c0:c0 + FF_CHUNK], preferred_element_type=F32)
        b = jnp.dot(xb, w3_ref[:, c0:c0 + FF_CHUNK], preferred_element_type=F32)
        t = (a * _sigmoid(a) * b).astype(BF16)
        part = jnp.dot(t, w2_ref[c0:c0 + FF_CHUNK, :], preferred_element_type=F32)
        acc = part if acc is None else acc + part
    o_ref[...] = _layer_norm(ALPHA * h + acc, g_ref[...], b_ref[...])


def _ffn(h, w1, w3, w2, g, b, tm):
    rows = h.shape[0]
    resident = dict(pipeline_mode=pl.Buffered(1))
    return pl.pallas_call(
        _ffn_kernel,
        out_shape=jax.ShapeDtypeStruct((rows, D_MODEL), F32),
        grid=(rows // tm,),
        in_specs=[
            pl.BlockSpec((tm, D_MODEL), lambda i: (i, 0)),
            pl.BlockSpec((D_MODEL, D_FF), lambda i: (0, 0), **resident),
            pl.BlockSpec((D_MODEL, D_FF), lambda i: (0, 0), **resident),
            pl.BlockSpec((D_FF, D_MODEL), lambda i: (0, 0), **resident),
            pl.BlockSpec((1, D_MODEL), lambda i: (0, 0)),
            pl.BlockSpec((1, D_MODEL), lambda i: (0, 0)),
        ],
        out_specs=pl.BlockSpec((tm, D_MODEL), lambda i: (i, 0)),
        compiler_params=pltpu.CompilerParams(dimension_semantics=("parallel",),
                                             vmem_limit_bytes=VMEM_LIMIT),
        name="ffn_ln",
    )(h, w1, w3, w2, g, b)


def _pack_w_in(w):
    d = w.shape[0]

    def rot_pack(seg):
        s = seg.reshape(d, RET_HEADS, RET_DK)
        x1 = s[:, :, :RET_HALF].reshape(d, RET_HEADS * RET_HALF)
        x2 = s[:, :, RET_HALF:].reshape(d, RET_HEADS * RET_HALF)
        z = jnp.zeros((d, LANES - RET_HEADS * RET_HALF), w.dtype)
        return jnp.concatenate([x1, z, x2, z], axis=1)

    def head_pad(seg):
        s = seg.reshape(d, RET_HEADS, RET_DV)
        return jnp.pad(s, ((0, 0), (0, 0), (0, RET_HEAD_PAD - RET_DV))).reshape(d, RETV_W)

    o = 0
    segs = []
    for sz in (RET_QK, RET_QK, RET_W, RET_W, POOL_W, FOX_W, FOX_W, FOX_W, FOX_HEADS):
        segs.append(w[:, o:o + sz])
        o += sz
    q_r, k_r, v_r, g_r, u_p, q_f, k_f, v_f, f_l = segs
    packed = jnp.concatenate(
        [rot_pack(q_r), rot_pack(k_r), head_pad(v_r), head_pad(g_r), u_p, q_f, k_f, v_f,
         jnp.pad(f_l, ((0, 0), (0, LANES - FOX_HEADS)))], axis=1)
    return packed.astype(BF16)


def _pack_w_out(w):
    d = w.shape[1]
    w_r = jnp.pad(w[:RET_W].reshape(RET_HEADS, RET_DV, d),
                  ((0, 0), (0, RET_HEAD_PAD - RET_DV), (0, 0))).reshape(RETV_W, d)
    return jnp.concatenate([w_r, w[RET_W:]], axis=0).astype(BF16)


def _pad_heads(vec):
    return jnp.pad(vec.reshape(RET_HEADS, RET_DV), ((0, 0), (0, RET_HEAD_PAD - RET_DV))).reshape(1, RETV_W)


def _block_diag(pw):
    out = jnp.zeros((POOL_W, POOL_W), pw.dtype)
    for g in range(len(POOL_WINDOWS)):
        out = out.at[g * POOL_GROUP:(g + 1) * POOL_GROUP, g * POOL_GROUP:(g + 1) * POOL_GROUP].set(pw[g])
    return out.astype(BF16)


def _retention_tables():
    ch = SEQ_BLOCK
    gamma = (1.0 - 2.0 ** (-5.0 - np.arange(RET_HEADS, dtype=np.float32))).astype(np.float32)
    lg = np.log(gamma).astype(np.float32)
    i = np.arange(ch, dtype=np.float32)
    diff = i[:, None] - i[None, :]
    dm = np.where(diff >= 0, np.exp(lg[:, None, None] * np.maximum(diff, 0.0)), 0.0).astype(np.float32)
    xi = np.exp(lg[:, None] * (i + 1.0)).astype(np.float32)
    zeta = np.exp(lg[:, None] * (ch - 1.0 - i)).astype(np.float32)
    lane = np.arange(2 * LANES)
    within = lane % LANES
    lane_head = np.where(within < RET_HEADS * RET_HALF, within // RET_HALF, -1)
    xiq = np.zeros((ch, 2 * LANES), np.float32)
    zk = np.zeros((ch, 2 * LANES), np.float32)
    bm = np.zeros((2 * LANES, RETV_W), np.float32)
    for h in range(RET_HEADS):
        sel = lane_head == h
        xiq[:, sel] = xi[h][:, None]
        zk[:, sel] = zeta[h][:, None]
        bm[sel, h * LANES:(h + 1) * LANES] = 1.0
    dec = np.repeat(np.exp(lg * ch).astype(np.float32), LANES)[None, :]
    return dict(xiq=jnp.asarray(xiq), zk=jnp.asarray(zk), dm=jnp.asarray(dm),
                dec=jnp.asarray(dec), bm=jnp.asarray(bm))


def _rotary_tables(lp):
    pos = jnp.arange(lp, dtype=F32)
    inv_freq = ROPE_BASE ** (-jnp.arange(RET_HALF, dtype=F32) / RET_HALF)
    ang = pos[:, None] * inv_freq[None, :]
    pad = LANES - RET_HEADS * RET_HALF
    cos = jnp.pad(jnp.tile(jnp.cos(ang), (1, RET_HEADS)), ((0, 0), (0, pad)))
    sin = jnp.pad(jnp.tile(jnp.sin(ang), (1, RET_HEADS)), ((0, 0), (0, pad)))
    return cos, sin


def kernel(x, meta, ln_emb_g, ln_emb_b, w_in, b_f, ret_gn_g, pool_w, pool_scale, w_out, ln1_g, ln1_b,
           w_ffn1, w_ffn3, w_ffn2, ln2_g, ln2_b):
    bsz, seq, d = x.shape
    assert d == D_MODEL and seq % SEQ_BLOCK == 0
    depth = w_in.shape[0]
    assert depth == DEPTH
    lp = seq + SEQ_BLOCK
    rows = bsz * lp
    tm = ROW_TILE if rows % ROW_TILE == 0 else SEQ_BLOCK

    consts = _retention_tables()
    consts["cos"], consts["sin"] = _rotary_tables(lp)

    h = _embed(x, meta, ln_emb_g, ln_emb_b, lp).reshape(rows, d)
    for l in range(depth):
        lw = dict(gn_g=_pad_heads(ret_gn_g[l]),
                  b_f=jnp.pad(b_f[l], (0, LANES - FOX_HEADS)).reshape(1, LANES),
                  pool_w=_block_diag(pool_w[l]),
                  pool_scale=pool_scale[l].reshape(1, POOL_W))
        proj, flog = _inproj(h, _pack_w_in(w_in[l]), tm)
        o_r, o_p, cum = _seq_mix(proj, flog, consts, lw, bsz, lp)
        o_f = _fox(proj, cum, bsz, lp)
        h = _outproj(o_r, o_p, o_f, h, _pack_w_out(w_out[l]),
                     ln1_g[l].reshape(1, d), ln1_b[l].reshape(1, d), tm)
        h = _ffn(h, w_ffn1[l].astype(BF16), w_ffn3[l].astype(BF16), w_ffn2[l].astype(BF16),
                 ln2_g[l].reshape(1, d), ln2_b[l].reshape(1, d), tm)
    return h.reshape(bsz, lp, d)[:, N_META:N_META + seq]
```

```python
import functools

import numpy as np
import jax
import jax.numpy as jnp
from jax import lax
from jax.experimental import pallas as pl
from jax.experimental.pallas import tpu as pltpu

F32 = jnp.float32
BF16 = jnp.bfloat16

D_MODEL = 1024
N_META = 16
RET_HEADS = 4
RET_DK = 48
RET_HALF = RET_DK // 2
RET_DV = 96
RET_QK = RET_HEADS * RET_DK
RET_W = RET_HEADS * RET_DV
POOL_WINDOWS = (2, 4, 8, 16)
POOL_GROUP = 64
POOL_W = len(POOL_WINDOWS) * POOL_GROUP
FOX_HEADS = 6
FOX_DH = 64
FOX_W = FOX_HEADS * FOX_DH
D_FF = 2816
ROPE_BASE = 10000.0
LN_EPS = 1e-5
NEG_INF = -1e30
DEPTH = 2
ALPHA = (2.0 * DEPTH) ** 0.25

LANES = 128
MXU_DIM = 256
SEQ_BLOCK = 256
ROW_TILE = 512
VMEM_LIMIT = 56 * 1024 * 1024

RET_HEAD_PAD = LANES
QR_OFF = 0
KR_OFF = 2 * LANES
VR_OFF = 4 * LANES
GR_OFF = VR_OFF + RET_HEADS * RET_HEAD_PAD
UP_OFF = GR_OFF + RET_HEADS * RET_HEAD_PAD
QF_OFF = UP_OFF + POOL_W
KF_OFF = QF_OFF + FOX_W
VF_OFF = KF_OFF + FOX_W
FL_OFF = VF_OFF + FOX_W
RET_COLS = QF_OFF
FOX_COLS = 3 * FOX_W
N_PACK = FL_OFF + LANES
RETV_W = RET_HEADS * RET_HEAD_PAD
MIX_W = RETV_W + POOL_W + FOX_W
PROJ_CHUNK = 512
FF_CHUNK = 256


def _layer_norm(x, g, b):
    mu = jnp.mean(x, axis=-1, keepdims=True)
    d = x - mu
    var = jnp.mean(d * d, axis=-1, keepdims=True)
    return d * lax.rsqrt(var + LN_EPS) * g + b


def _sigmoid(x):
    return 1.0 / (1.0 + jnp.exp(-x))


def _embed_kernel(meta_ref, xm_ref, xe_ref, g_ref, b_ref, o_ref):
    j = pl.program_id(1)
    last = pl.num_programs(1) - 1
    top = jnp.where(j == 0, meta_ref[...], xe_ref[...])
    body = jnp.where(j == last, 0.0, xm_ref[0:SEQ_BLOCK - N_META, :])
    rows = jnp.concatenate([top, body], axis=0)
    o_ref[...] = _layer_norm(rows, g_ref[...], b_ref[...])


def _embed(x, meta, g, b, lp):
    bsz, seq, d = x.shape
    nblk = lp // SEQ_BLOCK
    n_xblk = seq // SEQ_BLOCK
    per = SEQ_BLOCK // N_META
    return pl.pallas_call(
        _embed_kernel,
        out_shape=jax.ShapeDtypeStruct((bsz, lp, d), F32),
        grid=(bsz, nblk),
        in_specs=[
            pl.BlockSpec((N_META, d), lambda bb, j: (0, 0)),
            pl.BlockSpec((None, SEQ_BLOCK, d), lambda bb, j: (bb, jnp.minimum(j, n_xblk - 1), 0)),
            pl.BlockSpec((None, N_META, d), lambda bb, j: (bb, jnp.maximum(per * j - 1, 0), 0)),
            pl.BlockSpec((1, d), lambda bb, j: (0, 0)),
            pl.BlockSpec((1, d), lambda bb, j: (0, 0)),
        ],
        out_specs=pl.BlockSpec((None, SEQ_BLOCK, d), lambda bb, j: (bb, j, 0)),
        compiler_params=pltpu.CompilerParams(dimension_semantics=("parallel", "arbitrary")),
        name="embed_ln",
    )(meta, x, x, g.reshape(1, d), b.reshape(1, d))


def _inproj_kernel(h_ref, w_ref, ret_ref, fox_ref, flog_ref):
    xb = h_ref[...].astype(BF16)

    def cols(c0, width):
        return jnp.dot(xb, w_ref[:, c0:c0 + width], preferred_element_type=F32)

    for c0 in range(0, RET_COLS, PROJ_CHUNK):
        width = min(PROJ_CHUNK, RET_COLS - c0)
        ret_ref[:, c0:c0 + width] = cols(c0, width).astype(BF16)
    for c0 in range(0, FOX_COLS - LANES, PROJ_CHUNK):
        fox_ref[:, c0:c0 + PROJ_CHUNK] = cols(QF_OFF + c0, PROJ_CHUNK).astype(BF16)
    tail = cols(FL_OFF - LANES, 2 * LANES)
    fox_ref[:, FOX_COLS - LANES:] = tail[:, :LANES].astype(BF16)
    flog_ref[...] = tail[:, LANES:]


def _inproj(h, w, tm):
    rows = h.shape[0]
    return pl.pallas_call(
        _inproj_kernel,
        out_shape=(jax.ShapeDtypeStruct((rows, RET_COLS), BF16),
                   jax.ShapeDtypeStruct((rows, FOX_COLS), BF16),
                   jax.ShapeDtypeStruct((rows, LANES), F32)),
        grid=(rows // tm,),
        in_specs=[
            pl.BlockSpec((tm, D_MODEL), lambda i: (i, 0)),
            pl.BlockSpec((D_MODEL, N_PACK), lambda i: (0, 0)),
        ],
        out_specs=(pl.BlockSpec((tm, RET_COLS), lambda i: (i, 0)),
                   pl.BlockSpec((tm, FOX_COLS), lambda i: (i, 0)),
                   pl.BlockSpec((tm, LANES), lambda i: (i, 0))),
        compiler_params=pltpu.CompilerParams(dimension_semantics=("parallel",),
                                             vmem_limit_bytes=VMEM_LIMIT),
        name="in_proj",
    )(h, w)


def _seq_kernel(qk_ref, v_ref, g_ref, u_ref, fl_ref, cos_ref, sin_ref,
                xiq_ref, zk_ref, dm_ref, dec_ref, bm_ref, gng_ref, bf_ref, pw_ref, ps_ref,
                or_ref, op_ref, c_ref,
                state_sc, tail_sc, carry_sc):
    j = pl.program_id(1)
    ch = SEQ_BLOCK

    @pl.when(j == 0)
    def _():
        state_sc[...] = jnp.zeros_like(state_sc)
        tail_sc[...] = jnp.zeros_like(tail_sc)
        carry_sc[...] = jnp.zeros_like(carry_sc)

    qk = qk_ref[...].astype(F32)
    cs = cos_ref[...]
    sn = sin_ref[...]
    q1, q2 = qk[:, 0:LANES], qk[:, LANES:2 * LANES]
    k1, k2 = qk[:, 2 * LANES:3 * LANES], qk[:, 3 * LANES:4 * LANES]
    qr = jnp.concatenate([q1 * cs - q2 * sn, q1 * sn + q2 * cs], axis=-1)
    kr = jnp.concatenate([k1 * cs - k2 * sn, k1 * sn + k2 * cs], axis=-1) * (RET_DK ** -0.5)
    qb = qr.astype(BF16)
    qx = (qr * xiq_ref[...]).astype(BF16)
    kb = kr.astype(BF16)
    kz = (kr * zk_ref[...]).astype(BF16)
    v = v_ref[...]
    st = state_sc[...]
    cross = jnp.dot(qx, st.astype(BF16), preferred_element_type=F32)
    qlane = lax.broadcasted_iota(jnp.int32, (1, 2 * LANES), 1)
    qhead = jnp.where(qlane % LANES < RET_HEADS * RET_HALF, (qlane % LANES) // RET_HALF, RET_HEADS)
    inner = []
    for h in range(RET_HEADS):
        qh = jnp.where(qhead == h, qb, jnp.zeros_like(qb))
        s = lax.dot_general(qh, kb, (((1,), (1,)), ((), ())), preferred_element_type=F32)
        p = (s * dm_ref[h]).astype(BF16)
        inner.append(jnp.dot(p, v[:, h * LANES:(h + 1) * LANES], preferred_element_type=F32))
    o = jnp.concatenate(inner, axis=-1) + cross
    kv = lax.dot_general(kz, v, (((0,), (0,)), ((), ())), preferred_element_type=F32)
    state_sc[...] = st * dec_ref[...] + kv * bm_ref[...]

    vlane = lax.broadcasted_iota(jnp.int32, (1, LANES), 1) < RET_DV
    normed = []
    for h in range(RET_HEADS):
        xh = o[:, h * LANES:(h + 1) * LANES]
        mu = jnp.sum(xh, axis=-1, keepdims=True) * (1.0 / RET_DV)
        d = jnp.where(vlane, xh - mu, 0.0)
        var = jnp.sum(d * d, axis=-1, keepdims=True) * (1.0 / RET_DV)
        normed.append(d * lax.rsqrt(var + LN_EPS))
    y = jnp.concatenate(normed, axis=-1) * gng_ref[...]
    gate = g_ref[...].astype(F32)
    or_ref[...] = (gate * _sigmoid(gate) * y).astype(BF16)

    u = u_ref[...].astype(F32)
    tail_rows = tail_sc.shape[0]
    ext = jnp.concatenate([tail_sc[...], u], axis=0)
    tail_sc[...] = u[ch - tail_rows:, :]
    e2 = ext + pltpu.roll(ext, 1, 0)
    e4 = e2 + pltpu.roll(e2, 2, 0)
    e8 = e4 + pltpu.roll(e4, 4, 0)
    e16 = e8 + pltpu.roll(e8, 8, 0)
    glane = lax.broadcasted_iota(jnp.int32, (1, POOL_W), 1) // POOL_GROUP
    win = jnp.where(glane == 0, e2, jnp.where(glane == 1, e4, jnp.where(glane == 2, e8, e16)))
    win = win[tail_rows:, :]
    wlen = jnp.where(glane == 0, 2, jnp.where(glane == 1, 4, jnp.where(glane == 2, 8, 16)))
    pos = j * ch + lax.broadcasted_iota(jnp.int32, (ch, POOL_W), 0)
    cnt = jnp.minimum(pos + 1, wlen).astype(F32)
    pooled = (win / cnt - u).astype(BF16)
    yp = jnp.dot(pooled, pw_ref[...], preferred_element_type=F32) * ps_ref[...]
    op_ref[...] = yp.astype(BF16)

    z = fl_ref[...] + bf_ref[...]
    logf = jnp.minimum(z, 0.0) - jnp.log1p(jnp.exp(-jnp.abs(z)))
    row = lax.broadcasted_iota(jnp.int32, (ch, LANES), 0)
    sh = 1
    while sh < ch:
        logf = logf + jnp.where(row >= sh, pltpu.roll(logf, sh, 0), 0.0)
        sh *= 2
    c = logf + carry_sc[0:1, :]
    carry_sc[...] = jnp.broadcast_to(c[ch - 1:ch, :], carry_sc.shape)
    c_ref[...] = c.T[0:8, :]


def _seq_mix(proj, flog, consts, lw, bsz, lp):
    ch = SEQ_BLOCK
    nc = lp // ch
    rows = bsz * lp

    def rowblk(width, colblk):
        return pl.BlockSpec((ch, width), lambda bb, j: (bb * nc + j, colblk))

    def const(shape):
        nd = len(shape)
        return pl.BlockSpec(shape, lambda bb, j: (0,) * nd)

    in_specs = [
        rowblk(4 * LANES, QR_OFF // (4 * LANES)),
        rowblk(RETV_W, VR_OFF // RETV_W),
        rowblk(RETV_W, GR_OFF // RETV_W),
        rowblk(POOL_W, UP_OFF // POOL_W),
        pl.BlockSpec((ch, LANES), lambda bb, j: (bb * nc + j, 0)),
        pl.BlockSpec((ch, LANES), lambda bb, j: (j, 0)),
        pl.BlockSpec((ch, LANES), lambda bb, j: (j, 0)),
        const((ch, 2 * LANES)), const((ch, 2 * LANES)), const((RET_HEADS, ch, ch)),
        const((1, RETV_W)), const((2 * LANES, RETV_W)),
        const((1, RETV_W)), const((1, LANES)), const((POOL_W, POOL_W)), const((1, POOL_W)),
    ]
    out_shape = (jax.ShapeDtypeStruct((rows, RETV_W), BF16),
                 jax.ShapeDtypeStruct((rows, POOL_W), BF16),
                 jax.ShapeDtypeStruct((bsz, 8, lp), F32))
    out_specs = (pl.BlockSpec((ch, RETV_W), lambda bb, j: (bb * nc + j, 0)),
                 pl.BlockSpec((ch, POOL_W), lambda bb, j: (bb * nc + j, 0)),
                 pl.BlockSpec((None, 8, ch), lambda bb, j: (bb, 0, j)))
    return pl.pallas_call(
        _seq_kernel,
        out_shape=out_shape,
        grid=(bsz, nc),
        in_specs=in_specs,
        out_specs=out_specs,
        scratch_shapes=[pltpu.VMEM((2 * LANES, RETV_W), F32),
                        pltpu.VMEM((16, POOL_W), F32),
                        pltpu.VMEM((8, LANES), F32)],
        compiler_params=pltpu.CompilerParams(dimension_semantics=("parallel", "arbitrary"),
                                             vmem_limit_bytes=VMEM_LIMIT),
        name="seq_mixers",
    )(proj, proj, proj, proj, flog, consts["cos"], consts["sin"],
      consts["xiq"], consts["zk"], consts["dm"], consts["dec"], consts["bm"],
      lw["gn_g"], lw["b_f"], lw["pool_w"], lw["pool_scale"])


FOX_ROWS = 64


def _fox_kernel(q_ref, k_ref, v_ref, c_ref, o_ref,
                va_sc, vb_sc, qh_sc, s_sc, p_sc, al_sc, m_sc, acc_sc):
    i = pl.program_id(1)
    tq = SEQ_BLOCK
    tk = SEQ_BLOCK
    half = lax.broadcasted_iota(jnp.int32, (1, LANES), 1) < FOX_DH
    half_w = jnp.concatenate([half] * (FOX_HEADS // 2), axis=-1)

    @pl.when(i == 0)
    def _():
        vv = v_ref[...]
        one = jnp.ones_like(vv)
        va_sc[...] = jnp.where(half_w, vv, one)
        vb_sc[...] = jnp.where(half_w, one, vv)

    q = q_ref[...] * jnp.asarray(FOX_DH ** -0.5, BF16)
    zero = jnp.zeros_like(q)
    qa = jnp.where(half_w, q, zero)
    qb = jnp.where(half_w, zero, q)
    for h in range(FOX_HEADS):
        src = qa if h % 2 == 0 else qb
        qh_sc[h] = src[:, (h // 2) * LANES:(h // 2 + 1) * LANES]
    m_sc[...] = jnp.full(m_sc.shape, NEG_INF, F32)
    acc_sc[...] = jnp.zeros_like(acc_sc)

    def step(jb, masked):
        ks = pl.multiple_of(jb * tk, tk)
        for h in range(FOX_HEADS):
            lo = (h // 2) * LANES
            kblk = k_ref[pl.ds(ks, tk), lo:lo + LANES]
            s_sc[h] = lax.dot_general(qh_sc[h], kblk, (((1,), (1,)), ((), ())),
                                      preferred_element_type=F32)
        for h in range(FOX_HEADS):
            crow = c_ref[h:h + 1, pl.ds(ks, tk)]
            for r0 in range(0, tq, FOX_ROWS):
                s = s_sc[h, r0:r0 + FOX_ROWS, :] - crow
                if masked:
                    r_id = r0 + lax.broadcasted_iota(jnp.int32, (FOX_ROWS, tk), 0)
                    c_id = lax.broadcasted_iota(jnp.int32, (FOX_ROWS, tk), 1)
                    s = jnp.where(c_id <= r_id, s, NEG_INF)
                m_prev = m_sc[h, r0:r0 + FOX_ROWS, :]
                m_new = jnp.maximum(m_prev, jnp.max(s, axis=-1, keepdims=True))
                p = jnp.exp(s - jnp.concatenate([m_new] * (tk // LANES), axis=-1))
                p_sc[h, r0:r0 + FOX_ROWS, :] = p.astype(BF16)
                al_sc[h, r0:r0 + FOX_ROWS, :] = jnp.exp(m_prev - m_new)
                m_sc[h, r0:r0 + FOX_ROWS, :] = m_new
        for h in range(FOX_HEADS):
            lo = (h // 2) * LANES
            v_sc = va_sc if h % 2 == 0 else vb_sc
            pv = jnp.dot(p_sc[h], v_sc[pl.ds(ks, tk), lo:lo + LANES], preferred_element_type=F32)
            acc_sc[h] = al_sc[h] * acc_sc[h] + pv

    def body(jb, carry):
        step(jb, False)
        return carry

    lax.fori_loop(0, i, body, 0)
    step(i, True)

    outs = []
    for pr in range(FOX_HEADS // 2):
        acc_a = acc_sc[2 * pr]
        acc_b = acc_sc[2 * pr + 1]
        outs.append(jnp.where(half, acc_a / pltpu.roll(acc_a, FOX_DH, 1),
                              acc_b / pltpu.roll(acc_b, FOX_DH, 1)))
    o_ref[...] = jnp.concatenate(outs, axis=-1).astype(BF16)


def _fox(qkv, cum, bsz, lp):
    tq = SEQ_BLOCK
    nq = lp // tq
    rows = bsz * lp
    return pl.pallas_call(
        _fox_kernel,
        out_shape=jax.ShapeDtypeStruct((rows, FOX_W), BF16),
        grid=(bsz, nq),
        in_specs=[
            pl.BlockSpec((tq, FOX_W), lambda bb, i: (bb * nq + i, 0)),
            pl.BlockSpec((lp, FOX_W), lambda bb, i: (bb, 1)),
            pl.BlockSpec((lp, FOX_W), lambda bb, i: (bb, 2)),
            pl.BlockSpec((None, 8, lp), lambda bb, i: (bb, 0, 0)),
        ],
        out_specs=pl.BlockSpec((tq, FOX_W), lambda bb, i: (bb * nq + i, 0)),
        scratch_shapes=[pltpu.VMEM((lp, FOX_W), BF16),
                        pltpu.VMEM((lp, FOX_W), BF16),
                        pltpu.VMEM((FOX_HEADS, tq, LANES), BF16),
                        pltpu.VMEM((FOX_HEADS, tq, SEQ_BLOCK), F32),
                        pltpu.VMEM((FOX_HEADS, tq, SEQ_BLOCK), BF16),
                        pltpu.VMEM((FOX_HEADS, tq, LANES), F32),
                        pltpu.VMEM((FOX_HEADS, tq, LANES), F32),
                        pltpu.VMEM((FOX_HEADS, tq, LANES), F32)],
        compiler_params=pltpu.CompilerParams(
            dimension_semantics=("parallel", "arbitrary"),
            vmem_limit_bytes=VMEM_LIMIT),
        name="fox_attention",
    )(qkv, qkv, qkv, cum)


def _outproj_kernel(or_ref, op_ref, of_ref, h_ref, w_ref, g_ref, b_ref, o_ref):
    mix = jnp.concatenate([or_ref[...], op_ref[...], of_ref[...]], axis=-1)
    y = ALPHA * h_ref[...] + jnp.dot(mix, w_ref[...], preferred_element_type=F32)
    o_ref[...] = _layer_norm(y, g_ref[...], b_ref[...])


def _outproj(o_r, o_p, o_f, h, w, g, b, tm):
    rows = h.shape[0]
    return pl.pallas_call(
        _outproj_kernel,
        out_shape=jax.ShapeDtypeStruct((rows, D_MODEL), F32),
        grid=(rows // tm,),
        in_specs=[
            pl.BlockSpec((tm, RETV_W), lambda i: (i, 0)),
            pl.BlockSpec((tm, POOL_W), lambda i: (i, 0)),
            pl.BlockSpec((tm, FOX_W), lambda i: (i, 0)),
            pl.BlockSpec((tm, D_MODEL), lambda i: (i, 0)),
            pl.BlockSpec((MIX_W, D_MODEL), lambda i: (0, 0)),
            pl.BlockSpec((1, D_MODEL), lambda i: (0, 0)),
            pl.BlockSpec((1, D_MODEL), lambda i: (0, 0)),
        ],
        out_specs=pl.BlockSpec((tm, D_MODEL), lambda i: (i, 0)),
        compiler_params=pltpu.CompilerParams(dimension_semantics=("parallel",),
                                             vmem_limit_bytes=VMEM_LIMIT),
        name="out_proj_ln",
    )(o_r, o_p, o_f, h, w, g, b)


def _ffn_kernel(h_ref, w1_ref, w3_ref, w2_ref, g_ref, b_ref, o_ref):
    h = h_ref[...]
    xb = h.astype(BF16)
    acc = None
    for c0 in range(0, D_FF, FF_CHUNK):
        a = jnp.dot(xb, w1_ref[:, c0:c0 + FF_CHUNK], preferred_element_type=F32)
        b = jnp.dot(xb, w3_ref[:, c0:c0 + FF_CHUNK], preferred_element_type=F32)
        t = (a * _sigmoid(a) * b).astype(BF16)
        part = jnp.dot(t, w2_ref[c0:c0 + FF_CHUNK, :], preferred_element_type=F32)
        acc = part if acc is None else acc + part
    o_ref[...] = _layer_norm(ALPHA * h + acc, g_ref[...], b_ref[...])


def _ffn(h, w1, w3, w2, g, b, tm):
    rows = h.shape[0]
    resident = dict(pipeline_mode=pl.Buffered(1))
    return pl.pallas_call(
        _ffn_kernel,
        out_shape=jax.ShapeDtypeStruct((rows, D_MODEL), F32),
        grid=(rows // tm,),
        in_specs=[
            pl.BlockSpec((tm, D_MODEL), lambda i: (i, 0)),
            pl.BlockSpec((D_MODEL, D_FF), lambda i: (0, 0), **resident),
            pl.BlockSpec((D_MODEL, D_FF), lambda i: (0, 0), **resident),
            pl.BlockSpec((D_FF, D_MODEL), lambda i: (0, 0), **resident),
            pl.BlockSpec((1, D_MODEL), lambda i: (0, 0)),
            pl.BlockSpec((1, D_MODEL), lambda i: (0, 0)),
        ],
        out_specs=pl.BlockSpec((tm, D_MODEL), lambda i: (i, 0)),
        compiler_params=pltpu.CompilerParams(dimension_semantics=("parallel",),
                                             vmem_limit_bytes=VMEM_LIMIT),
        name="ffn_ln",
    )(h, w1, w3, w2, g, b)


def _pack_w_in(w):
    d = w.shape[0]

    def rot_pack(seg):
        s = seg.reshape(d, RET_HEADS, RET_DK)
        x1 = s[:, :, :RET_HALF].reshape(d, RET_HEADS * RET_HALF)
        x2 = s[:, :, RET_HALF:].reshape(d, RET_HEADS * RET_HALF)
        z = jnp.zeros((d, LANES - RET_HEADS * RET_HALF), w.dtype)
        return jnp.concatenate([x1, z, x2, z], axis=1)

    def head_pad(seg):
        s = seg.reshape(d, RET_HEADS, RET_DV)
        return jnp.pad(s, ((0, 0), (0, 0), (0, RET_HEAD_PAD - RET_DV))).reshape(d, RETV_W)

    o = 0
    segs = []
    for sz in (RET_QK, RET_QK, RET_W, RET_W, POOL_W, FOX_W, FOX_W, FOX_W, FOX_HEADS):
        segs.append(w[:, o:o + sz])
        o += sz
    q_r, k_r, v_r, g_r, u_p, q_f, k_f, v_f, f_l = segs
    packed = jnp.concatenate(
        [rot_pack(q_r), rot_pack(k_r), head_pad(v_r), head_pad(g_r), u_p, q_f, k_f, v_f,
         jnp.pad(f_l, ((0, 0), (0, LANES - FOX_HEADS)))], axis=1)
    return packed.astype(BF16)


def _pack_w_out(w):
    d = w.shape[1]
    w_r = jnp.pad(w[:RET_W].reshape(RET_HEADS, RET_DV, d),
                  ((0, 0), (0, RET_HEAD_PAD - RET_DV), (0, 0))).reshape(RETV_W, d)
    return jnp.concatenate([w_r, w[RET_W:]], axis=0).astype(BF16)


def _pad_heads(vec):
    return jnp.pad(vec.reshape(RET_HEADS, RET_DV), ((0, 0), (0, RET_HEAD_PAD - RET_DV))).reshape(1, RETV_W)


def _block_diag(pw):
    out = jnp.zeros((POOL_W, POOL_W), pw.dtype)
    for g in range(len(POOL_WINDOWS)):
        out = out.at[g * POOL_GROUP:(g + 1) * POOL_GROUP, g * POOL_GROUP:(g + 1) * POOL_GROUP].set(pw[g])
    return out.astype(BF16)


def _retention_tables():
    ch = SEQ_BLOCK
    gamma = (1.0 - 2.0 ** (-5.0 - np.arange(RET_HEADS, dtype=np.float32))).astype(np.float32)
    lg = np.log(gamma).astype(np.float32)
    i = np.arange(ch, dtype=np.float32)
    diff = i[:, None] - i[None, :]
    dm = np.where(diff >= 0, np.exp(lg[:, None, None] * np.maximum(diff, 0.0)), 0.0).astype(np.float32)
    xi = np.exp(lg[:, None] * (i + 1.0)).astype(np.float32)
    zeta = np.exp(lg[:, None] * (ch - 1.0 - i)).astype(np.float32)
    lane = np.arange(2 * LANES)
    within = lane % LANES
    lane_head = np.where(within < RET_HEADS * RET_HALF, within // RET_HALF, -1)
    xiq = np.zeros((ch, 2 * LANES), np.float32)
    zk = np.zeros((ch, 2 * LANES), np.float32)
    bm = np.zeros((2 * LANES, RETV_W), np.float32)
    for h in range(RET_HEADS):
        sel = lane_head == h
        xiq[:, sel] = xi[h][:, None]
        zk[:, sel] = zeta[h][:, None]
        bm[sel, h * LANES:(h + 1) * LANES] = 1.0
    dec = np.repeat(np.exp(lg * ch).astype(np.float32), LANES)[None, :]
    return dict(xiq=jnp.asarray(xiq), zk=jnp.asarray(zk), dm=jnp.asarray(dm),
                dec=jnp.asarray(dec), bm=jnp.asarray(bm))


def _rotary_tables(lp):
    pos = jnp.arange(lp, dtype=F32)
    inv_freq = ROPE_BASE ** (-jnp.arange(RET_HALF, dtype=F32) / RET_HALF)
    ang = pos[:, None] * inv_freq[None, :]
    pad = LANES - RET_HEADS * RET_HALF
    cos = jnp.pad(jnp.tile(jnp.cos(ang), (1, RET_HEADS)), ((0, 0), (0, pad)))
    sin = jnp.pad(jnp.tile(jnp.sin(ang), (1, RET_HEADS)), ((0, 0), (0, pad)))
    return cos, sin


def kernel(x, meta, ln_emb_g, ln_emb_b, w_in, b_f, ret_gn_g, pool_w, pool_scale, w_out, ln1_g, ln1_b,
           w_ffn1, w_ffn3, w_ffn2, ln2_g, ln2_b):
    bsz, seq, d = x.shape
    assert d == D_MODEL and seq % SEQ_BLOCK == 0
    depth = w_in.shape[0]
    assert depth == DEPTH
    lp = seq + SEQ_BLOCK
    rows = bsz * lp
    tm = ROW_TILE if rows % ROW_TILE == 0 else SEQ_BLOCK

    consts = _retention_tables()
    consts["cos"], consts["sin"] = _rotary_tables(lp)

    h = _embed(x, meta, ln_emb_g, ln_emb_b, lp).reshape(rows, d)
    for l in range(depth):
        lw = dict(gn_g=_pad_heads(ret_gn_g[l]),
                  b_f=jnp.pad(b_f[l], (0, LANES - FOX_HEADS)).reshape(1, LANES),
                  pool_w=_block_diag(pool_w[l]),
                  pool_scale=pool_scale[l].reshape(1, POOL_W))
        ret, qkv, flog = _inproj(h, _pack_w_in(w_in[l]), tm)
        o_r, o_p, cum = _seq_mix(ret, flog, consts, lw, bsz, lp)
        o_f = _fox(qkv, cum, bsz, lp)
        h = _outproj(o_r, o_p, o_f, h, _pack_w_out(w_out[l]),
                     ln1_g[l].reshape(1, d), ln1_b[l].reshape(1, d), tm)
        h = _ffn(h, w_ffn1[l].astype(BF16), w_ffn3[l].astype(BF16), w_ffn2[l].astype(BF16),
                 ln2_g[l].reshape(1, d), ln2_b[l].reshape(1, d), tm)
    return h.reshape(bsz, lp, d)[:, N_META:N_META + seq]
```

```python
import functools

import numpy as np
import jax
import jax.numpy as jnp
from jax import lax
from jax.experimental import pallas as pl
from jax.experimental.pallas import tpu as pltpu

F32 = jnp.float32
BF16 = jnp.bfloat16

D_MODEL = 1024
N_META = 16
RET_HEADS = 4
RET_DK = 48
RET_HALF = RET_DK // 2
RET_DV = 96
RET_QK = RET_HEADS * RET_DK
RET_W = RET_HEADS * RET_DV
POOL_WINDOWS = (2, 4, 8, 16)
POOL_GROUP = 64
POOL_W = len(POOL_WINDOWS) * POOL_GROUP
FOX_HEADS = 6
FOX_DH = 64
FOX_W = FOX_HEADS * FOX_DH
D_FF = 2816
ROPE_BASE = 10000.0
LN_EPS = 1e-5
NEG_INF = -1e30
DEPTH = 2
ALPHA = (2.0 * DEPTH) ** 0.25
LOG2E = 1.4426950408889634

LANES = 128
MXU_DIM = 256
SEQ_BLOCK = 256
ROW_TILE = 512
VMEM_LIMIT = 56 * 1024 * 1024

RET_HEAD_PAD = LANES
QR_OFF = 0
KR_OFF = 2 * LANES
VR_OFF = 4 * LANES
GR_OFF = VR_OFF + RET_HEADS * RET_HEAD_PAD
UP_OFF = GR_OFF + RET_HEADS * RET_HEAD_PAD
QF_OFF = UP_OFF + POOL_W
KF_OFF = QF_OFF + FOX_W
VF_OFF = KF_OFF + FOX_W
FL_OFF = VF_OFF + FOX_W
RET_COLS = QF_OFF
FOX_COLS = 3 * FOX_W
N_PACK = FL_OFF + LANES
RETV_W = RET_HEADS * RET_HEAD_PAD
MIX_W = RETV_W + POOL_W + FOX_W
PROJ_CHUNK = 512
FF_CHUNK = 256


def _layer_norm(x, g, b):
    mu = jnp.mean(x, axis=-1, keepdims=True)
    d = x - mu
    var = jnp.mean(d * d, axis=-1, keepdims=True)
    return d * lax.rsqrt(var + LN_EPS) * g + b


def _sigmoid(x):
    return 1.0 / (1.0 + jnp.exp(-x))


def _embed_kernel(meta_ref, xm_ref, xe_ref, g_ref, b_ref, o_ref):
    j = pl.program_id(1)
    last = pl.num_programs(1) - 1
    top = jnp.where(j == 0, meta_ref[...], xe_ref[...])
    body = jnp.where(j == last, 0.0, xm_ref[0:SEQ_BLOCK - N_META, :])
    rows = jnp.concatenate([top, body], axis=0)
    o_ref[...] = _layer_norm(rows, g_ref[...], b_ref[...])


def _embed(x, meta, g, b, lp):
    bsz, seq, d = x.shape
    nblk = lp // SEQ_BLOCK
    n_xblk = seq // SEQ_BLOCK
    per = SEQ_BLOCK // N_META
    return pl.pallas_call(
        _embed_kernel,
        out_shape=jax.ShapeDtypeStruct((bsz, lp, d), F32),
        grid=(bsz, nblk),
        in_specs=[
            pl.BlockSpec((N_META, d), lambda bb, j: (0, 0)),
            pl.BlockSpec((None, SEQ_BLOCK, d), lambda bb, j: (bb, jnp.minimum(j, n_xblk - 1), 0)),
            pl.BlockSpec((None, N_META, d), lambda bb, j: (bb, jnp.maximum(per * j - 1, 0), 0)),
            pl.BlockSpec((1, d), lambda bb, j: (0, 0)),
            pl.BlockSpec((1, d), lambda bb, j: (0, 0)),
        ],
        out_specs=pl.BlockSpec((None, SEQ_BLOCK, d), lambda bb, j: (bb, j, 0)),
        compiler_params=pltpu.CompilerParams(dimension_semantics=("parallel", "arbitrary")),
        name="embed_ln",
    )(meta, x, x, g.reshape(1, d), b.reshape(1, d))


def _inproj_kernel(h_ref, w_ref, cs_ref, ret_ref, fox_ref, flog_ref):
    xb = h_ref[...].astype(BF16)

    def cols(c0, width):
        r = jnp.dot(xb, w_ref[:, c0:c0 + width], preferred_element_type=F32)
        return r * cs_ref[:, c0:c0 + width]

    for c0 in range(0, RET_COLS, PROJ_CHUNK):
        width = min(PROJ_CHUNK, RET_COLS - c0)
        ret_ref[:, c0:c0 + width] = cols(c0, width).astype(BF16)
    for c0 in range(0, FOX_COLS - LANES, PROJ_CHUNK):
        fox_ref[:, c0:c0 + PROJ_CHUNK] = cols(QF_OFF + c0, PROJ_CHUNK).astype(BF16)
    tail = cols(FL_OFF - LANES, 2 * LANES)
    fox_ref[:, FOX_COLS - LANES:] = tail[:, :LANES].astype(BF16)
    flog_ref[...] = tail[:, LANES:]


def _inproj_col_scale():
    cs = np.ones((1, N_PACK), np.float32)
    cs[0, KR_OFF:KR_OFF + 2 * LANES] = RET_DK ** -0.5
    cs[0, QF_OFF:QF_OFF + FOX_W] = FOX_DH ** -0.5 * LOG2E
    return jnp.asarray(cs)


def _inproj(h, w, tm):
    rows = h.shape[0]
    return pl.pallas_call(
        _inproj_kernel,
        out_shape=(jax.ShapeDtypeStruct((rows, RET_COLS), BF16),
                   jax.ShapeDtypeStruct((rows, FOX_COLS), BF16),
                   jax.ShapeDtypeStruct((rows, LANES), F32)),
        grid=(rows // tm,),
        in_specs=[
            pl.BlockSpec((tm, D_MODEL), lambda i: (i, 0)),
            pl.BlockSpec((D_MODEL, N_PACK), lambda i: (0, 0)),
            pl.BlockSpec((1, N_PACK), lambda i: (0, 0)),
        ],
        out_specs=(pl.BlockSpec((tm, RET_COLS), lambda i: (i, 0)),
                   pl.BlockSpec((tm, FOX_COLS), lambda i: (i, 0)),
                   pl.BlockSpec((tm, LANES), lambda i: (i, 0))),
        compiler_params=pltpu.CompilerParams(dimension_semantics=("parallel",),
                                             vmem_limit_bytes=VMEM_LIMIT),
        name="in_proj",
    )(h, w, _inproj_col_scale())


def _seq_kernel(qk_ref, v_ref, g_ref, u_ref, fl_ref, cos_ref, sin_ref,
                xiq_ref, zk_ref, dm_ref, dec_ref, bm_ref, gng_ref, bf_ref, pw_ref, ps_ref,
                or_ref, op_ref, c_ref,
                state_sc, tail_sc, carry_sc):
    j = pl.program_id(1)
    ch = SEQ_BLOCK

    @pl.when(j == 0)
    def _():
        state_sc[...] = jnp.zeros_like(state_sc)
        tail_sc[...] = jnp.zeros_like(tail_sc)
        carry_sc[...] = jnp.zeros_like(carry_sc)

    qk = qk_ref[...].astype(F32)
    cs = cos_ref[...]
    sn = sin_ref[...]
    q1, q2 = qk[:, 0:LANES], qk[:, LANES:2 * LANES]
    k1, k2 = qk[:, 2 * LANES:3 * LANES], qk[:, 3 * LANES:4 * LANES]
    qr = jnp.concatenate([q1 * cs - q2 * sn, q1 * sn + q2 * cs], axis=-1)
    kr = jnp.concatenate([k1 * cs - k2 * sn, k1 * sn + k2 * cs], axis=-1)
    qb = qr.astype(BF16)
    qx = (qr * xiq_ref[...]).astype(BF16)
    kb = kr.astype(BF16)
    kz = (kr * zk_ref[...]).astype(BF16)
    v = v_ref[...]
    st = state_sc[...]
    cross = jnp.dot(qx, st.astype(BF16), preferred_element_type=F32)
    qlane = lax.broadcasted_iota(jnp.int32, (1, 2 * LANES), 1)
    qhead = jnp.where(qlane % LANES < RET_HEADS * RET_HALF, (qlane % LANES) // RET_HALF, RET_HEADS)
    inner = []
    for h in range(RET_HEADS):
        qh = jnp.where(qhead == h, qb, jnp.zeros_like(qb))
        s = lax.dot_general(qh, kb, (((1,), (1,)), ((), ())), preferred_element_type=F32)
        p = (s * dm_ref[h]).astype(BF16)
        inner.append(jnp.dot(p, v[:, h * LANES:(h + 1) * LANES], preferred_element_type=F32))
    o = jnp.concatenate(inner, axis=-1) + cross
    kv = lax.dot_general(kz, v, (((0,), (0,)), ((), ())), preferred_element_type=F32)
    state_sc[...] = st * dec_ref[...] + kv * bm_ref[...]

    vlane = lax.broadcasted_iota(jnp.int32, (1, LANES), 1) < RET_DV
    normed = []
    for h in range(RET_HEADS):
        xh = o[:, h * LANES:(h + 1) * LANES]
        mu = jnp.sum(xh, axis=-1, keepdims=True) * (1.0 / RET_DV)
        d = jnp.where(vlane, xh - mu, 0.0)
        var = jnp.sum(d * d, axis=-1, keepdims=True) * (1.0 / RET_DV)
        normed.append(d * lax.rsqrt(var + LN_EPS))
    y = jnp.concatenate(normed, axis=-1) * gng_ref[...]
    gate = g_ref[...].astype(F32)
    or_ref[...] = (gate * _sigmoid(gate) * y).astype(BF16)

    u = u_ref[...].astype(F32)
    tail_rows = tail_sc.shape[0]
    ext = jnp.concatenate([tail_sc[...], u], axis=0)
    tail_sc[...] = u[ch - tail_rows:, :]
    e2 = ext + pltpu.roll(ext, 1, 0)
    e4 = e2 + pltpu.roll(e2, 2, 0)
    e8 = e4 + pltpu.roll(e4, 4, 0)
    e16 = e8 + pltpu.roll(e8, 8, 0)
    glane = lax.broadcasted_iota(jnp.int32, (1, POOL_W), 1) // POOL_GROUP
    win = jnp.where(glane == 0, e2, jnp.where(glane == 1, e4, jnp.where(glane == 2, e8, e16)))
    win = win[tail_rows:, :]
    wlen = jnp.where(glane == 0, 2, jnp.where(glane == 1, 4, jnp.where(glane == 2, 8, 16)))
    pos = j * ch + lax.broadcasted_iota(jnp.int32, (ch, POOL_W), 0)
    cnt = jnp.minimum(pos + 1, wlen).astype(F32)
    pooled = (win / cnt - u).astype(BF16)
    yp = jnp.dot(pooled, pw_ref[...], preferred_element_type=F32) * ps_ref[...]
    op_ref[...] = yp.astype(BF16)

    z = fl_ref[...] + bf_ref[...]
    logf = jnp.minimum(z, 0.0) - jnp.log1p(jnp.exp(-jnp.abs(z)))
    row = lax.broadcasted_iota(jnp.int32, (ch, LANES), 0)
    sh = 1
    while sh < ch:
        logf = logf + jnp.where(row >= sh, pltpu.roll(logf, sh, 0), 0.0)
        sh *= 2
    c = logf + carry_sc[0:1, :]
    carry_sc[...] = jnp.broadcast_to(c[ch - 1:ch, :], carry_sc.shape)
    c_ref[...] = c * LOG2E


def _seq_mix(proj, flog, consts, lw, bsz, lp):
    ch = SEQ_BLOCK
    nc = lp // ch
    rows = bsz * lp

    def rowblk(width, colblk):
        return pl.BlockSpec((ch, width), lambda bb, j: (bb * nc + j, colblk))

    def const(shape):
        nd = len(shape)
        return pl.BlockSpec(shape, lambda bb, j: (0,) * nd)

    in_specs = [
        rowblk(4 * LANES, QR_OFF // (4 * LANES)),
        rowblk(RETV_W, VR_OFF // RETV_W),
        rowblk(RETV_W, GR_OFF // RETV_W),
        rowblk(POOL_W, UP_OFF // POOL_W),
        pl.BlockSpec((ch, LANES), lambda bb, j: (bb * nc + j, 0)),
        pl.BlockSpec((ch, LANES), lambda bb, j: (j, 0)),
        pl.BlockSpec((ch, LANES), lambda bb, j: (j, 0)),
        const((ch, 2 * LANES)), const((ch, 2 * LANES)), const((RET_HEADS, ch, ch)),
        const((1, RETV_W)), const((2 * LANES, RETV_W)),
        const((1, RETV_W)), const((1, LANES)), const((POOL_W, POOL_W)), const((1, POOL_W)),
    ]
    out_shape = (jax.ShapeDtypeStruct((rows, RETV_W), BF16),
                 jax.ShapeDtypeStruct((rows, POOL_W), BF16),
                 jax.ShapeDtypeStruct((rows, LANES), F32))
    out_specs = (pl.BlockSpec((ch, RETV_W), lambda bb, j: (bb * nc + j, 0)),
                 pl.BlockSpec((ch, POOL_W), lambda bb, j: (bb * nc + j, 0)),
                 pl.BlockSpec((ch, LANES), lambda bb, j: (bb * nc + j, 0)))
    return pl.pallas_call(
        _seq_kernel,
        out_shape=out_shape,
        grid=(bsz, nc),
        in_specs=in_specs,
        out_specs=out_specs,
        scratch_shapes=[pltpu.VMEM((2 * LANES, RETV_W), F32),
                        pltpu.VMEM((16, POOL_W), F32),
                        pltpu.VMEM((8, LANES), F32)],
        compiler_params=pltpu.CompilerParams(dimension_semantics=("parallel", "arbitrary"),
                                             vmem_limit_bytes=VMEM_LIMIT),
        name="seq_mixers",
    )(proj, proj, proj, proj, flog, consts["cos"], consts["sin"],
      consts["xiq"], consts["zk"], consts["dm"], consts["dec"], consts["bm"],
      lw["gn_g"], lw["b_f"], lw["pool_w"], lw["pool_scale"])


FOX_AUG = 3


def _fox_kernel(q_ref, k_ref, v_ref, c_ref, o_ref,
                ka_sc, vt_sc, qa_sc, st_sc, pt_sc, al_sc, m_sc, acc_sc):
    i = pl.program_id(1)
    tq = SEQ_BLOCK
    tk = SEQ_BLOCK
    npair = FOX_HEADS // 2
    lane = lax.broadcasted_iota(jnp.int32, (1, LANES), 1)

    @pl.when(i == 0)
    def _():
        def chunk(t, carry):
            r0 = pl.multiple_of(t * tk, tk)
            cc = c_ref[pl.ds(r0, tk), :]
            for pr in range(npair):
                kk = k_ref[pl.ds(r0, tk), pr * LANES:(pr + 1) * LANES].astype(F32)
                vv = v_ref[pl.ds(r0, tk), pr * LANES:(pr + 1) * LANES].astype(F32)
                vtt = vv.T
                ones = jnp.ones((LANES - FOX_DH, tk), F32)
                k_heads = (kk, pltpu.roll(kk, FOX_DH, 1))
                v_heads = (vtt[:FOX_DH], vtt[FOX_DH:])
                for hh in range(2):
                    h = 2 * pr + hh
                    cb = jnp.broadcast_to(cc[:, h:h + 1], (tk, LANES))
                    aug = jnp.where(lane < FOX_DH, k_heads[hh], 0.0)
                    rem = cb
                    for a in range(FOX_AUG):
                        piece = rem.astype(BF16).astype(F32)
                        aug = jnp.where(lane == FOX_DH + a, -piece, aug)
                        rem = rem - piece
                    ka_sc[h, pl.ds(r0, tk), :] = aug.astype(BF16)
                    vt_sc[h, :, pl.ds(r0, tk)] = jnp.concatenate([v_heads[hh], ones], axis=0).astype(BF16)
            return carry

        lax.fori_loop(0, k_ref.shape[0] // tk, chunk, 0)

    for pr in range(npair):
        qq = q_ref[:, pr * LANES:(pr + 1) * LANES].astype(F32)
        tail = jnp.where(lane < FOX_DH + FOX_AUG, 1.0, 0.0)
        qa_sc[2 * pr] = jnp.where(lane < FOX_DH, qq, tail).astype(BF16)
        qa_sc[2 * pr + 1] = jnp.where(lane < FOX_DH, pltpu.roll(qq, FOX_DH, 1), tail).astype(BF16)
    m_sc[...] = jnp.full(m_sc.shape, NEG_INF, F32)
    acc_sc[...] = jnp.zeros_like(acc_sc)

    def scores(t, slot):
        ks = t * tk if isinstance(t, int) else pl.multiple_of(t * tk, tk)
        for h in range(FOX_HEADS):
            st_sc[slot, h] = lax.dot_general(ka_sc[h, pl.ds(ks, tk), :], qa_sc[h], (((1,), (1,)), ((), ())),
                                             preferred_element_type=F32)

    def softmax(slot, masked):
        def load(h, c0):
            s = st_sc[slot, h, :, c0:c0 + LANES]
            if masked:
                k_id = lax.broadcasted_iota(jnp.int32, (tk, LANES), 0)
                q_id = c0 + lax.broadcasted_iota(jnp.int32, (tk, LANES), 1)
                s = jnp.where(k_id <= q_id, s, NEG_INF)
            return s

        for h in range(FOX_HEADS):
            for c0 in range(0, tq, LANES):
                m_prev = m_sc[h, :, c0:c0 + LANES]
                m_new = jnp.maximum(m_prev, jnp.max(load(h, c0), axis=0, keepdims=True))
                m_sc[h, :, c0:c0 + LANES] = m_new
                al_sc[slot, h, :, c0:c0 + LANES] = jnp.exp2(m_prev - m_new)
                pt_sc[slot, h, :, c0:c0 + LANES] = jnp.exp2(load(h, c0) - m_new).astype(BF16)

    def values(t, slot):
        ks = t * tk if isinstance(t, int) else pl.multiple_of(t * tk, tk)
        for h in range(FOX_HEADS):
            pv = jnp.dot(vt_sc[h, :, pl.ds(ks, tk)], pt_sc[slot, h], preferred_element_type=F32)
            acc_sc[h] = al_sc[slot, h] * acc_sc[h] + pv

    def trip(t, p):
        values(t - 1, 1 - p)
        softmax(p, False)
        scores(t + 1, 1 - p)

    @pl.when(i == 0)
    def _():
        scores(0, 0)
        softmax(0, True)
        values(0, 0)

    @pl.when(i > 0)
    def _():
        scores(0, 0)
        softmax(0, False)
        scores(1, 1)

        def body(t, carry):
            @pl.when(t % 2 == 0)
            def _():
                trip(t, 0)

            @pl.when(t % 2 == 1)
            def _():
                trip(t, 1)

            return carry

        lax.fori_loop(1, i, body, 0)

        for p in range(2):
            @pl.when(i % 2 == p)
            def _():
                values(i - 1, 1 - p)
                softmax(p, True)
                values(i, p)

    outs = []
    for pr in range(npair):
        acc_a = acc_sc[2 * pr]
        acc_b = acc_sc[2 * pr + 1]
        o_t = jnp.concatenate([acc_a[:FOX_DH] / acc_a[FOX_DH:FOX_DH + 1],
                               acc_b[:FOX_DH] / acc_b[FOX_DH:FOX_DH + 1]], axis=0)
        outs.append(o_t.T)
    o_ref[...] = jnp.concatenate(outs, axis=-1).astype(BF16)


def _fox(qkv, cum, bsz, lp):
    tq = SEQ_BLOCK
    nq = lp // tq
    rows = bsz * lp
    return pl.pallas_call(
        _fox_kernel,
        out_shape=jax.ShapeDtypeStruct((rows, FOX_W), BF16),
        grid=(bsz, nq),
        in_specs=[
            pl.BlockSpec((tq, FOX_W), lambda bb, i: (bb * nq + i, 0)),
            pl.BlockSpec((lp, FOX_W), lambda bb, i: (bb, 1)),
            pl.BlockSpec((lp, FOX_W), lambda bb, i: (bb, 2)),
            pl.BlockSpec((lp, LANES), lambda bb, i: (bb, 0)),
        ],
        out_specs=pl.BlockSpec((tq, FOX_W), lambda bb, i: (bb * nq + i, 0)),
        scratch_shapes=[pltpu.VMEM((FOX_HEADS, lp, LANES), BF16),
                        pltpu.VMEM((FOX_HEADS, LANES, lp), BF16),
                        pltpu.VMEM((FOX_HEADS, tq, LANES), BF16),
                        pltpu.VMEM((2, FOX_HEADS, SEQ_BLOCK, tq), F32),
                        pltpu.VMEM((2, FOX_HEADS, SEQ_BLOCK, tq), BF16),
                        pltpu.VMEM((2, FOX_HEADS, 1, tq), F32),
                        pltpu.VMEM((FOX_HEADS, 1, tq), F32),
                        pltpu.VMEM((FOX_HEADS, LANES, tq), F32)],
        compiler_params=pltpu.CompilerParams(
            dimension_semantics=("parallel", "arbitrary"),
            vmem_limit_bytes=VMEM_LIMIT),
        name="fox_attention",
    )(qkv, qkv, qkv, cum)


def _outproj_kernel(or_ref, op_ref, of_ref, h_ref, w_ref, g_ref, b_ref, o_ref):
    mix = jnp.concatenate([or_ref[...], op_ref[...], of_ref[...]], axis=-1)
    y = ALPHA * h_ref[...] + jnp.dot(mix, w_ref[...], preferred_element_type=F32)
    o_ref[...] = _layer_norm(y, g_ref[...], b_ref[...])


def _outproj(o_r, o_p, o_f, h, w, g, b, tm):
    rows = h.shape[0]
    return pl.pallas_call(
        _outproj_kernel,
        out_shape=jax.ShapeDtypeStruct((rows, D_MODEL), F32),
        grid=(rows // tm,),
        in_specs=[
            pl.BlockSpec((tm, RETV_W), lambda i: (i, 0)),
            pl.BlockSpec((tm, POOL_W), lambda i: (i, 0)),
            pl.BlockSpec((tm, FOX_W), lambda i: (i, 0)),
            pl.BlockSpec((tm, D_MODEL), lambda i: (i, 0)),
            pl.BlockSpec((MIX_W, D_MODEL), lambda i: (0, 0)),
            pl.BlockSpec((1, D_MODEL), lambda i: (0, 0)),
            pl.BlockSpec((1, D_MODEL), lambda i: (0, 0)),
        ],
        out_specs=pl.BlockSpec((tm, D_MODEL), lambda i: (i, 0)),
        compiler_params=pltpu.CompilerParams(dimension_semantics=("parallel",),
                                             vmem_limit_bytes=VMEM_LIMIT),
        name="out_proj_ln",
    )(o_r, o_p, o_f, h, w, g, b)


def _ffn_kernel(h_ref, w1_ref, w3_ref, w2_ref, g_ref, b_ref, o_ref):
    h = h_ref[...]
    xb = h.astype(BF16)
    acc = None
    for c0 in range(0, D_FF, FF_CHUNK):
        a = jnp.dot(xb, w1_ref[:, c0:c0 + FF_CHUNK], preferred_element_type=F32)
        b = jnp.dot(xb, w3_ref[:, c0:c0 + FF_CHUNK], preferred_element_type=F32)
        t = (a * _sigmoid(a) * b).astype(BF16)
        part = jnp.dot(t, w2_ref[c0:c0 + FF_CHUNK, :], preferred_element_type=F32)
        acc = part if acc is None else acc + part
    o_ref[...] = _layer_norm(ALPHA * h + acc, g_ref[...], b_ref[...])


def _ffn(h, w1, w3, w2, g, b, tm):
    rows = h.shape[0]
    resident = dict(pipeline_mode=pl.Buffered(1))
    return pl.pallas_call(
        _ffn_kernel,
        out_shape=jax.ShapeDtypeStruct((rows, D_MODEL), F32),
        grid=(rows // tm,),
        in_specs=[
            pl.BlockSpec((tm, D_MODEL), lambda i: (i, 0)),
            pl.BlockSpec((D_MODEL, D_FF), lambda i: (0, 0), **resident),
            pl.BlockSpec((D_MODEL, D_FF), lambda i: (0, 0), **resident),
            pl.BlockSpec((D_FF, D_MODEL), lambda i: (0, 0), **resident),
            pl.BlockSpec((1, D_MODEL), lambda i: (0, 0)),
            pl.BlockSpec((1, D_MODEL), lambda i: (0, 0)),
        ],
        out_specs=pl.BlockSpec((tm, D_MODEL), lambda i: (i, 0)),
        compiler_params=pltpu.CompilerParams(dimension_semantics=("parallel",),
                                             vmem_limit_bytes=VMEM_LIMIT),
        name="ffn_ln",
    )(h, w1, w3, w2, g, b)


def _pack_w_in(w):
    d = w.shape[0]

    def rot_pack(seg):
        s = seg.reshape(d, RET_HEADS, RET_DK)
        x1 = s[:, :, :RET_HALF].reshape(d, RET_HEADS * RET_HALF)
        x2 = s[:, :, RET_HALF:].reshape(d, RET_HEADS * RET_HALF)
        z = jnp.zeros((d, LANES - RET_HEADS * RET_HALF), w.dtype)
        return jnp.concatenate([x1, z, x2, z], axis=1)

    def head_pad(seg):
        s = seg.reshape(d, RET_HEADS, RET_DV)
        return jnp.pad(s, ((0, 0), (0, 0), (0, RET_HEAD_PAD - RET_DV))).reshape(d, RETV_W)

    o = 0
    segs = []
    for sz in (RET_QK, RET_QK, RET_W, RET_W, POOL_W, FOX_W, FOX_W, FOX_W, FOX_HEADS):
        segs.append(w[:, o:o + sz])
        o += sz
    q_r, k_r, v_r, g_r, u_p, q_f, k_f, v_f, f_l = segs
    packed = jnp.concatenate(
        [rot_pack(q_r), rot_pack(k_r), head_pad(v_r), head_pad(g_r), u_p, q_f, k_f, v_f,
         jnp.pad(f_l, ((0, 0), (0, LANES - FOX_HEADS)))], axis=1)
    return packed.astype(BF16)


def _pack_w_out(w):
    d = w.shape[1]
    w_r = jnp.pad(w[:RET_W].reshape(RET_HEADS, RET_DV, d),
                  ((0, 0), (0, RET_HEAD_PAD - RET_DV), (0, 0))).reshape(RETV_W, d)
    return jnp.concatenate([w_r, w[RET_W:]], axis=0).astype(BF16)


def _pad_heads(vec):
    return jnp.pad(vec.reshape(RET_HEADS, RET_DV), ((0, 0), (0, RET_HEAD_PAD - RET_DV))).reshape(1, RETV_W)


def _block_diag(pw):
    out = jnp.zeros((POOL_W, POOL_W), pw.dtype)
    for g in range(len(POOL_WINDOWS)):
        out = out.at[g * POOL_GROUP:(g + 1) * POOL_GROUP, g * POOL_GROUP:(g + 1) * POOL_GROUP].set(pw[g])
    return out.astype(BF16)


def _retention_tables():
    ch = SEQ_BLOCK
    gamma = (1.0 - 2.0 ** (-5.0 - np.arange(RET_HEADS, dtype=np.float32))).astype(np.float32)
    lg = np.log(gamma).astype(np.float32)
    i = np.arange(ch, dtype=np.float32)
    diff = i[:, None] - i[None, :]
    dm = np.where(diff >= 0, np.exp(lg[:, None, None] * np.maximum(diff, 0.0)), 0.0).astype(np.float32)
    xi = np.exp(lg[:, None] * (i + 1.0)).astype(np.float32)
    zeta = np.exp(lg[:, None] * (ch - 1.0 - i)).astype(np.float32)
    lane = np.arange(2 * LANES)
    within = lane % LANES
    lane_head = np.where(within < RET_HEADS * RET_HALF, within // RET_HALF, -1)
    xiq = np.zeros((ch, 2 * LANES), np.float32)
    zk = np.zeros((ch, 2 * LANES), np.float32)
    bm = np.zeros((2 * LANES, RETV_W), np.float32)
    for h in range(RET_HEADS):
        sel = lane_head == h
        xiq[:, sel] = xi[h][:, None]
        zk[:, sel] = zeta[h][:, None]
        bm[sel, h * LANES:(h + 1) * LANES] = 1.0
    dec = np.repeat(np.exp(lg * ch).astype(np.float32), LANES)[None, :]
    return dict(xiq=jnp.asarray(xiq), zk=jnp.asarray(zk), dm=jnp.asarray(dm),
                dec=jnp.asarray(dec), bm=jnp.asarray(bm))


def _rotary_tables(lp):
    pos = jnp.arange(lp, dtype=F32)
    inv_freq = ROPE_BASE ** (-jnp.arange(RET_HALF, dtype=F32) / RET_HALF)
    ang = pos[:, None] * inv_freq[None, :]
    pad = LANES - RET_HEADS * RET_HALF
    cos = jnp.pad(jnp.tile(jnp.cos(ang), (1, RET_HEADS)), ((0, 0), (0, pad)))
    sin = jnp.pad(jnp.tile(jnp.sin(ang), (1, RET_HEADS)), ((0, 0), (0, pad)))
    return cos, sin


def kernel(x, meta, ln_emb_g, ln_emb_b, w_in, b_f, ret_gn_g, pool_w, pool_scale, w_out, ln1_g, ln1_b,
           w_ffn1, w_ffn3, w_ffn2, ln2_g, ln2_b):
    bsz, seq, d = x.shape
    assert d == D_MODEL and seq % SEQ_BLOCK == 0
    depth = w_in.shape[0]
    assert depth == DEPTH
    lp = seq + SEQ_BLOCK
    rows = bsz * lp
    tm = ROW_TILE if rows % ROW_TILE == 0 else SEQ_BLOCK

    consts = _retention_tables()
    consts["cos"], consts["sin"] = _rotary_tables(lp)

    h = _embed(x, meta, ln_emb_g, ln_emb_b, lp).reshape(rows, d)
    for l in range(depth):
        lw = dict(gn_g=_pad_heads(ret_gn_g[l]),
                  b_f=jnp.pad(b_f[l], (0, LANES - FOX_HEADS)).reshape(1, LANES),
                  pool_w=_block_diag(pool_w[l]),
                  pool_scale=pool_scale[l].reshape(1, POOL_W))
        ret, qkv, flog = _inproj(h, _pack_w_in(w_in[l]), tm)
        o_r, o_p, cum = _seq_mix(ret, flog, consts, lw, bsz, lp)
        o_f = _fox(qkv, cum, bsz, lp)
        h = _outproj(o_r, o_p, o_f, h, _pack_w_out(w_out[l]),
                     ln1_g[l].reshape(1, d), ln1_b[l].reshape(1, d), tm)
        h = _ffn(h, w_ffn1[l].astype(BF16), w_ffn3[l].astype(BF16), w_ffn2[l].astype(BF16),
                 ln2_g[l].reshape(1, d), ln2_b[l].reshape(1, d), tm)
    return h.reshape(bsz, lp, d)[:, N_META:N_META + seq]
```

```python
import functools

import numpy as np
import jax
import jax.numpy as jnp
from jax import lax
from jax.experimental import pallas as pl
from jax.experimental.pallas import tpu as pltpu

F32 = jnp.float32
BF16 = jnp.bfloat16

D_MODEL = 1024
N_META = 16
RET_HEADS = 4
RET_DK = 48
RET_HALF = RET_DK // 2
RET_DV = 96
RET_QK = RET_HEADS * RET_DK
RET_W = RET_HEADS * RET_DV
POOL_WINDOWS = (2, 4, 8, 16)
POOL_GROUP = 64
POOL_W = len(POOL_WINDOWS) * POOL_GROUP
FOX_HEADS = 6
FOX_DH = 64
FOX_W = FOX_HEADS * FOX_DH
D_FF = 2816
ROPE_BASE = 10000.0
LN_EPS = 1e-5
NEG_INF = -1e30
DEPTH = 2
ALPHA = (2.0 * DEPTH) ** 0.25
LOG2E = 1.4426950408889634

LANES = 128
MXU_DIM = 256
SEQ_BLOCK = 256
ROW_TILE = 512
VMEM_LIMIT = 56 * 1024 * 1024

RET_HEAD_PAD = LANES
QR_OFF = 0
KR_OFF = 2 * LANES
VR_OFF = 4 * LANES
GR_OFF = VR_OFF + RET_HEADS * RET_HEAD_PAD
UP_OFF = GR_OFF + RET_HEADS * RET_HEAD_PAD
QF_OFF = UP_OFF + POOL_W
KF_OFF = QF_OFF + FOX_W
VF_OFF = KF_OFF + FOX_W
FL_OFF = VF_OFF + FOX_W
RET_COLS = QF_OFF
FOX_COLS = 3 * FOX_W
N_PACK = FL_OFF + LANES
RETV_W = RET_HEADS * RET_HEAD_PAD
MIX_W = RETV_W + POOL_W + FOX_W
PROJ_CHUNK = 512
FF_CHUNK = 256


def _layer_norm(x, g, b):
    mu = jnp.mean(x, axis=-1, keepdims=True)
    d = x - mu
    var = jnp.mean(d * d, axis=-1, keepdims=True)
    return d * lax.rsqrt(var + LN_EPS) * g + b


def _sigmoid(x):
    return 1.0 / (1.0 + jnp.exp(-x))


def _embed_kernel(meta_ref, xm_ref, xe_ref, g_ref, b_ref, o_ref):
    j = pl.program_id(1)
    last = pl.num_programs(1) - 1
    top = jnp.where(j == 0, meta_ref[...], xe_ref[...])
    body = jnp.where(j == last, 0.0, xm_ref[0:SEQ_BLOCK - N_META, :])
    rows = jnp.concatenate([top, body], axis=0)
    o_ref[...] = _layer_norm(rows, g_ref[...], b_ref[...])


def _embed(x, meta, g, b, lp):
    bsz, seq, d = x.shape
    nblk = lp // SEQ_BLOCK
    n_xblk = seq // SEQ_BLOCK
    per = SEQ_BLOCK // N_META
    return pl.pallas_call(
        _embed_kernel,
        out_shape=jax.ShapeDtypeStruct((bsz, lp, d), F32),
        grid=(bsz, nblk),
        in_specs=[
            pl.BlockSpec((N_META, d), lambda bb, j: (0, 0)),
            pl.BlockSpec((None, SEQ_BLOCK, d), lambda bb, j: (bb, jnp.minimum(j, n_xblk - 1), 0)),
            pl.BlockSpec((None, N_META, d), lambda bb, j: (bb, jnp.maximum(per * j - 1, 0), 0)),
            pl.BlockSpec((1, d), lambda bb, j: (0, 0)),
            pl.BlockSpec((1, d), lambda bb, j: (0, 0)),
        ],
        out_specs=pl.BlockSpec((None, SEQ_BLOCK, d), lambda bb, j: (bb, j, 0)),
        compiler_params=pltpu.CompilerParams(dimension_semantics=("parallel", "arbitrary")),
        name="embed_ln",
    )(meta, x, x, g.reshape(1, d), b.reshape(1, d))


def _inproj_kernel(h_ref, w_ref, cs_ref, ret_ref, fox_ref, flog_ref):
    xb = h_ref[...].astype(BF16)

    def cols(c0, width):
        r = jnp.dot(xb, w_ref[:, c0:c0 + width], preferred_element_type=F32)
        return r * cs_ref[:, c0:c0 + width]

    for c0 in range(0, RET_COLS, PROJ_CHUNK):
        width = min(PROJ_CHUNK, RET_COLS - c0)
        ret_ref[:, c0:c0 + width] = cols(c0, width).astype(BF16)
    for c0 in range(0, FOX_COLS - LANES, PROJ_CHUNK):
        fox_ref[:, c0:c0 + PROJ_CHUNK] = cols(QF_OFF + c0, PROJ_CHUNK).astype(BF16)
    tail = cols(FL_OFF - LANES, 2 * LANES)
    fox_ref[:, FOX_COLS - LANES:] = tail[:, :LANES].astype(BF16)
    flog_ref[...] = tail[:, LANES:]


def _inproj_col_scale():
    cs = np.ones((1, N_PACK), np.float32)
    cs[0, KR_OFF:KR_OFF + 2 * LANES] = RET_DK ** -0.5
    cs[0, QF_OFF:QF_OFF + FOX_W] = FOX_DH ** -0.5 * LOG2E
    return jnp.asarray(cs)


def _inproj(h, w, tm):
    rows = h.shape[0]
    return pl.pallas_call(
        _inproj_kernel,
        out_shape=(jax.ShapeDtypeStruct((rows, RET_COLS), BF16),
                   jax.ShapeDtypeStruct((rows, FOX_COLS), BF16),
                   jax.ShapeDtypeStruct((rows, LANES), F32)),
        grid=(rows // tm,),
        in_specs=[
            pl.BlockSpec((tm, D_MODEL), lambda i: (i, 0)),
            pl.BlockSpec((D_MODEL, N_PACK), lambda i: (0, 0)),
            pl.BlockSpec((1, N_PACK), lambda i: (0, 0)),
        ],
        out_specs=(pl.BlockSpec((tm, RET_COLS), lambda i: (i, 0)),
                   pl.BlockSpec((tm, FOX_COLS), lambda i: (i, 0)),
                   pl.BlockSpec((tm, LANES), lambda i: (i, 0))),
        compiler_params=pltpu.CompilerParams(dimension_semantics=("parallel",),
                                             vmem_limit_bytes=VMEM_LIMIT),
        name="in_proj",
    )(h, w, _inproj_col_scale())


def _seq_kernel(qk_ref, v_ref, g_ref, u_ref, fl_ref, cos_ref, sin_ref,
                xiq_ref, zk_ref, dm_ref, dec_ref, bm_ref, gng_ref, bf_ref, pw_ref, ps_ref,
                or_ref, op_ref, c_ref,
                state_sc, tail_sc, carry_sc):
    j = pl.program_id(1)
    ch = SEQ_BLOCK

    @pl.when(j == 0)
    def _():
        state_sc[...] = jnp.zeros_like(state_sc)
        tail_sc[...] = jnp.zeros_like(tail_sc)
        carry_sc[...] = jnp.zeros_like(carry_sc)

    qk = qk_ref[...].astype(F32)
    cs = cos_ref[...]
    sn = sin_ref[...]
    q1, q2 = qk[:, 0:LANES], qk[:, LANES:2 * LANES]
    k1, k2 = qk[:, 2 * LANES:3 * LANES], qk[:, 3 * LANES:4 * LANES]
    qr = jnp.concatenate([q1 * cs - q2 * sn, q1 * sn + q2 * cs], axis=-1)
    kr = jnp.concatenate([k1 * cs - k2 * sn, k1 * sn + k2 * cs], axis=-1)
    qb = qr.astype(BF16)
    qx = (qr * xiq_ref[...]).astype(BF16)
    kb = kr.astype(BF16)
    kz = (kr * zk_ref[...]).astype(BF16)
    v = v_ref[...]
    st = state_sc[...]
    cross = jnp.dot(qx, st.astype(BF16), preferred_element_type=F32)
    qlane = lax.broadcasted_iota(jnp.int32, (1, 2 * LANES), 1)
    qhead = jnp.where(qlane % LANES < RET_HEADS * RET_HALF, (qlane % LANES) // RET_HALF, RET_HEADS)
    inner = []
    for h in range(RET_HEADS):
        qh = jnp.where(qhead == h, qb, jnp.zeros_like(qb))
        s = lax.dot_general(qh, kb, (((1,), (1,)), ((), ())), preferred_element_type=F32)
        p = (s * dm_ref[h]).astype(BF16)
        inner.append(jnp.dot(p, v[:, h * LANES:(h + 1) * LANES], preferred_element_type=F32))
    o = jnp.concatenate(inner, axis=-1) + cross
    kv = lax.dot_general(kz, v, (((0,), (0,)), ((), ())), preferred_element_type=F32)
    state_sc[...] = st * dec_ref[...] + kv * bm_ref[...]

    vlane = lax.broadcasted_iota(jnp.int32, (1, LANES), 1) < RET_DV
    normed = []
    for h in range(RET_HEADS):
        xh = o[:, h * LANES:(h + 1) * LANES]
        mu = jnp.sum(xh, axis=-1, keepdims=True) * (1.0 / RET_DV)
        d = jnp.where(vlane, xh - mu, 0.0)
        var = jnp.sum(d * d, axis=-1, keepdims=True) * (1.0 / RET_DV)
        normed.append(d * lax.rsqrt(var + LN_EPS))
    y = jnp.concatenate(normed, axis=-1) * gng_ref[...]
    gate = g_ref[...].astype(F32)
    or_ref[...] = (gate * _sigmoid(gate) * y).astype(BF16)

    u = u_ref[...].astype(F32)
    tail_rows = tail_sc.shape[0]
    ext = jnp.concatenate([tail_sc[...], u], axis=0)
    tail_sc[...] = u[ch - tail_rows:, :]
    e2 = ext + pltpu.roll(ext, 1, 0)
    e4 = e2 + pltpu.roll(e2, 2, 0)
    e8 = e4 + pltpu.roll(e4, 4, 0)
    e16 = e8 + pltpu.roll(e8, 8, 0)
    glane = lax.broadcasted_iota(jnp.int32, (1, POOL_W), 1) // POOL_GROUP
    win = jnp.where(glane == 0, e2, jnp.where(glane == 1, e4, jnp.where(glane == 2, e8, e16)))
    win = win[tail_rows:, :]
    wlen = jnp.where(glane == 0, 2, jnp.where(glane == 1, 4, jnp.where(glane == 2, 8, 16)))
    pos = j * ch + lax.broadcasted_iota(jnp.int32, (ch, POOL_W), 0)
    cnt = jnp.minimum(pos + 1, wlen).astype(F32)
    pooled = (win / cnt - u).astype(BF16)
    yp = jnp.dot(pooled, pw_ref[...], preferred_element_type=F32) * ps_ref[...]
    op_ref[...] = yp.astype(BF16)

    z = fl_ref[...] + bf_ref[...]
    logf = jnp.minimum(z, 0.0) - jnp.log1p(jnp.exp(-jnp.abs(z)))
    row = lax.broadcasted_iota(jnp.int32, (ch, LANES), 0)
    sh = 1
    while sh < ch:
        logf = logf + jnp.where(row >= sh, pltpu.roll(logf, sh, 0), 0.0)
        sh *= 2
    c = logf + carry_sc[0:1, :]
    carry_sc[...] = jnp.broadcast_to(c[ch - 1:ch, :], carry_sc.shape)
    c_ref[...] = c * LOG2E


def _seq_mix(proj, flog, consts, lw, bsz, lp):
    ch = SEQ_BLOCK
    nc = lp // ch
    rows = bsz * lp

    def rowblk(width, colblk):
        return pl.BlockSpec((ch, width), lambda bb, j: (bb * nc + j, colblk))

    def const(shape):
        nd = len(shape)
        return pl.BlockSpec(shape, lambda bb, j: (0,) * nd)

    in_specs = [
        rowblk(4 * LANES, QR_OFF // (4 * LANES)),
        rowblk(RETV_W, VR_OFF // RETV_W),
        rowblk(RETV_W, GR_OFF // RETV_W),
        rowblk(POOL_W, UP_OFF // POOL_W),
        pl.BlockSpec((ch, LANES), lambda bb, j: (bb * nc + j, 0)),
        pl.BlockSpec((ch, LANES), lambda bb, j: (j, 0)),
        pl.BlockSpec((ch, LANES), lambda bb, j: (j, 0)),
        const((ch, 2 * LANES)), const((ch, 2 * LANES)), const((RET_HEADS, ch, ch)),
        const((1, RETV_W)), const((2 * LANES, RETV_W)),
        const((1, RETV_W)), const((1, LANES)), const((POOL_W, POOL_W)), const((1, POOL_W)),
    ]
    out_shape = (jax.ShapeDtypeStruct((rows, RETV_W), BF16),
                 jax.ShapeDtypeStruct((rows, POOL_W), BF16),
                 jax.ShapeDtypeStruct((rows, LANES), F32))
    out_specs = (pl.BlockSpec((ch, RETV_W), lambda bb, j: (bb * nc + j, 0)),
                 pl.BlockSpec((ch, POOL_W), lambda bb, j: (bb * nc + j, 0)),
                 pl.BlockSpec((ch, LANES), lambda bb, j: (bb * nc + j, 0)))
    return pl.pallas_call(
        _seq_kernel,
        out_shape=out_shape,
        grid=(bsz, nc),
        in_specs=in_specs,
        out_specs=out_specs,
        scratch_shapes=[pltpu.VMEM((2 * LANES, RETV_W), F32),
                        pltpu.VMEM((16, POOL_W), F32),
                        pltpu.VMEM((8, LANES), F32)],
        compiler_params=pltpu.CompilerParams(dimension_semantics=("parallel", "arbitrary"),
                                             vmem_limit_bytes=VMEM_LIMIT),
        name="seq_mixers",
    )(proj, proj, proj, proj, flog, consts["cos"], consts["sin"],
      consts["xiq"], consts["zk"], consts["dm"], consts["dec"], consts["bm"],
      lw["gn_g"], lw["b_f"], lw["pool_w"], lw["pool_scale"])


FOX_AUG = 3
FOX_VROWS = FOX_DH + 16


def _fox_kernel(q_ref, k_ref, v_ref, c_ref, o_ref,
                ka_sc, vt_sc, qa_sc, st_sc, m_sc, acc_sc):
    i = pl.program_id(1)
    tq = SEQ_BLOCK
    tk = SEQ_BLOCK
    npair = FOX_HEADS // 2
    lane = lax.broadcasted_iota(jnp.int32, (1, LANES), 1)

    @pl.when(i == 0)
    def _():
        def chunk(t, carry):
            r0 = pl.multiple_of(t * tk, tk)
            cc = c_ref[pl.ds(r0, tk), :]
            for pr in range(npair):
                kk = k_ref[pl.ds(r0, tk), pr * LANES:(pr + 1) * LANES].astype(F32)
                vv = v_ref[pl.ds(r0, tk), pr * LANES:(pr + 1) * LANES].astype(F32)
                vtt = vv.T
                ones = jnp.ones((FOX_VROWS - FOX_DH, tk), F32)
                k_heads = (kk, pltpu.roll(kk, FOX_DH, 1))
                v_heads = (vtt[:FOX_DH], vtt[FOX_DH:])
                for hh in range(2):
                    h = 2 * pr + hh
                    cb = jnp.broadcast_to(cc[:, h:h + 1], (tk, LANES))
                    aug = jnp.where(lane < FOX_DH, k_heads[hh], 0.0)
                    rem = cb
                    for a in range(FOX_AUG):
                        piece = rem.astype(BF16).astype(F32)
                        aug = jnp.where(lane == FOX_DH + a, -piece, aug)
                        rem = rem - piece
                    ka_sc[h, pl.ds(r0, tk), :] = aug.astype(BF16)
                    vt_sc[h, :, pl.ds(r0, tk)] = jnp.concatenate([v_heads[hh], ones], axis=0).astype(BF16)
            return carry

        lax.fori_loop(0, k_ref.shape[0] // tk, chunk, 0)

    for pr in range(npair):
        qq = q_ref[:, pr * LANES:(pr + 1) * LANES].astype(F32)
        tail = jnp.where(lane < FOX_DH + FOX_AUG, 1.0, 0.0)
        qa_sc[2 * pr] = jnp.where(lane < FOX_DH, qq, tail).astype(BF16)
        qa_sc[2 * pr + 1] = jnp.where(lane < FOX_DH, pltpu.roll(qq, FOX_DH, 1), tail).astype(BF16)
    m_sc[...] = jnp.full(m_sc.shape, NEG_INF, F32)
    acc_sc[...] = jnp.zeros_like(acc_sc)

    def scores(t):
        ks = t * tk if isinstance(t, int) else pl.multiple_of(t * tk, tk)
        for h in range(FOX_HEADS):
            st_sc[h] = lax.dot_general(ka_sc[h, pl.ds(ks, tk), :], qa_sc[h], (((1,), (1,)), ((), ())),
                                       preferred_element_type=F32)

    def attend(t, masked):
        ks = t * tk if isinstance(t, int) else pl.multiple_of(t * tk, tk)

        def load(h, c0):
            s = st_sc[h, :, c0:c0 + LANES]
            if masked:
                k_id = lax.broadcasted_iota(jnp.int32, (tk, LANES), 0)
                q_id = c0 + lax.broadcasted_iota(jnp.int32, (tk, LANES), 1)
                s = jnp.where(k_id <= q_id, s, NEG_INF)
            return s

        for h in range(FOX_HEADS):
            p_halves, a_halves = [], []
            for c0 in range(0, tq, LANES):
                m_prev = m_sc[h, :, c0:c0 + LANES]
                m_new = jnp.maximum(m_prev, jnp.max(load(h, c0), axis=0, keepdims=True))
                m_sc[h, :, c0:c0 + LANES] = m_new
                a_halves.append(jnp.exp2(m_prev - m_new))
                p_halves.append(jnp.exp2(load(h, c0) - m_new).astype(BF16))
            p_t = jnp.concatenate(p_halves, axis=1)
            pv = jnp.dot(vt_sc[h, :, pl.ds(ks, tk)], p_t, preferred_element_type=F32)
            acc_sc[h] = jnp.concatenate(a_halves, axis=1) * acc_sc[h] + pv

    scores(0)

    def body(t, carry):
        attend(t, False)
        scores(t + 1)
        return carry

    lax.fori_loop(0, i, body, 0)
    attend(i, True)

    outs = []
    for pr in range(npair):
        acc_a = acc_sc[2 * pr]
        acc_b = acc_sc[2 * pr + 1]
        o_t = jnp.concatenate([acc_a[:FOX_DH] / acc_a[FOX_DH:FOX_DH + 1],
                               acc_b[:FOX_DH] / acc_b[FOX_DH:FOX_DH + 1]], axis=0)
        outs.append(o_t.T)
    o_ref[...] = jnp.concatenate(outs, axis=-1).astype(BF16)


def _fox(qkv, cum, bsz, lp):
    tq = SEQ_BLOCK
    nq = lp // tq
    rows = bsz * lp
    return pl.pallas_call(
        _fox_kernel,
        out_shape=jax.ShapeDtypeStruct((rows, FOX_W), BF16),
        grid=(bsz, nq),
        in_specs=[
            pl.BlockSpec((tq, FOX_W), lambda bb, i: (bb * nq + i, 0)),
            pl.BlockSpec((lp, FOX_W), lambda bb, i: (bb, 1)),
            pl.BlockSpec((lp, FOX_W), lambda bb, i: (bb, 2)),
            pl.BlockSpec((lp, LANES), lambda bb, i: (bb, 0)),
        ],
        out_specs=pl.BlockSpec((tq, FOX_W), lambda bb, i: (bb * nq + i, 0)),
        scratch_shapes=[pltpu.VMEM((FOX_HEADS, lp, LANES), BF16),
                        pltpu.VMEM((FOX_HEADS, FOX_VROWS, lp), BF16),
                        pltpu.VMEM((FOX_HEADS, tq, LANES), BF16),
                        pltpu.VMEM((FOX_HEADS, SEQ_BLOCK, tq), F32),
                        pltpu.VMEM((FOX_HEADS, 1, tq), F32),
                        pltpu.VMEM((FOX_HEADS, FOX_VROWS, tq), F32)],
        compiler_params=pltpu.CompilerParams(
            dimension_semantics=("parallel", "arbitrary"),
            vmem_limit_bytes=VMEM_LIMIT),
        name="fox_attention",
    )(qkv, qkv, qkv, cum)


def _outproj_kernel(or_ref, op_ref, of_ref, h_ref, w_ref, g_ref, b_ref, o_ref):
    mix = jnp.concatenate([or_ref[...], op_ref[...], of_ref[...]], axis=-1)
    y = ALPHA * h_ref[...] + jnp.dot(mix, w_ref[...], preferred_element_type=F32)
    o_ref[...] = _layer_norm(y, g_ref[...], b_ref[...])


def _outproj(o_r, o_p, o_f, h, w, g, b, tm):
    rows = h.shape[0]
    return pl.pallas_call(
        _outproj_kernel,
        out_shape=jax.ShapeDtypeStruct((rows, D_MODEL), F32),
        grid=(rows // tm,),
        in_specs=[
            pl.BlockSpec((tm, RETV_W), lambda i: (i, 0)),
            pl.BlockSpec((tm, POOL_W), lambda i: (i, 0)),
            pl.BlockSpec((tm, FOX_W), lambda i: (i, 0)),
            pl.BlockSpec((tm, D_MODEL), lambda i: (i, 0)),
            pl.BlockSpec((MIX_W, D_MODEL), lambda i: (0, 0)),
            pl.BlockSpec((1, D_MODEL), lambda i: (0, 0)),
            pl.BlockSpec((1, D_MODEL), lambda i: (0, 0)),
        ],
        out_specs=pl.BlockSpec((tm, D_MODEL), lambda i: (i, 0)),
        compiler_params=pltpu.CompilerParams(dimension_semantics=("parallel",),
                                             vmem_limit_bytes=VMEM_LIMIT),
        name="out_proj_ln",
    )(o_r, o_p, o_f, h, w, g, b)


def _ffn_kernel(h_ref, w1_ref, w3_ref, w2_ref, g_ref, b_ref, o_ref):
    h = h_ref[...]
    xb = h.astype(BF16)
    acc = None
    for c0 in range(0, D_FF, FF_CHUNK):
        a = jnp.dot(xb, w1_ref[:, c0:c0 + FF_CHUNK], preferred_element_type=F32)
        b = jnp.dot(xb, w3_ref[:, c0:c0 + FF_CHUNK], preferred_element_type=F32)
        t = (a * _sigmoid(a) * b).astype(BF16)
        part = jnp.dot(t, w2_ref[c0:c0 + FF_CHUNK, :], preferred_element_type=F32)
        acc = part if acc is None else acc + part
    o_ref[...] = _layer_norm(ALPHA * h + acc, g_ref[...], b_ref[...])


def _ffn(h, w1, w3, w2, g, b, tm):
    rows = h.shape[0]
    resident = dict(pipeline_mode=pl.Buffered(1))
    return pl.pallas_call(
        _ffn_kernel,
        out_shape=jax.ShapeDtypeStruct((rows, D_MODEL), F32),
        grid=(rows // tm,),
        in_specs=[
            pl.BlockSpec((tm, D_MODEL), lambda i: (i, 0)),
            pl.BlockSpec((D_MODEL, D_FF), lambda i: (0, 0), **resident),
            pl.BlockSpec((D_MODEL, D_FF), lambda i: (0, 0), **resident),
            pl.BlockSpec((D_FF, D_MODEL), lambda i: (0, 0), **resident),
            pl.BlockSpec((1, D_MODEL), lambda i: (0, 0)),
            pl.BlockSpec((1, D_MODEL), lambda i: (0, 0)),
        ],
        out_specs=pl.BlockSpec((tm, D_MODEL), lambda i: (i, 0)),
        compiler_params=pltpu.CompilerParams(dimension_semantics=("parallel",),
                                             vmem_limit_bytes=VMEM_LIMIT),
        name="ffn_ln",
    )(h, w1, w3, w2, g, b)


def _pack_w_in(w):
    d = w.shape[0]

    def rot_pack(seg):
        s = seg.reshape(d, RET_HEADS, RET_DK)
        x1 = s[:, :, :RET_HALF].reshape(d, RET_HEADS * RET_HALF)
        x2 = s[:, :, RET_HALF:].reshape(d, RET_HEADS * RET_HALF)
        z = jnp.zeros((d, LANES - RET_HEADS * RET_HALF), w.dtype)
        return jnp.concatenate([x1, z, x2, z], axis=1)

    def head_pad(seg):
        s = seg.reshape(d, RET_HEADS, RET_DV)
        return jnp.pad(s, ((0, 0), (0, 0), (0, RET_HEAD_PAD - RET_DV))).reshape(d, RETV_W)

    o = 0
    segs = []
    for sz in (RET_QK, RET_QK, RET_W, RET_W, POOL_W, FOX_W, FOX_W, FOX_W, FOX_HEADS):
        segs.append(w[:, o:o + sz])
        o += sz
    q_r, k_r, v_r, g_r, u_p, q_f, k_f, v_f, f_l = segs
    packed = jnp.concatenate(
        [rot_pack(q_r), rot_pack(k_r), head_pad(v_r), head_pad(g_r), u_p, q_f, k_f, v_f,
         jnp.pad(f_l, ((0, 0), (0, LANES - FOX_HEADS)))], axis=1)
    return packed.astype(BF16)


def _pack_w_out(w):
    d = w.shape[1]
    w_r = jnp.pad(w[:RET_W].reshape(RET_HEADS, RET_DV, d),
                  ((0, 0), (0, RET_HEAD_PAD - RET_DV), (0, 0))).reshape(RETV_W, d)
    return jnp.concatenate([w_r, w[RET_W:]], axis=0).astype(BF16)


def _pad_heads(vec):
    return jnp.pad(vec.reshape(RET_HEADS, RET_DV), ((0, 0), (0, RET_HEAD_PAD - RET_DV))).reshape(1, RETV_W)


def _block_diag(pw):
    out = jnp.zeros((POOL_W, POOL_W), pw.dtype)
    for g in range(len(POOL_WINDOWS)):
        out = out.at[g * POOL_GROUP:(g + 1) * POOL_GROUP, g * POOL_GROUP:(g + 1) * POOL_GROUP].set(pw[g])
    return out.astype(BF16)


def _retention_tables():
    ch = SEQ_BLOCK
    gamma = (1.0 - 2.0 ** (-5.0 - np.arange(RET_HEADS, dtype=np.float32))).astype(np.float32)
    lg = np.log(gamma).astype(np.float32)
    i = np.arange(ch, dtype=np.float32)
    diff = i[:, None] - i[None, :]
    dm = np.where(diff >= 0, np.exp(lg[:, None, None] * np.maximum(diff, 0.0)), 0.0).astype(np.float32)
    xi = np.exp(lg[:, None] * (i + 1.0)).astype(np.float32)
    zeta = np.exp(lg[:, None] * (ch - 1.0 - i)).astype(np.float32)
    lane = np.arange(2 * LANES)
    within = lane % LANES
    lane_head = np.where(within < RET_HEADS * RET_HALF, within // RET_HALF, -1)
    xiq = np.zeros((ch, 2 * LANES), np.float32)
    zk = np.zeros((ch, 2 * LANES), np.float32)
    bm = np.zeros((2 * LANES, RETV_W), np.float32)
    for h in range(RET_HEADS):
        sel = lane_head == h
        xiq[:, sel] = xi[h][:, None]
        zk[:, sel] = zeta[h][:, None]
        bm[sel, h * LANES:(h + 1) * LANES] = 1.0
    dec = np.repeat(np.exp(lg * ch).astype(np.float32), LANES)[None, :]
    return dict(xiq=jnp.asarray(xiq), zk=jnp.asarray(zk), dm=jnp.asarray(dm),
                dec=jnp.asarray(dec), bm=jnp.asarray(bm))


def _rotary_tables(lp):
    pos = jnp.arange(lp, dtype=F32)
    inv_freq = ROPE_BASE ** (-jnp.arange(RET_HALF, dtype=F32) / RET_HALF)
    ang = pos[:, None] * inv_freq[None, :]
    pad = LANES - RET_HEADS * RET_HALF
    cos = jnp.pad(jnp.tile(jnp.cos(ang), (1, RET_HEADS)), ((0, 0), (0, pad)))
    sin = jnp.pad(jnp.tile(jnp.sin(ang), (1, RET_HEADS)), ((0, 0), (0, pad)))
    return cos, sin


def kernel(x, meta, ln_emb_g, ln_emb_b, w_in, b_f, ret_gn_g, pool_w, pool_scale, w_out, ln1_g, ln1_b,
           w_ffn1, w_ffn3, w_ffn2, ln2_g, ln2_b):
    bsz, seq, d = x.shape
    assert d == D_MODEL and seq % SEQ_BLOCK == 0
    depth = w_in.shape[0]
    assert depth == DEPTH
    lp = seq + SEQ_BLOCK
    rows = bsz * lp
    tm = ROW_TILE if rows % ROW_TILE == 0 else SEQ_BLOCK

    consts = _retention_tables()
    consts["cos"], consts["sin"] = _rotary_tables(lp)

    h = _embed(x, meta, ln_emb_g, ln_emb_b, lp).reshape(rows, d)
    for l in range(depth):
        lw = dict(gn_g=_pad_heads(ret_gn_g[l]),
                  b_f=jnp.pad(b_f[l], (0, LANES - FOX_HEADS)).reshape(1, LANES),
                  pool_w=_block_diag(pool_w[l]),
                  pool_scale=pool_scale[l].reshape(1, POOL_W))
        ret, qkv, flog = _inproj(h, _pack_w_in(w_in[l]), tm)
        o_r, o_p, cum = _seq_mix(ret, flog, consts, lw, bsz, lp)
        o_f = _fox(qkv, cum, bsz, lp)
        h = _outproj(o_r, o_p, o_f, h, _pack_w_out(w_out[l]),
                     ln1_g[l].reshape(1, d), ln1_b[l].reshape(1, d), tm)
        h = _ffn(h, w_ffn1[l].astype(BF16), w_ffn3[l].astype(BF16), w_ffn2[l].astype(BF16),
                 ln2_g[l].reshape(1, d), ln2_b[l].reshape(1, d), tm)
    return h.reshape(bsz, lp, d)[:, N_META:N_META + seq]
```

```python
import functools

import numpy as np
import jax
import jax.numpy as jnp
from jax import lax
from jax.experimental import pallas as pl
from jax.experimental.pallas import tpu as pltpu

F32 = jnp.float32
BF16 = jnp.bfloat16

D_MODEL = 1024
N_META = 16
RET_HEADS = 4
RET_DK = 48
RET_HALF = RET_DK // 2
RET_DV = 96
RET_QK = RET_HEADS * RET_DK
RET_W = RET_HEADS * RET_DV
POOL_WINDOWS = (2, 4, 8, 16)
POOL_GROUP = 64
POOL_W = len(POOL_WINDOWS) * POOL_GROUP
FOX_HEADS = 6
FOX_DH = 64
FOX_W = FOX_HEADS * FOX_DH
D_FF = 2816
ROPE_BASE = 10000.0
LN_EPS = 1e-5
NEG_INF = -1e30
DEPTH = 2
ALPHA = (2.0 * DEPTH) ** 0.25
LOG2E = 1.4426950408889634

LANES = 128
MXU_DIM = 256
SEQ_BLOCK = 256
ROW_TILE = 512
VMEM_LIMIT = 56 * 1024 * 1024

RET_HEAD_PAD = LANES
QR_OFF = 0
KR_OFF = 2 * LANES
VR_OFF = 4 * LANES
GR_OFF = VR_OFF + RET_HEADS * RET_HEAD_PAD
UP_OFF = GR_OFF + RET_HEADS * RET_HEAD_PAD
QF_OFF = UP_OFF + POOL_W
KF_OFF = QF_OFF + FOX_W
VF_OFF = KF_OFF + FOX_W
FL_OFF = VF_OFF + FOX_W
RET_COLS = QF_OFF
FOX_COLS = 3 * FOX_W
N_PACK = FL_OFF + LANES
RETV_W = RET_HEADS * RET_HEAD_PAD
MIX_W = RETV_W + POOL_W + FOX_W
PROJ_CHUNK = 512
FF_CHUNK = 256


def _layer_norm(x, g, b):
    mu = jnp.mean(x, axis=-1, keepdims=True)
    d = x - mu
    var = jnp.mean(d * d, axis=-1, keepdims=True)
    return d * lax.rsqrt(var + LN_EPS) * g + b


def _sigmoid(x):
    return 1.0 / (1.0 + jnp.exp(-x))


def _embed_kernel(meta_ref, xm_ref, xe_ref, g_ref, b_ref, o_ref):
    j = pl.program_id(1)
    last = pl.num_programs(1) - 1
    top = jnp.where(j == 0, meta_ref[...], xe_ref[...])
    body = jnp.where(j == last, 0.0, xm_ref[0:SEQ_BLOCK - N_META, :])
    rows = jnp.concatenate([top, body], axis=0)
    o_ref[...] = _layer_norm(rows, g_ref[...], b_ref[...])


def _embed(x, meta, g, b, lp):
    bsz, seq, d = x.shape
    nblk = lp // SEQ_BLOCK
    n_xblk = seq // SEQ_BLOCK
    per = SEQ_BLOCK // N_META
    return pl.pallas_call(
        _embed_kernel,
        out_shape=jax.ShapeDtypeStruct((bsz, lp, d), F32),
        grid=(bsz, nblk),
        in_specs=[
            pl.BlockSpec((N_META, d), lambda bb, j: (0, 0)),
            pl.BlockSpec((None, SEQ_BLOCK, d), lambda bb, j: (bb, jnp.minimum(j, n_xblk - 1), 0)),
            pl.BlockSpec((None, N_META, d), lambda bb, j: (bb, jnp.maximum(per * j - 1, 0), 0)),
            pl.BlockSpec((1, d), lambda bb, j: (0, 0)),
            pl.BlockSpec((1, d), lambda bb, j: (0, 0)),
        ],
        out_specs=pl.BlockSpec((None, SEQ_BLOCK, d), lambda bb, j: (bb, j, 0)),
        compiler_params=pltpu.CompilerParams(dimension_semantics=("parallel", "arbitrary")),
        name="embed_ln",
    )(meta, x, x, g.reshape(1, d), b.reshape(1, d))


def _inproj_kernel(h_ref, w_ref, cs_ref, ret_ref, fox_ref, flog_ref):
    xb = h_ref[...].astype(BF16)

    def cols(c0, width):
        r = jnp.dot(xb, w_ref[:, c0:c0 + width], preferred_element_type=F32)
        return r * cs_ref[:, c0:c0 + width]

    for c0 in range(0, RET_COLS, PROJ_CHUNK):
        width = min(PROJ_CHUNK, RET_COLS - c0)
        ret_ref[:, c0:c0 + width] = cols(c0, width).astype(BF16)
    for c0 in range(0, FOX_COLS - LANES, PROJ_CHUNK):
        fox_ref[:, c0:c0 + PROJ_CHUNK] = cols(QF_OFF + c0, PROJ_CHUNK).astype(BF16)
    tail = cols(FL_OFF - LANES, 2 * LANES)
    fox_ref[:, FOX_COLS - LANES:] = tail[:, :LANES].astype(BF16)
    flog_ref[...] = tail[:, LANES:]


def _inproj_col_scale():
    cs = np.ones((1, N_PACK), np.float32)
    cs[0, KR_OFF:KR_OFF + 2 * LANES] = RET_DK ** -0.5
    cs[0, QF_OFF:QF_OFF + FOX_W] = FOX_DH ** -0.5 * LOG2E
    return jnp.asarray(cs)


def _inproj(h, w, layer, tm):
    rows = h.shape[0]
    return pl.pallas_call(
        _inproj_kernel,
        out_shape=(jax.ShapeDtypeStruct((rows, RET_COLS), BF16),
                   jax.ShapeDtypeStruct((rows, FOX_COLS), BF16),
                   jax.ShapeDtypeStruct((rows, LANES), F32)),
        grid=(rows // tm,),
        in_specs=[
            pl.BlockSpec((tm, D_MODEL), lambda i: (i, 0)),
            pl.BlockSpec((None, D_MODEL, N_PACK), lambda i: (layer, 0, 0)),
            pl.BlockSpec((1, N_PACK), lambda i: (0, 0)),
        ],
        out_specs=(pl.BlockSpec((tm, RET_COLS), lambda i: (i, 0)),
                   pl.BlockSpec((tm, FOX_COLS), lambda i: (i, 0)),
                   pl.BlockSpec((tm, LANES), lambda i: (i, 0))),
        compiler_params=pltpu.CompilerParams(dimension_semantics=("parallel",),
                                             vmem_limit_bytes=VMEM_LIMIT),
        name="in_proj",
    )(h, w, _inproj_col_scale())


def _seq_kernel(qk_ref, v_ref, g_ref, u_ref, fl_ref, cos_ref, sin_ref,
                xiq_ref, zk_ref, dm_ref, dec_ref, bm_ref, gng_ref, bf_ref, pw_ref, ps_ref,
                or_ref, op_ref, c_ref,
                state_sc, tail_sc, carry_sc):
    j = pl.program_id(1)
    ch = SEQ_BLOCK

    @pl.when(j == 0)
    def _():
        state_sc[...] = jnp.zeros_like(state_sc)
        tail_sc[...] = jnp.zeros_like(tail_sc)
        carry_sc[...] = jnp.zeros_like(carry_sc)

    qk = qk_ref[...].astype(F32)
    cs = cos_ref[...]
    sn = sin_ref[...]
    q1, q2 = qk[:, 0:LANES], qk[:, LANES:2 * LANES]
    k1, k2 = qk[:, 2 * LANES:3 * LANES], qk[:, 3 * LANES:4 * LANES]
    qr = jnp.concatenate([q1 * cs - q2 * sn, q1 * sn + q2 * cs], axis=-1)
    kr = jnp.concatenate([k1 * cs - k2 * sn, k1 * sn + k2 * cs], axis=-1)
    qb = qr.astype(BF16)
    qx = (qr * xiq_ref[...]).astype(BF16)
    kb = kr.astype(BF16)
    kz = (kr * zk_ref[...]).astype(BF16)
    v = v_ref[...]
    st = state_sc[...]
    cross = jnp.dot(qx, st.astype(BF16), preferred_element_type=F32)
    qlane = lax.broadcasted_iota(jnp.int32, (1, 2 * LANES), 1)
    qhead = jnp.where(qlane % LANES < RET_HEADS * RET_HALF, (qlane % LANES) // RET_HALF, RET_HEADS)
    inner = []
    for h in range(RET_HEADS):
        qh = jnp.where(qhead == h, qb, jnp.zeros_like(qb))
        s = lax.dot_general(qh, kb, (((1,), (1,)), ((), ())), preferred_element_type=F32)
        p = (s * dm_ref[h]).astype(BF16)
        inner.append(jnp.dot(p, v[:, h * LANES:(h + 1) * LANES], preferred_element_type=F32))
    o = jnp.concatenate(inner, axis=-1) + cross
    kv = lax.dot_general(kz, v, (((0,), (0,)), ((), ())), preferred_element_type=F32)
    state_sc[...] = st * dec_ref[...] + kv * bm_ref[...]

    vlane = lax.broadcasted_iota(jnp.int32, (1, LANES), 1) < RET_DV
    normed = []
    for h in range(RET_HEADS):
        xh = o[:, h * LANES:(h + 1) * LANES]
        mu = jnp.sum(xh, axis=-1, keepdims=True) * (1.0 / RET_DV)
        d = jnp.where(vlane, xh - mu, 0.0)
        var = jnp.sum(d * d, axis=-1, keepdims=True) * (1.0 / RET_DV)
        normed.append(d * lax.rsqrt(var + LN_EPS))
    y = jnp.concatenate(normed, axis=-1) * gng_ref[...]
    gate = g_ref[...].astype(F32)
    or_ref[...] = (gate * _sigmoid(gate) * y).astype(BF16)

    u = u_ref[...].astype(F32)
    tail_rows = tail_sc.shape[0]
    ext = jnp.concatenate([tail_sc[...], u], axis=0)
    tail_sc[...] = u[ch - tail_rows:, :]
    e2 = ext + pltpu.roll(ext, 1, 0)
    e4 = e2 + pltpu.roll(e2, 2, 0)
    e8 = e4 + pltpu.roll(e4, 4, 0)
    e16 = e8 + pltpu.roll(e8, 8, 0)
    glane = lax.broadcasted_iota(jnp.int32, (1, POOL_W), 1) // POOL_GROUP
    win = jnp.where(glane == 0, e2, jnp.where(glane == 1, e4, jnp.where(glane == 2, e8, e16)))
    win = win[tail_rows:, :]
    wlen = jnp.where(glane == 0, 2, jnp.where(glane == 1, 4, jnp.where(glane == 2, 8, 16)))
    pos = j * ch + lax.broadcasted_iota(jnp.int32, (ch, POOL_W), 0)
    cnt = jnp.minimum(pos + 1, wlen).astype(F32)
    pooled = (win / cnt - u).astype(BF16)
    yp = jnp.dot(pooled, pw_ref[...], preferred_element_type=F32) * ps_ref[...]
    op_ref[...] = yp.astype(BF16)

    z = fl_ref[...] + bf_ref[...]
    logf = jnp.minimum(z, 0.0) - jnp.log1p(jnp.exp(-jnp.abs(z)))
    row = lax.broadcasted_iota(jnp.int32, (ch, LANES), 0)
    sh = 1
    while sh < ch:
        logf = logf + jnp.where(row >= sh, pltpu.roll(logf, sh, 0), 0.0)
        sh *= 2
    c = logf + carry_sc[0:1, :]
    carry_sc[...] = jnp.broadcast_to(c[ch - 1:ch, :], carry_sc.shape)
    c_ref[...] = c * LOG2E


def _seq_mix(proj, flog, consts, params, layer, bsz, lp):
    ch = SEQ_BLOCK
    nc = lp // ch
    rows = bsz * lp

    def rowblk(width, colblk):
        return pl.BlockSpec((ch, width), lambda bb, j: (bb * nc + j, colblk))

    def const(shape):
        nd = len(shape)
        return pl.BlockSpec(shape, lambda bb, j: (0,) * nd)

    def layer_param(shape):
        return pl.BlockSpec((None,) + shape, lambda bb, j: (layer,) + (0,) * len(shape))

    in_specs = [
        rowblk(4 * LANES, QR_OFF // (4 * LANES)),
        rowblk(RETV_W, VR_OFF // RETV_W),
        rowblk(RETV_W, GR_OFF // RETV_W),
        rowblk(POOL_W, UP_OFF // POOL_W),
        pl.BlockSpec((ch, LANES), lambda bb, j: (bb * nc + j, 0)),
        pl.BlockSpec((ch, LANES), lambda bb, j: (j, 0)),
        pl.BlockSpec((ch, LANES), lambda bb, j: (j, 0)),
        const((ch, 2 * LANES)), const((ch, 2 * LANES)), const((RET_HEADS, ch, ch)),
        const((1, RETV_W)), const((2 * LANES, RETV_W)),
        layer_param((1, RETV_W)), layer_param((1, LANES)), layer_param((POOL_W, POOL_W)),
        layer_param((1, POOL_W)),
    ]
    out_shape = (jax.ShapeDtypeStruct((rows, RETV_W), BF16),
                 jax.ShapeDtypeStruct((rows, POOL_W), BF16),
                 jax.ShapeDtypeStruct((rows, LANES), F32))
    out_specs = (pl.BlockSpec((ch, RETV_W), lambda bb, j: (bb * nc + j, 0)),
                 pl.BlockSpec((ch, POOL_W), lambda bb, j: (bb * nc + j, 0)),
                 pl.BlockSpec((ch, LANES), lambda bb, j: (bb * nc + j, 0)))
    return pl.pallas_call(
        _seq_kernel,
        out_shape=out_shape,
        grid=(bsz, nc),
        in_specs=in_specs,
        out_specs=out_specs,
        scratch_shapes=[pltpu.VMEM((2 * LANES, RETV_W), F32),
                        pltpu.VMEM((16, POOL_W), F32),
                        pltpu.VMEM((8, LANES), F32)],
        compiler_params=pltpu.CompilerParams(dimension_semantics=("parallel", "arbitrary"),
                                             vmem_limit_bytes=VMEM_LIMIT),
        name="seq_mixers",
    )(proj, proj, proj, proj, flog, consts["cos"], consts["sin"],
      consts["xiq"], consts["zk"], consts["dm"], consts["dec"], consts["bm"],
      params["gn_g"], params["b_f"], params["pool_w"], params["pool_scale"])


FOX_AUG = 3
FOX_VROWS = FOX_DH + 16


def _fox_kernel(q_ref, k_ref, v_ref, c_ref, o_ref,
                ka_sc, vt_sc, qa_sc, st_sc, m_sc, acc_sc):
    i = pl.program_id(1)
    tq = SEQ_BLOCK
    tk = SEQ_BLOCK
    npair = FOX_HEADS // 2
    lane = lax.broadcasted_iota(jnp.int32, (1, LANES), 1)

    @pl.when(i == 0)
    def _():
        def chunk(t, carry):
            r0 = pl.multiple_of(t * tk, tk)
            cc = c_ref[pl.ds(r0, tk), :]
            for pr in range(npair):
                kk = k_ref[pl.ds(r0, tk), pr * LANES:(pr + 1) * LANES].astype(F32)
                vv = v_ref[pl.ds(r0, tk), pr * LANES:(pr + 1) * LANES].astype(F32)
                vtt = vv.T
                ones = jnp.ones((FOX_VROWS - FOX_DH, tk), F32)
                k_heads = (kk, pltpu.roll(kk, FOX_DH, 1))
                v_heads = (vtt[:FOX_DH], vtt[FOX_DH:])
                for hh in range(2):
                    h = 2 * pr + hh
                    cb = jnp.broadcast_to(cc[:, h:h + 1], (tk, LANES))
                    aug = jnp.where(lane < FOX_DH, k_heads[hh], 0.0)
                    rem = cb
                    for a in range(FOX_AUG):
                        piece = rem.astype(BF16).astype(F32)
                        aug = jnp.where(lane == FOX_DH + a, -piece, aug)
                        rem = rem - piece
                    ka_sc[h, pl.ds(r0, tk), :] = aug.astype(BF16)
                    vt_sc[h, :, pl.ds(r0, tk)] = jnp.concatenate([v_heads[hh], ones], axis=0).astype(BF16)
            return carry

        lax.fori_loop(0, k_ref.shape[0] // tk, chunk, 0)

    for pr in range(npair):
        qq = q_ref[:, pr * LANES:(pr + 1) * LANES].astype(F32)
        tail = jnp.where(lane < FOX_DH + FOX_AUG, 1.0, 0.0)
        qa_sc[2 * pr] = jnp.where(lane < FOX_DH, qq, tail).astype(BF16)
        qa_sc[2 * pr + 1] = jnp.where(lane < FOX_DH, pltpu.roll(qq, FOX_DH, 1), tail).astype(BF16)
    m_sc[...] = jnp.full(m_sc.shape, NEG_INF, F32)
    acc_sc[...] = jnp.zeros_like(acc_sc)

    def scores(t):
        ks = t * tk if isinstance(t, int) else pl.multiple_of(t * tk, tk)
        for h in range(FOX_HEADS):
            st_sc[h] = lax.dot_general(ka_sc[h, pl.ds(ks, tk), :], qa_sc[h], (((1,), (1,)), ((), ())),
                                       preferred_element_type=F32)

    def attend(t, masked):
        ks = t * tk if isinstance(t, int) else pl.multiple_of(t * tk, tk)

        def load(h, c0):
            s = st_sc[h, :, c0:c0 + LANES]
            if masked:
                k_id = lax.broadcasted_iota(jnp.int32, (tk, LANES), 0)
                q_id = c0 + lax.broadcasted_iota(jnp.int32, (tk, LANES), 1)
                s = jnp.where(k_id <= q_id, s, NEG_INF)
            return s

        for h in range(FOX_HEADS):
            p_halves, a_halves = [], []
            for c0 in range(0, tq, LANES):
                m_prev = m_sc[h, :, c0:c0 + LANES]
                m_new = jnp.maximum(m_prev, jnp.max(load(h, c0), axis=0, keepdims=True))
                m_sc[h, :, c0:c0 + LANES] = m_new
                a_halves.append(jnp.exp2(m_prev - m_new))
                p_halves.append(jnp.exp2(load(h, c0) - m_new).astype(BF16))
            p_t = jnp.concatenate(p_halves, axis=1)
            pv = jnp.dot(vt_sc[h, :, pl.ds(ks, tk)], p_t, preferred_element_type=F32)
            acc_sc[h] = jnp.concatenate(a_halves, axis=1) * acc_sc[h] + pv

    scores(0)

    def body(t, carry):
        attend(t, False)
        scores(t + 1)
        return carry

    lax.fori_loop(0, i, body, 0)
    attend(i, True)

    outs = []
    for pr in range(npair):
        acc_a = acc_sc[2 * pr]
        acc_b = acc_sc[2 * pr + 1]
        o_t = jnp.concatenate([acc_a[:FOX_DH] / acc_a[FOX_DH:FOX_DH + 1],
                               acc_b[:FOX_DH] / acc_b[FOX_DH:FOX_DH + 1]], axis=0)
        outs.append(o_t.T)
    o_ref[...] = jnp.concatenate(outs, axis=-1).astype(BF16)


def _fox(qkv, cum, bsz, lp):
    tq = SEQ_BLOCK
    nq = lp // tq
    rows = bsz * lp
    return pl.pallas_call(
        _fox_kernel,
        out_shape=jax.ShapeDtypeStruct((rows, FOX_W), BF16),
        grid=(bsz, nq),
        in_specs=[
            pl.BlockSpec((tq, FOX_W), lambda bb, i: (bb * nq + i, 0)),
            pl.BlockSpec((lp, FOX_W), lambda bb, i: (bb, 1)),
            pl.BlockSpec((lp, FOX_W), lambda bb, i: (bb, 2)),
            pl.BlockSpec((lp, LANES), lambda bb, i: (bb, 0)),
        ],
        out_specs=pl.BlockSpec((tq, FOX_W), lambda bb, i: (bb * nq + i, 0)),
        scratch_shapes=[pltpu.VMEM((FOX_HEADS, lp, LANES), BF16),
                        pltpu.VMEM((FOX_HEADS, FOX_VROWS, lp), BF16),
                        pltpu.VMEM((FOX_HEADS, tq, LANES), BF16),
                        pltpu.VMEM((FOX_HEADS, SEQ_BLOCK, tq), F32),
                        pltpu.VMEM((FOX_HEADS, 1, tq), F32),
                        pltpu.VMEM((FOX_HEADS, FOX_VROWS, tq), F32)],
        compiler_params=pltpu.CompilerParams(
            dimension_semantics=("parallel", "arbitrary"),
            vmem_limit_bytes=VMEM_LIMIT),
        name="fox_attention",
    )(qkv, qkv, qkv, cum)


def _mix_ffn_kernel(or_ref, op_ref, of_ref, h_ref, wo_ref, g1_ref, b1_ref,
                    w1_ref, w3_ref, w2_ref, g2_ref, b2_ref, o_ref):
    mix = jnp.concatenate([or_ref[...], op_ref[...], of_ref[...]], axis=-1)
    y = ALPHA * h_ref[...] + jnp.dot(mix, wo_ref[...], preferred_element_type=F32)
    h1 = _layer_norm(y, g1_ref[...], b1_ref[...])
    xb = h1.astype(BF16)
    acc = None
    for c0 in range(0, D_FF, FF_CHUNK):
        a = jnp.dot(xb, w1_ref[:, c0:c0 + FF_CHUNK], preferred_element_type=F32)
        b = jnp.dot(xb, w3_ref[:, c0:c0 + FF_CHUNK], preferred_element_type=F32)
        t = (a * _sigmoid(a) * b).astype(BF16)
        part = jnp.dot(t, w2_ref[c0:c0 + FF_CHUNK, :], preferred_element_type=F32)
        acc = part if acc is None else acc + part
    o_ref[...] = _layer_norm(ALPHA * h1 + acc, g2_ref[...], b2_ref[...])


def _mix_ffn(o_r, o_p, o_f, h, params, layer, tm):
    rows = h.shape[0]

    def row(width):
        return pl.BlockSpec((tm, width), lambda i: (i, 0))

    def resident(shape):
        return pl.BlockSpec((None,) + shape, lambda i: (layer,) + (0,) * len(shape),
                            pipeline_mode=pl.Buffered(1))

    vec = resident((1, D_MODEL))
    return pl.pallas_call(
        _mix_ffn_kernel,
        out_shape=jax.ShapeDtypeStruct((rows, D_MODEL), F32),
        grid=(rows // tm,),
        in_specs=[row(RETV_W), row(POOL_W), row(FOX_W), row(D_MODEL),
                  resident((MIX_W, D_MODEL)), vec, vec,
                  resident((D_MODEL, D_FF)), resident((D_MODEL, D_FF)), resident((D_FF, D_MODEL)), vec, vec],
        out_specs=row(D_MODEL),
        compiler_params=pltpu.CompilerParams(dimension_semantics=("parallel",),
                                             vmem_limit_bytes=VMEM_LIMIT),
        name="mix_ffn_ln",
    )(o_r, o_p, o_f, h, params["w_out"], params["ln1_g"], params["ln1_b"],
      params["w1"], params["w3"], params["w2"], params["ln2_g"], params["ln2_b"])


def _pack_w_in(w):
    d = w.shape[0]

    def rot_pack(seg):
        s = seg.reshape(d, RET_HEADS, RET_DK)
        x1 = s[:, :, :RET_HALF].reshape(d, RET_HEADS * RET_HALF)
        x2 = s[:, :, RET_HALF:].reshape(d, RET_HEADS * RET_HALF)
        z = jnp.zeros((d, LANES - RET_HEADS * RET_HALF), w.dtype)
        return jnp.concatenate([x1, z, x2, z], axis=1)

    def head_pad(seg):
        s = seg.reshape(d, RET_HEADS, RET_DV)
        return jnp.pad(s, ((0, 0), (0, 0), (0, RET_HEAD_PAD - RET_DV))).reshape(d, RETV_W)

    o = 0
    segs = []
    for sz in (RET_QK, RET_QK, RET_W, RET_W, POOL_W, FOX_W, FOX_W, FOX_W, FOX_HEADS):
        segs.append(w[:, o:o + sz])
        o += sz
    q_r, k_r, v_r, g_r, u_p, q_f, k_f, v_f, f_l = segs
    packed = jnp.concatenate(
        [rot_pack(q_r), rot_pack(k_r), head_pad(v_r), head_pad(g_r), u_p, q_f, k_f, v_f,
         jnp.pad(f_l, ((0, 0), (0, LANES - FOX_HEADS)))], axis=1)
    return packed.astype(BF16)


def _pack_w_out(w):
    d = w.shape[1]
    w_r = jnp.pad(w[:RET_W].reshape(RET_HEADS, RET_DV, d),
                  ((0, 0), (0, RET_HEAD_PAD - RET_DV), (0, 0))).reshape(RETV_W, d)
    return jnp.concatenate([w_r, w[RET_W:]], axis=0).astype(BF16)


def _pad_heads(vec):
    return jnp.pad(vec.reshape(RET_HEADS, RET_DV), ((0, 0), (0, RET_HEAD_PAD - RET_DV))).reshape(1, RETV_W)


def _block_diag(pw):
    g = len(POOL_WINDOWS)
    eye = jnp.eye(g, dtype=pw.dtype)
    return jnp.einsum("gij,gh->gihj", pw, eye).reshape(POOL_W, POOL_W).astype(BF16)


def _prepare_params(w_in, b_f, ret_gn_g, pool_w, pool_scale, w_out, ln1_g, ln1_b,
                    w_ffn1, w_ffn3, w_ffn2, ln2_g, ln2_b):
    depth = w_in.shape[0]
    vec = lambda a: a.reshape(depth, 1, a.shape[-1])
    return dict(
        w_in=jax.vmap(_pack_w_in)(w_in),
        gn_g=jax.vmap(_pad_heads)(ret_gn_g),
        b_f=vec(jnp.pad(b_f, ((0, 0), (0, LANES - FOX_HEADS)))),
        pool_w=jax.vmap(_block_diag)(pool_w),
        pool_scale=vec(pool_scale),
        w_out=jax.vmap(_pack_w_out)(w_out),
        ln1_g=vec(ln1_g), ln1_b=vec(ln1_b), ln2_g=vec(ln2_g), ln2_b=vec(ln2_b),
        w1=w_ffn1.astype(BF16), w3=w_ffn3.astype(BF16), w2=w_ffn2.astype(BF16))


def _retention_tables():
    ch = SEQ_BLOCK
    gamma = (1.0 - 2.0 ** (-5.0 - np.arange(RET_HEADS, dtype=np.float32))).astype(np.float32)
    lg = np.log(gamma).astype(np.float32)
    i = np.arange(ch, dtype=np.float32)
    diff = i[:, None] - i[None, :]
    dm = np.where(diff >= 0, np.exp(lg[:, None, None] * np.maximum(diff, 0.0)), 0.0).astype(np.float32)
    xi = np.exp(lg[:, None] * (i + 1.0)).astype(np.float32)
    zeta = np.exp(lg[:, None] * (ch - 1.0 - i)).astype(np.float32)
    lane = np.arange(2 * LANES)
    within = lane % LANES
    lane_head = np.where(within < RET_HEADS * RET_HALF, within // RET_HALF, -1)
    xiq = np.zeros((ch, 2 * LANES), np.float32)
    zk = np.zeros((ch, 2 * LANES), np.float32)
    bm = np.zeros((2 * LANES, RETV_W), np.float32)
    for h in range(RET_HEADS):
        sel = lane_head == h
        xiq[:, sel] = xi[h][:, None]
        zk[:, sel] = zeta[h][:, None]
        bm[sel, h * LANES:(h + 1) * LANES] = 1.0
    dec = np.repeat(np.exp(lg * ch).astype(np.float32), LANES)[None, :]
    return dict(xiq=jnp.asarray(xiq), zk=jnp.asarray(zk), dm=jnp.asarray(dm),
                dec=jnp.asarray(dec), bm=jnp.asarray(bm))


def _rotary_tables(lp):
    pos = jnp.arange(lp, dtype=F32)
    inv_freq = ROPE_BASE ** (-jnp.arange(RET_HALF, dtype=F32) / RET_HALF)
    ang = pos[:, None] * inv_freq[None, :]
    pad = LANES - RET_HEADS * RET_HALF
    cos = jnp.pad(jnp.tile(jnp.cos(ang), (1, RET_HEADS)), ((0, 0), (0, pad)))
    sin = jnp.pad(jnp.tile(jnp.sin(ang), (1, RET_HEADS)), ((0, 0), (0, pad)))
    return cos, sin


def kernel(x, meta, ln_emb_g, ln_emb_b, w_in, b_f, ret_gn_g, pool_w, pool_scale, w_out, ln1_g, ln1_b,
           w_ffn1, w_ffn3, w_ffn2, ln2_g, ln2_b):
    bsz, seq, d = x.shape
    assert d == D_MODEL and seq % SEQ_BLOCK == 0
    depth = w_in.shape[0]
    assert depth == DEPTH
    lp = seq + SEQ_BLOCK
    rows = bsz * lp
    tm = ROW_TILE if rows % ROW_TILE == 0 else SEQ_BLOCK

    consts = _retention_tables()
    consts["cos"], consts["sin"] = _rotary_tables(lp)
    params = _prepare_params(w_in, b_f, ret_gn_g, pool_w, pool_scale, w_out, ln1_g, ln1_b,
                             w_ffn1, w_ffn3, w_ffn2, ln2_g, ln2_b)

    h = _embed(x, meta, ln_emb_g, ln_emb_b, lp).reshape(rows, d)
    for l in range(depth):
        ret, qkv, flog = _inproj(h, params["w_in"], l, tm)
        o_r, o_p, cum = _seq_mix(ret, flog, consts, params, l, bsz, lp)
        o_f = _fox(qkv, cum, bsz, lp)
        h = _mix_ffn(o_r, o_p, o_f, h, params, l, tm)
    return h.reshape(bsz, lp, d)[:, N_META:N_META + seq]
```

```python
import functools

import numpy as np
import jax
import jax.numpy as jnp
from jax import lax
from jax.experimental import pallas as pl
from jax.experimental.pallas import tpu as pltpu

F32 = jnp.float32
BF16 = jnp.bfloat16

D_MODEL = 1024
N_META = 16
RET_HEADS = 4
RET_DK = 48
RET_HALF = RET_DK // 2
RET_DV = 96
RET_QK = RET_HEADS * RET_DK
RET_W = RET_HEADS * RET_DV
POOL_WINDOWS = (2, 4, 8, 16)
POOL_GROUP = 64
POOL_W = len(POOL_WINDOWS) * POOL_GROUP
FOX_HEADS = 6
FOX_DH = 64
FOX_W = FOX_HEADS * FOX_DH
D_FF = 2816
ROPE_BASE = 10000.0
LN_EPS = 1e-5
NEG_INF = -1e30
DEPTH = 2
ALPHA = (2.0 * DEPTH) ** 0.25
LOG2E = 1.4426950408889634

LANES = 128
MXU_DIM = 256
SEQ_BLOCK = 256
ROW_TILE = 512
VMEM_LIMIT = 56 * 1024 * 1024

RET_HEAD_PAD = LANES
QR_OFF = 0
KR_OFF = 2 * LANES
VR_OFF = 4 * LANES
GR_OFF = VR_OFF + RET_HEADS * RET_HEAD_PAD
UP_OFF = GR_OFF + RET_HEADS * RET_HEAD_PAD
QF_OFF = UP_OFF + POOL_W
KF_OFF = QF_OFF + FOX_W
VF_OFF = KF_OFF + FOX_W
FL_OFF = VF_OFF + FOX_W
RET_COLS = QF_OFF
FOX_COLS = 3 * FOX_W
N_PACK = FL_OFF + LANES
RETV_W = RET_HEADS * RET_HEAD_PAD
MIX_W = RETV_W + POOL_W + FOX_W
PROJ_CHUNK = 512
FF_CHUNK = 256


def _layer_norm(x, g, b):
    mu = jnp.mean(x, axis=-1, keepdims=True)
    d = x - mu
    var = jnp.mean(d * d, axis=-1, keepdims=True)
    return d * lax.rsqrt(var + LN_EPS) * g + b


def _sigmoid(x):
    return 1.0 / (1.0 + jnp.exp(-x))


def _embed_kernel(meta_ref, xm_ref, xe_ref, g_ref, b_ref, o_ref):
    j = pl.program_id(1)
    last = pl.num_programs(1) - 1
    top = jnp.where(j == 0, meta_ref[...], xe_ref[...])
    body = jnp.where(j == last, 0.0, xm_ref[0:SEQ_BLOCK - N_META, :])
    rows = jnp.concatenate([top, body], axis=0)
    o_ref[...] = _layer_norm(rows, g_ref[...], b_ref[...])


def _embed(x, meta, g, b, lp):
    bsz, seq, d = x.shape
    nblk = lp // SEQ_BLOCK
    n_xblk = seq // SEQ_BLOCK
    per = SEQ_BLOCK // N_META
    return pl.pallas_call(
        _embed_kernel,
        out_shape=jax.ShapeDtypeStruct((bsz, lp, d), F32),
        grid=(bsz, nblk),
        in_specs=[
            pl.BlockSpec((N_META, d), lambda bb, j: (0, 0)),
            pl.BlockSpec((None, SEQ_BLOCK, d), lambda bb, j: (bb, jnp.minimum(j, n_xblk - 1), 0)),
            pl.BlockSpec((None, N_META, d), lambda bb, j: (bb, jnp.maximum(per * j - 1, 0), 0)),
            pl.BlockSpec((1, d), lambda bb, j: (0, 0)),
            pl.BlockSpec((1, d), lambda bb, j: (0, 0)),
        ],
        out_specs=pl.BlockSpec((None, SEQ_BLOCK, d), lambda bb, j: (bb, j, 0)),
        compiler_params=pltpu.CompilerParams(dimension_semantics=("parallel", "arbitrary")),
        name="embed_ln",
    )(meta, x, x, g.reshape(1, d), b.reshape(1, d))


def _inproj_kernel(h_ref, w_ref, cs_ref, ret_ref, fox_ref, flog_ref):
    xb = h_ref[...].astype(BF16)

    def cols(c0, width):
        r = jnp.dot(xb, w_ref[:, c0:c0 + width], preferred_element_type=F32)
        return r * cs_ref[:, c0:c0 + width]

    for c0 in range(0, RET_COLS, PROJ_CHUNK):
        width = min(PROJ_CHUNK, RET_COLS - c0)
        ret_ref[:, c0:c0 + width] = cols(c0, width).astype(BF16)
    for c0 in range(0, FOX_COLS - LANES, PROJ_CHUNK):
        fox_ref[:, c0:c0 + PROJ_CHUNK] = cols(QF_OFF + c0, PROJ_CHUNK).astype(BF16)
    tail = cols(FL_OFF - LANES, 2 * LANES)
    fox_ref[:, FOX_COLS - LANES:] = tail[:, :LANES].astype(BF16)
    flog_ref[...] = tail[:, LANES:]


def _inproj_col_scale():
    cs = np.ones((1, N_PACK), np.float32)
    cs[0, KR_OFF:KR_OFF + 2 * LANES] = RET_DK ** -0.5
    cs[0, QF_OFF:QF_OFF + FOX_W] = FOX_DH ** -0.5 * LOG2E
    return jnp.asarray(cs)


def _inproj(h, w, layer, tm):
    rows = h.shape[0]
    return pl.pallas_call(
        _inproj_kernel,
        out_shape=(jax.ShapeDtypeStruct((rows, RET_COLS), BF16),
                   jax.ShapeDtypeStruct((rows, FOX_COLS), BF16),
                   jax.ShapeDtypeStruct((rows, LANES), F32)),
        grid=(rows // tm,),
        in_specs=[
            pl.BlockSpec((tm, D_MODEL), lambda i: (i, 0)),
            pl.BlockSpec((None, D_MODEL, N_PACK), lambda i: (layer, 0, 0)),
            pl.BlockSpec((1, N_PACK), lambda i: (0, 0)),
        ],
        out_specs=(pl.BlockSpec((tm, RET_COLS), lambda i: (i, 0)),
                   pl.BlockSpec((tm, FOX_COLS), lambda i: (i, 0)),
                   pl.BlockSpec((tm, LANES), lambda i: (i, 0))),
        compiler_params=pltpu.CompilerParams(dimension_semantics=("parallel",),
                                             vmem_limit_bytes=VMEM_LIMIT),
        name="in_proj",
    )(h, w, _inproj_col_scale())


def _seq_kernel(qk_ref, v_ref, g_ref, u_ref, fl_ref, cos_ref, sin_ref,
                xiq_ref, zk_ref, dm_ref, dec_ref, bm_ref, gng_ref, bf_ref, pw_ref, ps_ref,
                or_ref, op_ref, c_ref,
                state_sc, tail_sc, carry_sc):
    j = pl.program_id(1)
    ch = SEQ_BLOCK

    @pl.when(j == 0)
    def _():
        state_sc[...] = jnp.zeros_like(state_sc)
        tail_sc[...] = jnp.zeros_like(tail_sc)
        carry_sc[...] = jnp.zeros_like(carry_sc)

    qk = qk_ref[...].astype(F32)
    cs = cos_ref[...]
    sn = sin_ref[...]
    q1, q2 = qk[:, 0:LANES], qk[:, LANES:2 * LANES]
    k1, k2 = qk[:, 2 * LANES:3 * LANES], qk[:, 3 * LANES:4 * LANES]
    qr = jnp.concatenate([q1 * cs - q2 * sn, q1 * sn + q2 * cs], axis=-1)
    kr = jnp.concatenate([k1 * cs - k2 * sn, k1 * sn + k2 * cs], axis=-1)
    qb = qr.astype(BF16)
    qx = (qr * xiq_ref[...]).astype(BF16)
    kb = kr.astype(BF16)
    kz = (kr * zk_ref[...]).astype(BF16)
    v = v_ref[...]
    st = state_sc[...]
    cross = jnp.dot(qx, st.astype(BF16), preferred_element_type=F32)
    qlane = lax.broadcasted_iota(jnp.int32, (1, 2 * LANES), 1)
    qhead = jnp.where(qlane % LANES < RET_HEADS * RET_HALF, (qlane % LANES) // RET_HALF, RET_HEADS)
    inner = []
    for h in range(RET_HEADS):
        qh = jnp.where(qhead == h, qb, jnp.zeros_like(qb))
        s = lax.dot_general(qh, kb, (((1,), (1,)), ((), ())), preferred_element_type=F32)
        p = (s * dm_ref[h]).astype(BF16)
        inner.append(jnp.dot(p, v[:, h * LANES:(h + 1) * LANES], preferred_element_type=F32))
    o = jnp.concatenate(inner, axis=-1) + cross
    kv = lax.dot_general(kz, v, (((0,), (0,)), ((), ())), preferred_element_type=F32)
    state_sc[...] = st * dec_ref[...] + kv * bm_ref[...]

    vlane = lax.broadcasted_iota(jnp.int32, (1, LANES), 1) < RET_DV
    normed = []
    for h in range(RET_HEADS):
        xh = o[:, h * LANES:(h + 1) * LANES]
        mu = jnp.sum(xh, axis=-1, keepdims=True) * (1.0 / RET_DV)
        d = jnp.where(vlane, xh - mu, 0.0)
        var = jnp.sum(d * d, axis=-1, keepdims=True) * (1.0 / RET_DV)
        normed.append(d * lax.rsqrt(var + LN_EPS))
    y = jnp.concatenate(normed, axis=-1) * gng_ref[...]
    gate = g_ref[...].astype(F32)
    or_ref[...] = (gate * _sigmoid(gate) * y).astype(BF16)

    u = u_ref[...].astype(F32)
    tail_rows = tail_sc.shape[0]
    ext = jnp.concatenate([tail_sc[...], u], axis=0)
    tail_sc[...] = u[ch - tail_rows:, :]
    e2 = ext + pltpu.roll(ext, 1, 0)
    e4 = e2 + pltpu.roll(e2, 2, 0)
    e8 = e4 + pltpu.roll(e4, 4, 0)
    e16 = e8 + pltpu.roll(e8, 8, 0)
    glane = lax.broadcasted_iota(jnp.int32, (1, POOL_W), 1) // POOL_GROUP
    win = jnp.where(glane == 0, e2, jnp.where(glane == 1, e4, jnp.where(glane == 2, e8, e16)))
    win = win[tail_rows:, :]
    wlen = jnp.where(glane == 0, 2, jnp.where(glane == 1, 4, jnp.where(glane == 2, 8, 16)))
    pos = j * ch + lax.broadcasted_iota(jnp.int32, (ch, POOL_W), 0)
    cnt = jnp.minimum(pos + 1, wlen).astype(F32)
    pooled = (win / cnt - u).astype(BF16)
    yp = jnp.dot(pooled, pw_ref[...], preferred_element_type=F32) * ps_ref[...]
    op_ref[...] = yp.astype(BF16)

    z = fl_ref[...] + bf_ref[...]
    logf = jnp.minimum(z, 0.0) - jnp.log1p(jnp.exp(-jnp.abs(z)))
    row = lax.broadcasted_iota(jnp.int32, (ch, LANES), 0)
    sh = 1
    while sh < ch:
        logf = logf + jnp.where(row >= sh, pltpu.roll(logf, sh, 0), 0.0)
        sh *= 2
    c = logf + carry_sc[0:1, :]
    carry_sc[...] = jnp.broadcast_to(c[ch - 1:ch, :], carry_sc.shape)
    c_ref[...] = c * LOG2E


def _seq_mix(proj, flog, consts, params, layer, bsz, lp):
    ch = SEQ_BLOCK
    nc = lp // ch
    rows = bsz * lp

    def rowblk(width, colblk):
        return pl.BlockSpec((ch, width), lambda bb, j: (bb * nc + j, colblk))

    def const(shape):
        nd = len(shape)
        return pl.BlockSpec(shape, lambda bb, j: (0,) * nd)

    def layer_param(shape):
        return pl.BlockSpec((None,) + shape, lambda bb, j: (layer,) + (0,) * len(shape))

    in_specs = [
        rowblk(4 * LANES, QR_OFF // (4 * LANES)),
        rowblk(RETV_W, VR_OFF // RETV_W),
        rowblk(RETV_W, GR_OFF // RETV_W),
        rowblk(POOL_W, UP_OFF // POOL_W),
        pl.BlockSpec((ch, LANES), lambda bb, j: (bb * nc + j, 0)),
        pl.BlockSpec((ch, LANES), lambda bb, j: (j, 0)),
        pl.BlockSpec((ch, LANES), lambda bb, j: (j, 0)),
        const((ch, 2 * LANES)), const((ch, 2 * LANES)), const((RET_HEADS, ch, ch)),
        const((1, RETV_W)), const((2 * LANES, RETV_W)),
        layer_param((1, RETV_W)), layer_param((1, LANES)), layer_param((POOL_W, POOL_W)),
        layer_param((1, POOL_W)),
    ]
    out_shape = (jax.ShapeDtypeStruct((rows, RETV_W), BF16),
                 jax.ShapeDtypeStruct((rows, POOL_W), BF16),
                 jax.ShapeDtypeStruct((rows, LANES), F32))
    out_specs = (pl.BlockSpec((ch, RETV_W), lambda bb, j: (bb * nc + j, 0)),
                 pl.BlockSpec((ch, POOL_W), lambda bb, j: (bb * nc + j, 0)),
                 pl.BlockSpec((ch, LANES), lambda bb, j: (bb * nc + j, 0)))
    return pl.pallas_call(
        _seq_kernel,
        out_shape=out_shape,
        grid=(bsz, nc),
        in_specs=in_specs,
        out_specs=out_specs,
        scratch_shapes=[pltpu.VMEM((2 * LANES, RETV_W), F32),
                        pltpu.VMEM((16, POOL_W), F32),
                        pltpu.VMEM((8, LANES), F32)],
        compiler_params=pltpu.CompilerParams(dimension_semantics=("parallel", "arbitrary"),
                                             vmem_limit_bytes=VMEM_LIMIT),
        name="seq_mixers",
    )(proj, proj, proj, proj, flog, consts["cos"], consts["sin"],
      consts["xiq"], consts["zk"], consts["dm"], consts["dec"], consts["bm"],
      params["gn_g"], params["b_f"], params["pool_w"], params["pool_scale"])


FOX_AUG = 3
FOX_VROWS = FOX_DH + 16
FOX_LATE_HEADS = 2


def _fox_kernel(q_ref, k_ref, v_ref, c_ref, o_ref,
                ka_sc, vt_sc, qa_sc, st_sc, m_sc, acc_sc):
    i = pl.program_id(1)
    tq = SEQ_BLOCK
    tk = SEQ_BLOCK
    npair = FOX_HEADS // 2
    lane = lax.broadcasted_iota(jnp.int32, (1, LANES), 1)

    @pl.when(i == 0)
    def _():
        def chunk(t, carry):
            r0 = pl.multiple_of(t * tk, tk)
            cc = c_ref[pl.ds(r0, tk), :]
            for pr in range(npair):
                kk = k_ref[pl.ds(r0, tk), pr * LANES:(pr + 1) * LANES].astype(F32)
                vv = v_ref[pl.ds(r0, tk), pr * LANES:(pr + 1) * LANES].astype(F32)
                vtt = vv.T
                ones = jnp.ones((FOX_VROWS - FOX_DH, tk), F32)
                k_heads = (kk, pltpu.roll(kk, FOX_DH, 1))
                v_heads = (vtt[:FOX_DH], vtt[FOX_DH:])
                for hh in range(2):
                    h = 2 * pr + hh
                    cb = jnp.broadcast_to(cc[:, h:h + 1], (tk, LANES))
                    aug = jnp.where(lane < FOX_DH, k_heads[hh], 0.0)
                    rem = cb
                    for a in range(FOX_AUG):
                        piece = rem.astype(BF16).astype(F32)
                        aug = jnp.where(lane == FOX_DH + a, -piece, aug)
                        rem = rem - piece
                    ka_sc[h, pl.ds(r0, tk), :] = aug.astype(BF16)
                    vt_sc[h, :, pl.ds(r0, tk)] = jnp.concatenate([v_heads[hh], ones], axis=0).astype(BF16)
            return carry

        lax.fori_loop(0, k_ref.shape[0] // tk, chunk, 0)

    for pr in range(npair):
        qq = q_ref[:, pr * LANES:(pr + 1) * LANES].astype(F32)
        tail = jnp.where(lane < FOX_DH + FOX_AUG, 1.0, 0.0)
        qa_sc[2 * pr] = jnp.where(lane < FOX_DH, qq, tail).astype(BF16)
        qa_sc[2 * pr + 1] = jnp.where(lane < FOX_DH, pltpu.roll(qq, FOX_DH, 1), tail).astype(BF16)
    m_sc[...] = jnp.full(m_sc.shape, NEG_INF, F32)
    acc_sc[...] = jnp.zeros_like(acc_sc)

    def score(t, h):
        ks = t * tk if isinstance(t, int) else pl.multiple_of(t * tk, tk)
        st_sc[h] = lax.dot_general(ka_sc[h, pl.ds(ks, tk), :], qa_sc[h], (((1,), (1,)), ((), ())),
                                   preferred_element_type=F32)

    def attend(t, masked, next_t):
        ks = t * tk if isinstance(t, int) else pl.multiple_of(t * tk, tk)

        def load(h, c0):
            s = st_sc[h, :, c0:c0 + LANES]
            if masked:
                k_id = lax.broadcasted_iota(jnp.int32, (tk, LANES), 0)
                q_id = c0 + lax.broadcasted_iota(jnp.int32, (tk, LANES), 1)
                s = jnp.where(k_id <= q_id, s, NEG_INF)
            return s

        for h in range(early, FOX_HEADS):
            score(t, h)
        for h in range(FOX_HEADS):
            p_halves, a_halves = [], []
            for c0 in range(0, tq, LANES):
                m_prev = m_sc[h, :, c0:c0 + LANES]
                m_new = jnp.maximum(m_prev, jnp.max(load(h, c0), axis=0, keepdims=True))
                m_sc[h, :, c0:c0 + LANES] = m_new
                a_halves.append(jnp.exp2(m_prev - m_new))
                p_halves.append(jnp.exp2(load(h, c0) - m_new).astype(BF16))
            p_t = jnp.concatenate(p_halves, axis=1)
            pv = jnp.dot(vt_sc[h, :, pl.ds(ks, tk)], p_t, preferred_element_type=F32)
            acc_sc[h] = jnp.concatenate(a_halves, axis=1) * acc_sc[h] + pv
            if next_t is not None and h < early:
                score(next_t, h)

    early = FOX_HEADS - FOX_LATE_HEADS
    for h in range(early):
        score(0, h)

    def body(t, carry):
        attend(t, False, t + 1)
        return carry

    lax.fori_loop(0, i, body, 0)
    attend(i, True, None)

    outs = []
    for pr in range(npair):
        acc_a = acc_sc[2 * pr]
        acc_b = acc_sc[2 * pr + 1]
        o_t = jnp.concatenate([acc_a[:FOX_DH] / acc_a[FOX_DH:FOX_DH + 1],
                               acc_b[:FOX_DH] / acc_b[FOX_DH:FOX_DH + 1]], axis=0)
        outs.append(o_t.T)
    o_ref[...] = jnp.concatenate(outs, axis=-1).astype(BF16)


def _fox(qkv, cum, bsz, lp):
    tq = SEQ_BLOCK
    nq = lp // tq
    rows = bsz * lp
    return pl.pallas_call(
        _fox_kernel,
        out_shape=jax.ShapeDtypeStruct((rows, FOX_W), BF16),
        grid=(bsz, nq),
        in_specs=[
            pl.BlockSpec((tq, FOX_W), lambda bb, i: (bb * nq + i, 0)),
            pl.BlockSpec((lp, FOX_W), lambda bb, i: (bb, 1)),
            pl.BlockSpec((lp, FOX_W), lambda bb, i: (bb, 2)),
            pl.BlockSpec((lp, LANES), lambda bb, i: (bb, 0)),
        ],
        out_specs=pl.BlockSpec((tq, FOX_W), lambda bb, i: (bb * nq + i, 0)),
        scratch_shapes=[pltpu.VMEM((FOX_HEADS, lp, LANES), BF16),
                        pltpu.VMEM((FOX_HEADS, FOX_VROWS, lp), BF16),
                        pltpu.VMEM((FOX_HEADS, tq, LANES), BF16),
                        pltpu.VMEM((FOX_HEADS, SEQ_BLOCK, tq), F32),
                        pltpu.VMEM((FOX_HEADS, 1, tq), F32),
                        pltpu.VMEM((FOX_HEADS, FOX_VROWS, tq), F32)],
        compiler_params=pltpu.CompilerParams(
            dimension_semantics=("parallel", "arbitrary"),
            vmem_limit_bytes=VMEM_LIMIT),
        name="fox_attention",
    )(qkv, qkv, qkv, cum)


def _mix_ffn_kernel(or_ref, op_ref, of_ref, h_ref, wo_ref, g1_ref, b1_ref,
                    w1_ref, w3_ref, w2_ref, g2_ref, b2_ref, o_ref):
    mix = jnp.concatenate([or_ref[...], op_ref[...], of_ref[...]], axis=-1)
    y = ALPHA * h_ref[...] + jnp.dot(mix, wo_ref[...], preferred_element_type=F32)
    h1 = _layer_norm(y, g1_ref[...], b1_ref[...])
    xb = h1.astype(BF16)
    acc = None
    for c0 in range(0, D_FF, FF_CHUNK):
        a = jnp.dot(xb, w1_ref[:, c0:c0 + FF_CHUNK], preferred_element_type=F32)
        b = jnp.dot(xb, w3_ref[:, c0:c0 + FF_CHUNK], preferred_element_type=F32)
        t = (a * _sigmoid(a) * b).astype(BF16)
        part = jnp.dot(t, w2_ref[c0:c0 + FF_CHUNK, :], preferred_element_type=F32)
        acc = part if acc is None else acc + part
    o_ref[...] = _layer_norm(ALPHA * h1 + acc, g2_ref[...], b2_ref[...])


def _mix_ffn(o_r, o_p, o_f, h, params, layer, tm):
    rows = h.shape[0]

    def row(width):
        return pl.BlockSpec((tm, width), lambda i: (i, 0))

    def resident(shape):
        return pl.BlockSpec((None,) + shape, lambda i: (layer,) + (0,) * len(shape),
                            pipeline_mode=pl.Buffered(1))

    vec = resident((1, D_MODEL))
    return pl.pallas_call(
        _mix_ffn_kernel,
        out_shape=jax.ShapeDtypeStruct((rows, D_MODEL), F32),
        grid=(rows // tm,),
        in_specs=[row(RETV_W), row(POOL_W), row(FOX_W), row(D_MODEL),
                  resident((MIX_W, D_MODEL)), vec, vec,
                  resident((D_MODEL, D_FF)), resident((D_MODEL, D_FF)), resident((D_FF, D_MODEL)), vec, vec],
        out_specs=row(D_MODEL),
        compiler_params=pltpu.CompilerParams(dimension_semantics=("parallel",),
                                             vmem_limit_bytes=VMEM_LIMIT),
        name="mix_ffn_ln",
    )(o_r, o_p, o_f, h, params["w_out"], params["ln1_g"], params["ln1_b"],
      params["w1"], params["w3"], params["w2"], params["ln2_g"], params["ln2_b"])


def _pack_w_in(w):
    d = w.shape[0]

    def rot_pack(seg):
        s = seg.reshape(d, RET_HEADS, RET_DK)
        x1 = s[:, :, :RET_HALF].reshape(d, RET_HEADS * RET_HALF)
        x2 = s[:, :, RET_HALF:].reshape(d, RET_HEADS * RET_HALF)
        z = jnp.zeros((d, LANES - RET_HEADS * RET_HALF), w.dtype)
        return jnp.concatenate([x1, z, x2, z], axis=1)

    def head_pad(seg):
        s = seg.reshape(d, RET_HEADS, RET_DV)
        return jnp.pad(s, ((0, 0), (0, 0), (0, RET_HEAD_PAD - RET_DV))).reshape(d, RETV_W)

    o = 0
    segs = []
    for sz in (RET_QK, RET_QK, RET_W, RET_W, POOL_W, FOX_W, FOX_W, FOX_W, FOX_HEADS):
        segs.append(w[:, o:o + sz])
        o += sz
    q_r, k_r, v_r, g_r, u_p, q_f, k_f, v_f, f_l = segs
    packed = jnp.concatenate(
        [rot_pack(q_r), rot_pack(k_r), head_pad(v_r), head_pad(g_r), u_p, q_f, k_f, v_f,
         jnp.pad(f_l, ((0, 0), (0, LANES - FOX_HEADS)))], axis=1)
    return packed.astype(BF16)


def _pack_w_out(w):
    d = w.shape[1]
    w_r = jnp.pad(w[:RET_W].reshape(RET_HEADS, RET_DV, d),
                  ((0, 0), (0, RET_HEAD_PAD - RET_DV), (0, 0))).reshape(RETV_W, d)
    return jnp.concatenate([w_r, w[RET_W:]], axis=0).astype(BF16)


def _pad_heads(vec):
    return jnp.pad(vec.reshape(RET_HEADS, RET_DV), ((0, 0), (0, RET_HEAD_PAD - RET_DV))).reshape(1, RETV_W)


def _block_diag(pw):
    g = len(POOL_WINDOWS)
    eye = jnp.eye(g, dtype=pw.dtype)
    return jnp.einsum("gij,gh->gihj", pw, eye).reshape(POOL_W, POOL_W).astype(BF16)


def _prepare_params(w_in, b_f, ret_gn_g, pool_w, pool_scale, w_out, ln1_g, ln1_b,
                    w_ffn1, w_ffn3, w_ffn2, ln2_g, ln2_b):
    depth = w_in.shape[0]
    vec = lambda a: a.reshape(depth, 1, a.shape[-1])
    return dict(
        w_in=jax.vmap(_pack_w_in)(w_in),
        gn_g=jax.vmap(_pad_heads)(ret_gn_g),
        b_f=vec(jnp.pad(b_f, ((0, 0), (0, LANES - FOX_HEADS)))),
        pool_w=jax.vmap(_block_diag)(pool_w),
        pool_scale=vec(pool_scale),
        w_out=jax.vmap(_pack_w_out)(w_out),
        ln1_g=vec(ln1_g), ln1_b=vec(ln1_b), ln2_g=vec(ln2_g), ln2_b=vec(ln2_b),
        w1=w_ffn1.astype(BF16), w3=w_ffn3.astype(BF16), w2=w_ffn2.astype(BF16))


def _retention_tables():
    ch = SEQ_BLOCK
    gamma = (1.0 - 2.0 ** (-5.0 - np.arange(RET_HEADS, dtype=np.float32))).astype(np.float32)
    lg = np.log(gamma).astype(np.float32)
    i = np.arange(ch, dtype=np.float32)
    diff = i[:, None] - i[None, :]
    dm = np.where(diff >= 0, np.exp(lg[:, None, None] * np.maximum(diff, 0.0)), 0.0).astype(np.float32)
    xi = np.exp(lg[:, None] * (i + 1.0)).astype(np.float32)
    zeta = np.exp(lg[:, None] * (ch - 1.0 - i)).astype(np.float32)
    lane = np.arange(2 * LANES)
    within = lane % LANES
    lane_head = np.where(within < RET_HEADS * RET_HALF, within // RET_HALF, -1)
    xiq = np.zeros((ch, 2 * LANES), np.float32)
    zk = np.zeros((ch, 2 * LANES), np.float32)
    bm = np.zeros((2 * LANES, RETV_W), np.float32)
    for h in range(RET_HEADS):
        sel = lane_head == h
        xiq[:, sel] = xi[h][:, None]
        zk[:, sel] = zeta[h][:, None]
        bm[sel, h * LANES:(h + 1) * LANES] = 1.0
    dec = np.repeat(np.exp(lg * ch).astype(np.float32), LANES)[None, :]
    return dict(xiq=jnp.asarray(xiq), zk=jnp.asarray(zk), dm=jnp.asarray(dm),
                dec=jnp.asarray(dec), bm=jnp.asarray(bm))


def _rotary_tables(lp):
    pos = jnp.arange(lp, dtype=F32)
    inv_freq = ROPE_BASE ** (-jnp.arange(RET_HALF, dtype=F32) / RET_HALF)
    ang = pos[:, None] * inv_freq[None, :]
    pad = LANES - RET_HEADS * RET_HALF
    cos = jnp.pad(jnp.tile(jnp.cos(ang), (1, RET_HEADS)), ((0, 0), (0, pad)))
    sin = jnp.pad(jnp.tile(jnp.sin(ang), (1, RET_HEADS)), ((0, 0), (0, pad)))
    return cos, sin


def kernel(x, meta, ln_emb_g, ln_emb_b, w_in, b_f, ret_gn_g, pool_w, pool_scale, w_out, ln1_g, ln1_b,
           w_ffn1, w_ffn3, w_ffn2, ln2_g, ln2_b):
    bsz, seq, d = x.shape
    assert d == D_MODEL and seq % SEQ_BLOCK == 0
    depth = w_in.shape[0]
    assert depth == DEPTH
    lp = seq + SEQ_BLOCK
    rows = bsz * lp
    tm = ROW_TILE if rows % ROW_TILE == 0 else SEQ_BLOCK

    consts = _retention_tables()
    consts["cos"], consts["sin"] = _rotary_tables(lp)
    params = _prepare_params(w_in, b_f, ret_gn_g, pool_w, pool_scale, w_out, ln1_g, ln1_b,
                             w_ffn1, w_ffn3, w_ffn2, ln2_g, ln2_b)

    h = _embed(x, meta, ln_emb_g, ln_emb_b, lp).reshape(rows, d)
    for l in range(depth):
        ret, qkv, flog = _inproj(h, params["w_in"], l, tm)
        o_r, o_p, cum = _seq_mix(ret, flog, consts, params, l, bsz, lp)
        o_f = _fox(qkv, cum, bsz, lp)
        h = _mix_ffn(o_r, o_p, o_f, h, params, l, tm)
    return h.reshape(bsz, lp, d)[:, N_META:N_META + seq]
```

```python
import functools

import numpy as np
import jax
import jax.numpy as jnp
from jax import lax
from jax.experimental import pallas as pl
from jax.experimental.pallas import tpu as pltpu

F32 = jnp.float32
BF16 = jnp.bfloat16

D_MODEL = 1024
N_META = 16
RET_HEADS = 4
RET_DK = 48
RET_HALF = RET_DK // 2
RET_DV = 96
RET_QK = RET_HEADS * RET_DK
RET_W = RET_HEADS * RET_DV
POOL_WINDOWS = (2, 4, 8, 16)
POOL_GROUP = 64
POOL_W = len(POOL_WINDOWS) * POOL_GROUP
FOX_HEADS = 6
FOX_DH = 64
FOX_W = FOX_HEADS * FOX_DH
D_FF = 2816
ROPE_BASE = 10000.0
LN_EPS = 1e-5
NEG_INF = -1e30
DEPTH = 2
ALPHA = (2.0 * DEPTH) ** 0.25
LOG2E = 1.4426950408889634

LANES = 128
MXU_DIM = 256
SEQ_BLOCK = 256
ROW_TILE = 512
VMEM_LIMIT = 56 * 1024 * 1024

RET_HEAD_PAD = LANES
QR_OFF = 0
KR_OFF = 2 * LANES
VR_OFF = 4 * LANES
GR_OFF = VR_OFF + RET_HEADS * RET_HEAD_PAD
UP_OFF = GR_OFF + RET_HEADS * RET_HEAD_PAD
QF_OFF = UP_OFF + POOL_W
KF_OFF = QF_OFF + FOX_W
VF_OFF = KF_OFF + FOX_W
FL_OFF = VF_OFF + FOX_W
RET_COLS = QF_OFF
FOX_COLS = 3 * FOX_W
N_PACK = FL_OFF + LANES
RETV_W = RET_HEADS * RET_HEAD_PAD
MIX_W = RETV_W + POOL_W + FOX_W
PROJ_CHUNK = 512
FF_CHUNK = 256


def _layer_norm(x, g, b):
    mu = jnp.mean(x, axis=-1, keepdims=True)
    d = x - mu
    var = jnp.mean(d * d, axis=-1, keepdims=True)
    return d * lax.rsqrt(var + LN_EPS) * g + b


def _sigmoid(x):
    return 1.0 / (1.0 + jnp.exp(-x))


def _embed_kernel(meta_ref, xm_ref, xe_ref, g_ref, b_ref, o_ref):
    j = pl.program_id(1)
    last = pl.num_programs(1) - 1
    top = jnp.where(j == 0, meta_ref[...], xe_ref[...])
    body = jnp.where(j == last, 0.0, xm_ref[0:SEQ_BLOCK - N_META, :])
    rows = jnp.concatenate([top, body], axis=0)
    o_ref[...] = _layer_norm(rows, g_ref[...], b_ref[...])


def _embed(x, meta, g, b, lp):
    bsz, seq, d = x.shape
    nblk = lp // SEQ_BLOCK
    n_xblk = seq // SEQ_BLOCK
    per = SEQ_BLOCK // N_META
    return pl.pallas_call(
        _embed_kernel,
        out_shape=jax.ShapeDtypeStruct((bsz, lp, d), F32),
        grid=(bsz, nblk),
        in_specs=[
            pl.BlockSpec((N_META, d), lambda bb, j: (0, 0)),
            pl.BlockSpec((None, SEQ_BLOCK, d), lambda bb, j: (bb, jnp.minimum(j, n_xblk - 1), 0)),
            pl.BlockSpec((None, N_META, d), lambda bb, j: (bb, jnp.maximum(per * j - 1, 0), 0)),
            pl.BlockSpec((1, d), lambda bb, j: (0, 0)),
            pl.BlockSpec((1, d), lambda bb, j: (0, 0)),
        ],
        out_specs=pl.BlockSpec((None, SEQ_BLOCK, d), lambda bb, j: (bb, j, 0)),
        compiler_params=pltpu.CompilerParams(dimension_semantics=("parallel", "arbitrary")),
        name="embed_ln",
    )(meta, x, x, g.reshape(1, d), b.reshape(1, d))


def _inproj_kernel(h_ref, w_ref, cs_ref, ret_ref, fox_ref, flog_ref):
    xb = h_ref[...].astype(BF16)

    def cols(c0, width):
        r = jnp.dot(xb, w_ref[:, c0:c0 + width], preferred_element_type=F32)
        return r * cs_ref[:, c0:c0 + width]

    for c0 in range(0, RET_COLS, PROJ_CHUNK):
        width = min(PROJ_CHUNK, RET_COLS - c0)
        ret_ref[:, c0:c0 + width] = cols(c0, width).astype(BF16)
    for c0 in range(0, FOX_COLS - LANES, PROJ_CHUNK):
        fox_ref[:, c0:c0 + PROJ_CHUNK] = cols(QF_OFF + c0, PROJ_CHUNK).astype(BF16)
    tail = cols(FL_OFF - LANES, 2 * LANES)
    fox_ref[:, FOX_COLS - LANES:] = tail[:, :LANES].astype(BF16)
    flog_ref[...] = tail[:, LANES:]


def _inproj_col_scale():
    cs = np.ones((1, N_PACK), np.float32)
    cs[0, KR_OFF:KR_OFF + 2 * LANES] = RET_DK ** -0.5
    cs[0, QF_OFF:QF_OFF + FOX_W] = FOX_DH ** -0.5 * LOG2E
    return jnp.asarray(cs)


def _inproj(h, w, layer, tm):
    rows = h.shape[0]
    return pl.pallas_call(
        _inproj_kernel,
        out_shape=(jax.ShapeDtypeStruct((rows, RET_COLS), BF16),
                   jax.ShapeDtypeStruct((rows, FOX_COLS), BF16),
                   jax.ShapeDtypeStruct((rows, LANES), F32)),
        grid=(rows // tm,),
        in_specs=[
            pl.BlockSpec((tm, D_MODEL), lambda i: (i, 0)),
            pl.BlockSpec((None, D_MODEL, N_PACK), lambda i: (layer, 0, 0)),
            pl.BlockSpec((1, N_PACK), lambda i: (0, 0)),
        ],
        out_specs=(pl.BlockSpec((tm, RET_COLS), lambda i: (i, 0)),
                   pl.BlockSpec((tm, FOX_COLS), lambda i: (i, 0)),
                   pl.BlockSpec((tm, LANES), lambda i: (i, 0))),
        compiler_params=pltpu.CompilerParams(dimension_semantics=("parallel",),
                                             vmem_limit_bytes=VMEM_LIMIT),
        name="in_proj",
    )(h, w, _inproj_col_scale())


def _seq_kernel(qk_ref, v_ref, g_ref, u_ref, fl_ref, cos_ref, sin_ref,
                xiq_ref, zk_ref, dm_ref, dec_ref, bm_ref, gng_ref, bf_ref, pw_ref, ps_ref,
                or_ref, op_ref, c_ref,
                state_sc, tail_sc, carry_sc):
    j = pl.program_id(1)
    ch = SEQ_BLOCK

    @pl.when(j == 0)
    def _():
        state_sc[...] = jnp.zeros_like(state_sc)
        tail_sc[...] = jnp.zeros_like(tail_sc)
        carry_sc[...] = jnp.zeros_like(carry_sc)

    qk = qk_ref[...].astype(F32)
    cs = cos_ref[...]
    sn = sin_ref[...]
    q1, q2 = qk[:, 0:LANES], qk[:, LANES:2 * LANES]
    k1, k2 = qk[:, 2 * LANES:3 * LANES], qk[:, 3 * LANES:4 * LANES]
    qr = jnp.concatenate([q1 * cs - q2 * sn, q1 * sn + q2 * cs], axis=-1)
    kr = jnp.concatenate([k1 * cs - k2 * sn, k1 * sn + k2 * cs], axis=-1)
    qb = qr.astype(BF16)
    qx = (qr * xiq_ref[...]).astype(BF16)
    kb = kr.astype(BF16)
    kz = (kr * zk_ref[...]).astype(BF16)
    v = v_ref[...]
    st = state_sc[...]
    cross = jnp.dot(qx, st.astype(BF16), preferred_element_type=F32)
    qlane = lax.broadcasted_iota(jnp.int32, (1, 2 * LANES), 1)
    qhead = jnp.where(qlane % LANES < RET_HEADS * RET_HALF, (qlane % LANES) // RET_HALF, RET_HEADS)
    inner = []
    for h in range(RET_HEADS):
        qh = jnp.where(qhead == h, qb, jnp.zeros_like(qb))
        s = lax.dot_general(qh, kb, (((1,), (1,)), ((), ())), preferred_element_type=F32)
        p = (s * dm_ref[h]).astype(BF16)
        inner.append(jnp.dot(p, v[:, h * LANES:(h + 1) * LANES], preferred_element_type=F32))
    o = jnp.concatenate(inner, axis=-1) + cross
    kv = lax.dot_general(kz, v, (((0,), (0,)), ((), ())), preferred_element_type=F32)
    state_sc[...] = st * dec_ref[...] + kv * bm_ref[...]

    vlane = lax.broadcasted_iota(jnp.int32, (1, LANES), 1) < RET_DV
    normed = []
    for h in range(RET_HEADS):
        xh = o[:, h * LANES:(h + 1) * LANES]
        mu = jnp.sum(xh, axis=-1, keepdims=True) * (1.0 / RET_DV)
        d = jnp.where(vlane, xh - mu, 0.0)
        var = jnp.sum(d * d, axis=-1, keepdims=True) * (1.0 / RET_DV)
        normed.append(d * lax.rsqrt(var + LN_EPS))
    y = jnp.concatenate(normed, axis=-1) * gng_ref[...]
    gate = g_ref[...].astype(F32)
    or_ref[...] = (gate * _sigmoid(gate) * y).astype(BF16)

    u = u_ref[...].astype(F32)
    tail_rows = tail_sc.shape[0]
    ext = jnp.concatenate([tail_sc[...], u], axis=0)
    tail_sc[...] = u[ch - tail_rows:, :]
    e2 = ext + pltpu.roll(ext, 1, 0)
    e4 = e2 + pltpu.roll(e2, 2, 0)
    e8 = e4 + pltpu.roll(e4, 4, 0)
    e16 = e8 + pltpu.roll(e8, 8, 0)
    glane = lax.broadcasted_iota(jnp.int32, (1, POOL_W), 1) // POOL_GROUP
    win = jnp.where(glane == 0, e2, jnp.where(glane == 1, e4, jnp.where(glane == 2, e8, e16)))
    win = win[tail_rows:, :]
    wlen = jnp.where(glane == 0, 2, jnp.where(glane == 1, 4, jnp.where(glane == 2, 8, 16)))
    pos = j * ch + lax.broadcasted_iota(jnp.int32, (ch, POOL_W), 0)
    cnt = jnp.minimum(pos + 1, wlen).astype(F32)
    pooled = (win / cnt - u).astype(BF16)
    yp = jnp.dot(pooled, pw_ref[...], preferred_element_type=F32) * ps_ref[...]
    op_ref[...] = yp.astype(BF16)

    z = fl_ref[...] + bf_ref[...]
    logf = jnp.minimum(z, 0.0) - jnp.log1p(jnp.exp(-jnp.abs(z)))
    row = lax.broadcasted_iota(jnp.int32, (ch, LANES), 0)
    sh = 1
    while sh < ch:
        logf = logf + jnp.where(row >= sh, pltpu.roll(logf, sh, 0), 0.0)
        sh *= 2
    c = logf + carry_sc[0:1, :]
    carry_sc[...] = jnp.broadcast_to(c[ch - 1:ch, :], carry_sc.shape)
    c_ref[...] = c * LOG2E


def _seq_mix(proj, flog, consts, params, layer, bsz, lp):
    ch = SEQ_BLOCK
    nc = lp // ch
    rows = bsz * lp

    def rowblk(width, colblk):
        return pl.BlockSpec((ch, width), lambda bb, j: (bb * nc + j, colblk))

    def const(shape):
        nd = len(shape)
        return pl.BlockSpec(shape, lambda bb, j: (0,) * nd)

    def layer_param(shape):
        return pl.BlockSpec((None,) + shape, lambda bb, j: (layer,) + (0,) * len(shape))

    in_specs = [
        rowblk(4 * LANES, QR_OFF // (4 * LANES)),
        rowblk(RETV_W, VR_OFF // RETV_W),
        rowblk(RETV_W, GR_OFF // RETV_W),
        rowblk(POOL_W, UP_OFF // POOL_W),
        pl.BlockSpec((ch, LANES), lambda bb, j: (bb * nc + j, 0)),
        pl.BlockSpec((ch, LANES), lambda bb, j: (j, 0)),
        pl.BlockSpec((ch, LANES), lambda bb, j: (j, 0)),
        const((ch, 2 * LANES)), const((ch, 2 * LANES)), const((RET_HEADS, ch, ch)),
        const((1, RETV_W)), const((2 * LANES, RETV_W)),
        layer_param((1, RETV_W)), layer_param((1, LANES)), layer_param((POOL_W, POOL_W)),
        layer_param((1, POOL_W)),
    ]
    out_shape = (jax.ShapeDtypeStruct((rows, RETV_W), BF16),
                 jax.ShapeDtypeStruct((rows, POOL_W), BF16),
                 jax.ShapeDtypeStruct((rows, LANES), F32))
    out_specs = (pl.BlockSpec((ch, RETV_W), lambda bb, j: (bb * nc + j, 0)),
                 pl.BlockSpec((ch, POOL_W), lambda bb, j: (bb * nc + j, 0)),
                 pl.BlockSpec((ch, LANES), lambda bb, j: (bb * nc + j, 0)))
    return pl.pallas_call(
        _seq_kernel,
        out_shape=out_shape,
        grid=(bsz, nc),
        in_specs=in_specs,
        out_specs=out_specs,
        scratch_shapes=[pltpu.VMEM((2 * LANES, RETV_W), F32),
                        pltpu.VMEM((16, POOL_W), F32),
                        pltpu.VMEM((8, LANES), F32)],
        compiler_params=pltpu.CompilerParams(dimension_semantics=("parallel", "arbitrary"),
                                             vmem_limit_bytes=VMEM_LIMIT),
        name="seq_mixers",
    )(proj, proj, proj, proj, flog, consts["cos"], consts["sin"],
      consts["xiq"], consts["zk"], consts["dm"], consts["dec"], consts["bm"],
      params["gn_g"], params["b_f"], params["pool_w"], params["pool_scale"])


FOX_AUG = 3
FOX_VROWS = FOX_DH + 16
FOX_LATE_HEADS = 2


def _fox_select_matrix():
    sel = np.zeros((LANES, FOX_HEADS * LANES), np.float32)
    for a in range(FOX_AUG):
        for h in range(FOX_HEADS):
            sel[8 * a + h, h * LANES + FOX_DH + a] = -1.0
    return jnp.asarray(sel, BF16)


def _fox_kernel(q_ref, k_ref, v_ref, c_ref, sel_ref, o_ref,
                ka_sc, vt_sc, qa_sc, st_sc, m_sc, acc_sc):
    i = pl.program_id(1)
    tq = SEQ_BLOCK
    tk = SEQ_BLOCK
    npair = FOX_HEADS // 2
    lane = lax.broadcasted_iota(jnp.int32, (1, LANES), 1)

    @pl.when(i == 0)
    def _():
        def chunk(t, carry):
            r0 = pl.multiple_of(t * tk, tk)
            cc = c_ref[pl.ds(r0, tk), :]
            pieces, rem = [], cc
            for a in range(FOX_AUG):
                piece = rem.astype(BF16).astype(F32)
                pieces.append(piece if a == 0 else pltpu.roll(piece, 8 * a, 1))
                rem = rem - piece
            packed = jnp.where(lane < 8, pieces[0], jnp.where(lane < 16, pieces[1], pieces[2]))
            c_aug = jnp.dot(packed.astype(BF16), sel_ref[...], preferred_element_type=F32)
            for pr in range(npair):
                kk = k_ref[pl.ds(r0, tk), pr * LANES:(pr + 1) * LANES].astype(F32)
                vv = v_ref[pl.ds(r0, tk), pr * LANES:(pr + 1) * LANES].astype(F32)
                vtt = vv.T
                ones = jnp.ones((FOX_VROWS - FOX_DH, tk), F32)
                k_heads = (kk, pltpu.roll(kk, FOX_DH, 1))
                v_heads = (vtt[:FOX_DH], vtt[FOX_DH:])
                for hh in range(2):
                    h = 2 * pr + hh
                    aug = jnp.where(lane < FOX_DH, k_heads[hh], c_aug[:, h * LANES:(h + 1) * LANES])
                    ka_sc[h, pl.ds(r0, tk), :] = aug.astype(BF16)
                    vt_sc[h, :, pl.ds(r0, tk)] = jnp.concatenate([v_heads[hh], ones], axis=0).astype(BF16)
            return carry

        lax.fori_loop(0, k_ref.shape[0] // tk, chunk, 0)

    for pr in range(npair):
        qq = q_ref[:, pr * LANES:(pr + 1) * LANES].astype(F32)
        tail = jnp.where(lane < FOX_DH + FOX_AUG, 1.0, 0.0)
        qa_sc[2 * pr] = jnp.where(lane < FOX_DH, qq, tail).astype(BF16)
        qa_sc[2 * pr + 1] = jnp.where(lane < FOX_DH, pltpu.roll(qq, FOX_DH, 1), tail).astype(BF16)
    m_sc[...] = jnp.full(m_sc.shape, NEG_INF, F32)
    acc_sc[...] = jnp.zeros_like(acc_sc)

    def score(t, h):
        ks = t * tk if isinstance(t, int) else pl.multiple_of(t * tk, tk)
        st_sc[h] = lax.dot_general(ka_sc[h, pl.ds(ks, tk), :], qa_sc[h], (((1,), (1,)), ((), ())),
                                   preferred_element_type=F32)

    def attend(t, masked, next_t):
        ks = t * tk if isinstance(t, int) else pl.multiple_of(t * tk, tk)

        def load(h, c0):
            s = st_sc[h, :, c0:c0 + LANES]
            if masked:
                k_id = lax.broadcasted_iota(jnp.int32, (tk, LANES), 0)
                q_id = c0 + lax.broadcasted_iota(jnp.int32, (tk, LANES), 1)
                s = jnp.where(k_id <= q_id, s, NEG_INF)
            return s

        for h in range(early, FOX_HEADS):
            score(t, h)
        for h in range(FOX_HEADS):
            p_halves, a_halves = [], []
            for c0 in range(0, tq, LANES):
                m_prev = m_sc[h, :, c0:c0 + LANES]
                m_new = jnp.maximum(m_prev, jnp.max(load(h, c0), axis=0, keepdims=True))
                m_sc[h, :, c0:c0 + LANES] = m_new
                a_halves.append(jnp.exp2(m_prev - m_new))
                p_halves.append(jnp.exp2(load(h, c0) - m_new).astype(BF16))
            p_t = jnp.concatenate(p_halves, axis=1)
            pv = jnp.dot(vt_sc[h, :, pl.ds(ks, tk)], p_t, preferred_element_type=F32)
            acc_sc[h] = jnp.concatenate(a_halves, axis=1) * acc_sc[h] + pv
            if next_t is not None and h < early:
                score(next_t, h)

    early = FOX_HEADS - FOX_LATE_HEADS
    for h in range(early):
        score(0, h)

    def body(u, carry):
        t = 2 * u
        attend(t, False, t + 1)
        attend(t + 1, False, t + 2)
        return carry

    lax.fori_loop(0, i // 2, body, 0)

    @pl.when(i % 2 == 1)
    def _():
        attend(i - 1, False, i)

    attend(i, True, None)

    outs = []
    for pr in range(npair):
        acc_a = acc_sc[2 * pr]
        acc_b = acc_sc[2 * pr + 1]
        o_t = jnp.concatenate([acc_a[:FOX_DH] / acc_a[FOX_DH:FOX_DH + 1],
                               acc_b[:FOX_DH] / acc_b[FOX_DH:FOX_DH + 1]], axis=0)
        outs.append(o_t.T)
    o_ref[...] = jnp.concatenate(outs, axis=-1).astype(BF16)


def _fox(qkv, cum, bsz, lp):
    tq = SEQ_BLOCK
    nq = lp // tq
    rows = bsz * lp
    return pl.pallas_call(
        _fox_kernel,
        out_shape=jax.ShapeDtypeStruct((rows, FOX_W), BF16),
        grid=(bsz, nq),
        in_specs=[
            pl.BlockSpec((tq, FOX_W), lambda bb, i: (bb * nq + i, 0)),
            pl.BlockSpec((lp, FOX_W), lambda bb, i: (bb, 1)),
            pl.BlockSpec((lp, FOX_W), lambda bb, i: (bb, 2)),
            pl.BlockSpec((lp, LANES), lambda bb, i: (bb, 0)),
            pl.BlockSpec((LANES, FOX_HEADS * LANES), lambda bb, i: (0, 0)),
        ],
        out_specs=pl.BlockSpec((tq, FOX_W), lambda bb, i: (bb * nq + i, 0)),
        scratch_shapes=[pltpu.VMEM((FOX_HEADS, lp, LANES), BF16),
                        pltpu.VMEM((FOX_HEADS, FOX_VROWS, lp), BF16),
                        pltpu.VMEM((FOX_HEADS, tq, LANES), BF16),
                        pltpu.VMEM((FOX_HEADS, SEQ_BLOCK, tq), F32),
                        pltpu.VMEM((FOX_HEADS, 1, tq), F32),
                        pltpu.VMEM((FOX_HEADS, FOX_VROWS, tq), F32)],
        compiler_params=pltpu.CompilerParams(
            dimension_semantics=("parallel", "arbitrary"),
            vmem_limit_bytes=VMEM_LIMIT),
        name="fox_attention",
    )(qkv, qkv, qkv, cum, _fox_select_matrix())


def _mix_ffn_kernel(or_ref, op_ref, of_ref, h_ref, wo_ref, g1_ref, b1_ref,
                    w1_ref, w3_ref, w2_ref, g2_ref, b2_ref, o_ref):
    mix = jnp.concatenate([or_ref[...], op_ref[...], of_ref[...]], axis=-1)
    y = ALPHA * h_ref[...] + jnp.dot(mix, wo_ref[...], preferred_element_type=F32)
    h1 = _layer_norm(y, g1_ref[...], b1_ref[...])
    xb = h1.astype(BF16)
    acc = None
    for c0 in range(0, D_FF, FF_CHUNK):
        a = jnp.dot(xb, w1_ref[:, c0:c0 + FF_CHUNK], preferred_element_type=F32)
        b = jnp.dot(xb, w3_ref[:, c0:c0 + FF_CHUNK], preferred_element_type=F32)
        t = (a * _sigmoid(a) * b).astype(BF16)
        part = jnp.dot(t, w2_ref[c0:c0 + FF_CHUNK, :], preferred_element_type=F32)
        acc = part if acc is None else acc + part
    o_ref[...] = _layer_norm(ALPHA * h1 + acc, g2_ref[...], b2_ref[...])


def _mix_ffn(o_r, o_p, o_f, h, params, layer, tm):
    rows = h.shape[0]

    def row(width):
        return pl.BlockSpec((tm, width), lambda i: (i, 0))

    def resident(shape):
        return pl.BlockSpec((None,) + shape, lambda i: (layer,) + (0,) * len(shape),
                            pipeline_mode=pl.Buffered(1))

    vec = resident((1, D_MODEL))
    return pl.pallas_call(
        _mix_ffn_kernel,
        out_shape=jax.ShapeDtypeStruct((rows, D_MODEL), F32),
        grid=(rows // tm,),
        in_specs=[row(RETV_W), row(POOL_W), row(FOX_W), row(D_MODEL),
                  resident((MIX_W, D_MODEL)), vec, vec,
                  resident((D_MODEL, D_FF)), resident((D_MODEL, D_FF)), resident((D_FF, D_MODEL)), vec, vec],
        out_specs=row(D_MODEL),
        compiler_params=pltpu.CompilerParams(dimension_semantics=("parallel",),
                                             vmem_limit_bytes=VMEM_LIMIT),
        name="mix_ffn_ln",
    )(o_r, o_p, o_f, h, params["w_out"], params["ln1_g"], params["ln1_b"],
      params["w1"], params["w3"], params["w2"], params["ln2_g"], params["ln2_b"])


def _pack_w_in(w):
    d = w.shape[0]

    def rot_pack(seg):
        s = seg.reshape(d, RET_HEADS, RET_DK)
        x1 = s[:, :, :RET_HALF].reshape(d, RET_HEADS * RET_HALF)
        x2 = s[:, :, RET_HALF:].reshape(d, RET_HEADS * RET_HALF)
        z = jnp.zeros((d, LANES - RET_HEADS * RET_HALF), w.dtype)
        return jnp.concatenate([x1, z, x2, z], axis=1)

    def head_pad(seg):
        s = seg.reshape(d, RET_HEADS, RET_DV)
        return jnp.pad(s, ((0, 0), (0, 0), (0, RET_HEAD_PAD - RET_DV))).reshape(d, RETV_W)

    o = 0
    segs = []
    for sz in (RET_QK, RET_QK, RET_W, RET_W, POOL_W, FOX_W, FOX_W, FOX_W, FOX_HEADS):
        segs.append(w[:, o:o + sz])
        o += sz
    q_r, k_r, v_r, g_r, u_p, q_f, k_f, v_f, f_l = segs
    packed = jnp.concatenate(
        [rot_pack(q_r), rot_pack(k_r), head_pad(v_r), head_pad(g_r), u_p, q_f, k_f, v_f,
         jnp.pad(f_l, ((0, 0), (0, LANES - FOX_HEADS)))], axis=1)
    return packed.astype(BF16)


def _pack_w_out(w):
    d = w.shape[1]
    w_r = jnp.pad(w[:RET_W].reshape(RET_HEADS, RET_DV, d),
                  ((0, 0), (0, RET_HEAD_PAD - RET_DV), (0, 0))).reshape(RETV_W, d)
    return jnp.concatenate([w_r, w[RET_W:]], axis=0).astype(BF16)


def _pad_heads(vec):
    return jnp.pad(vec.reshape(RET_HEADS, RET_DV), ((0, 0), (0, RET_HEAD_PAD - RET_DV))).reshape(1, RETV_W)


def _block_diag(pw):
    g = len(POOL_WINDOWS)
    eye = jnp.eye(g, dtype=pw.dtype)
    return jnp.einsum("gij,gh->gihj", pw, eye).reshape(POOL_W, POOL_W).astype(BF16)


def _prepare_params(w_in, b_f, ret_gn_g, pool_w, pool_scale, w_out, ln1_g, ln1_b,
                    w_ffn1, w_ffn3, w_ffn2, ln2_g, ln2_b):
    depth = w_in.shape[0]
    vec = lambda a: a.reshape(depth, 1, a.shape[-1])
    return dict(
        w_in=jax.vmap(_pack_w_in)(w_in),
        gn_g=jax.vmap(_pad_heads)(ret_gn_g),
        b_f=vec(jnp.pad(b_f, ((0, 0), (0, LANES - FOX_HEADS)))),
        pool_w=jax.vmap(_block_diag)(pool_w),
        pool_scale=vec(pool_scale),
        w_out=jax.vmap(_pack_w_out)(w_out),
        ln1_g=vec(ln1_g), ln1_b=vec(ln1_b), ln2_g=vec(ln2_g), ln2_b=vec(ln2_b),
        w1=w_ffn1.astype(BF16), w3=w_ffn3.astype(BF16), w2=w_ffn2.astype(BF16))


def _retention_tables():
    ch = SEQ_BLOCK
    gamma = (1.0 - 2.0 ** (-5.0 - np.arange(RET_HEADS, dtype=np.float32))).astype(np.float32)
    lg = np.log(gamma).astype(np.float32)
    i = np.arange(ch, dtype=np.float32)
    diff = i[:, None] - i[None, :]
    dm = np.where(diff >= 0, np.exp(lg[:, None, None] * np.maximum(diff, 0.0)), 0.0).astype(np.float32)
    xi = np.exp(lg[:, None] * (i + 1.0)).astype(np.float32)
    zeta = np.exp(lg[:, None] * (ch - 1.0 - i)).astype(np.float32)
    lane = np.arange(2 * LANES)
    within = lane % LANES
    lane_head = np.where(within < RET_HEADS * RET_HALF, within // RET_HALF, -1)
    xiq = np.zeros((ch, 2 * LANES), np.float32)
    zk = np.zeros((ch, 2 * LANES), np.float32)
    bm = np.zeros((2 * LANES, RETV_W), np.float32)
    for h in range(RET_HEADS):
        sel = lane_head == h
        xiq[:, sel] = xi[h][:, None]
        zk[:, sel] = zeta[h][:, None]
        bm[sel, h * LANES:(h + 1) * LANES] = 1.0
    dec = np.repeat(np.exp(lg * ch).astype(np.float32), LANES)[None, :]
    return dict(xiq=jnp.asarray(xiq), zk=jnp.asarray(zk), dm=jnp.asarray(dm),
                dec=jnp.asarray(dec), bm=jnp.asarray(bm))


def _rotary_tables(lp):
    pos = jnp.arange(lp, dtype=F32)
    inv_freq = ROPE_BASE ** (-jnp.arange(RET_HALF, dtype=F32) / RET_HALF)
    ang = pos[:, None] * inv_freq[None, :]
    pad = LANES - RET_HEADS * RET_HALF
    cos = jnp.pad(jnp.tile(jnp.cos(ang), (1, RET_HEADS)), ((0, 0), (0, pad)))
    sin = jnp.pad(jnp.tile(jnp.sin(ang), (1, RET_HEADS)), ((0, 0), (0, pad)))
    return cos, sin


def kernel(x, meta, ln_emb_g, ln_emb_b, w_in, b_f, ret_gn_g, pool_w, pool_scale, w_out, ln1_g, ln1_b,
           w_ffn1, w_ffn3, w_ffn2, ln2_g, ln2_b):
    bsz, seq, d = x.shape
    assert d == D_MODEL and seq % SEQ_BLOCK == 0
    depth = w_in.shape[0]
    assert depth == DEPTH
    lp = seq + SEQ_BLOCK
    rows = bsz * lp
    tm = ROW_TILE if rows % ROW_TILE == 0 else SEQ_BLOCK

    consts = _retention_tables()
    consts["cos"], consts["sin"] = _rotary_tables(lp)
    params = _prepare_params(w_in, b_f, ret_gn_g, pool_w, pool_scale, w_out, ln1_g, ln1_b,
                             w_ffn1, w_ffn3, w_ffn2, ln2_g, ln2_b)

    h = _embed(x, meta, ln_emb_g, ln_emb_b, lp).reshape(rows, d)
    for l in range(depth):
        ret, qkv, flog = _inproj(h, params["w_in"], l, tm)
        o_r, o_p, cum = _seq_mix(ret, flog, consts, params, l, bsz, lp)
        o_f = _fox(qkv, cum, bsz, lp)
        h = _mix_ffn(o_r, o_p, o_f, h, params, l, tm)
    return h.reshape(bsz, lp, d)[:, N_META:N_META + seq]
```

```python
import functools

import numpy as np
import jax
import jax.numpy as jnp
from jax import lax
from jax.experimental import pallas as pl
from jax.experimental.pallas import tpu as pltpu

F32 = jnp.float32
BF16 = jnp.bfloat16

D_MODEL = 1024
N_META = 16
RET_HEADS = 4
RET_DK = 48
RET_HALF = RET_DK // 2
RET_DV = 96
RET_QK = RET_HEADS * RET_DK
RET_W = RET_HEADS * RET_DV
POOL_WINDOWS = (2, 4, 8, 16)
POOL_GROUP = 64
POOL_W = len(POOL_WINDOWS) * POOL_GROUP
FOX_HEADS = 6
FOX_DH = 64
FOX_W = FOX_HEADS * FOX_DH
D_FF = 2816
ROPE_BASE = 10000.0
LN_EPS = 1e-5
NEG_INF = -1e30
DEPTH = 2
ALPHA = (2.0 * DEPTH) ** 0.25
LOG2E = 1.4426950408889634

LANES = 128
MXU_DIM = 256
SEQ_BLOCK = 256
ROW_TILE = 512
VMEM_LIMIT = 56 * 1024 * 1024

RET_HEAD_PAD = LANES
QR_OFF = 0
KR_OFF = 2 * LANES
VR_OFF = 4 * LANES
GR_OFF = VR_OFF + RET_HEADS * RET_HEAD_PAD
UP_OFF = GR_OFF + RET_HEADS * RET_HEAD_PAD
QF_OFF = UP_OFF + POOL_W
KF_OFF = QF_OFF + FOX_W
VF_OFF = KF_OFF + FOX_W
FL_OFF = VF_OFF + FOX_W
RET_COLS = QF_OFF
FOX_COLS = 3 * FOX_W
N_PACK = FL_OFF + LANES
RETV_W = RET_HEADS * RET_HEAD_PAD
MIX_W = RETV_W + POOL_W + FOX_W
PROJ_CHUNK = 512
FF_CHUNK = 256


def _layer_norm(x, g, b):
    mu = jnp.mean(x, axis=-1, keepdims=True)
    d = x - mu
    var = jnp.mean(d * d, axis=-1, keepdims=True)
    return d * lax.rsqrt(var + LN_EPS) * g + b


def _sigmoid(x):
    return 1.0 / (1.0 + jnp.exp(-x))


def _embed_kernel(meta_ref, xm_ref, xe_ref, g_ref, b_ref, o_ref):
    j = pl.program_id(1)
    last = pl.num_programs(1) - 1
    top = jnp.where(j == 0, meta_ref[...], xe_ref[...])
    body = jnp.where(j == last, 0.0, xm_ref[0:SEQ_BLOCK - N_META, :])
    rows = jnp.concatenate([top, body], axis=0)
    o_ref[...] = _layer_norm(rows, g_ref[...], b_ref[...])


def _embed(x, meta, g, b, lp):
    bsz, seq, d = x.shape
    nblk = lp // SEQ_BLOCK
    n_xblk = seq // SEQ_BLOCK
    per = SEQ_BLOCK // N_META
    return pl.pallas_call(
        _embed_kernel,
        out_shape=jax.ShapeDtypeStruct((bsz, lp, d), F32),
        grid=(bsz, nblk),
        in_specs=[
            pl.BlockSpec((N_META, d), lambda bb, j: (0, 0)),
            pl.BlockSpec((None, SEQ_BLOCK, d), lambda bb, j: (bb, jnp.minimum(j, n_xblk - 1), 0)),
            pl.BlockSpec((None, N_META, d), lambda bb, j: (bb, jnp.maximum(per * j - 1, 0), 0)),
            pl.BlockSpec((1, d), lambda bb, j: (0, 0)),
            pl.BlockSpec((1, d), lambda bb, j: (0, 0)),
        ],
        out_specs=pl.BlockSpec((None, SEQ_BLOCK, d), lambda bb, j: (bb, j, 0)),
        compiler_params=pltpu.CompilerParams(dimension_semantics=("parallel", "arbitrary")),
        name="embed_ln",
    )(meta, x, x, g.reshape(1, d), b.reshape(1, d))


def _inproj_kernel(h_ref, w_ref, cs_ref, ret_ref, fox_ref, flog_ref):
    xb = h_ref[...].astype(BF16)

    def cols(c0, width):
        r = jnp.dot(xb, w_ref[:, c0:c0 + width], preferred_element_type=F32)
        return r * cs_ref[:, c0:c0 + width]

    for c0 in range(0, RET_COLS, PROJ_CHUNK):
        width = min(PROJ_CHUNK, RET_COLS - c0)
        ret_ref[:, c0:c0 + width] = cols(c0, width).astype(BF16)
    for c0 in range(0, FOX_COLS - LANES, PROJ_CHUNK):
        fox_ref[:, c0:c0 + PROJ_CHUNK] = cols(QF_OFF + c0, PROJ_CHUNK).astype(BF16)
    tail = cols(FL_OFF - LANES, 2 * LANES)
    fox_ref[:, FOX_COLS - LANES:] = tail[:, :LANES].astype(BF16)
    flog_ref[...] = tail[:, LANES:]


def _inproj_col_scale():
    cs = np.ones((1, N_PACK), np.float32)
    cs[0, KR_OFF:KR_OFF + 2 * LANES] = RET_DK ** -0.5
    cs[0, QF_OFF:QF_OFF + FOX_W] = FOX_DH ** -0.5 * LOG2E
    return jnp.asarray(cs)


def _inproj(h, w, layer, tm):
    rows = h.shape[0]
    return pl.pallas_call(
        _inproj_kernel,
        out_shape=(jax.ShapeDtypeStruct((rows, RET_COLS), BF16),
                   jax.ShapeDtypeStruct((rows, FOX_COLS), BF16),
                   jax.ShapeDtypeStruct((rows, LANES), F32)),
        grid=(rows // tm,),
        in_specs=[
            pl.BlockSpec((tm, D_MODEL), lambda i: (i, 0)),
            pl.BlockSpec((None, D_MODEL, N_PACK), lambda i: (layer, 0, 0)),
            pl.BlockSpec((1, N_PACK), lambda i: (0, 0)),
        ],
        out_specs=(pl.BlockSpec((tm, RET_COLS), lambda i: (i, 0)),
                   pl.BlockSpec((tm, FOX_COLS), lambda i: (i, 0)),
                   pl.BlockSpec((tm, LANES), lambda i: (i, 0))),
        compiler_params=pltpu.CompilerParams(dimension_semantics=("parallel",),
                                             vmem_limit_bytes=VMEM_LIMIT),
        name="in_proj",
    )(h, w, _inproj_col_scale())


def _seq_kernel(qk_ref, v_ref, g_ref, u_ref, fl_ref, cos_ref, sin_ref,
                xiq_ref, zk_ref, dm_ref, dec_ref, bm_ref, gng_ref, bf_ref, pw_ref, ps_ref,
                or_ref, op_ref, c_ref,
                state_sc, tail_sc, carry_sc):
    j = pl.program_id(1)
    ch = SEQ_BLOCK

    @pl.when(j == 0)
    def _():
        state_sc[...] = jnp.zeros_like(state_sc)
        tail_sc[...] = jnp.zeros_like(tail_sc)
        carry_sc[...] = jnp.zeros_like(carry_sc)

    qk = qk_ref[...].astype(F32)
    cs = cos_ref[...]
    sn = sin_ref[...]
    q1, q2 = qk[:, 0:LANES], qk[:, LANES:2 * LANES]
    k1, k2 = qk[:, 2 * LANES:3 * LANES], qk[:, 3 * LANES:4 * LANES]
    qr = jnp.concatenate([q1 * cs - q2 * sn, q1 * sn + q2 * cs], axis=-1)
    kr = jnp.concatenate([k1 * cs - k2 * sn, k1 * sn + k2 * cs], axis=-1)
    qb = qr.astype(BF16)
    qx = (qr * xiq_ref[...]).astype(BF16)
    kb = kr.astype(BF16)
    kz = (kr * zk_ref[...]).astype(BF16)
    v = v_ref[...]
    st = state_sc[...]
    cross = jnp.dot(qx, st.astype(BF16), preferred_element_type=F32)
    qlane = lax.broadcasted_iota(jnp.int32, (1, 2 * LANES), 1)
    qhead = jnp.where(qlane % LANES < RET_HEADS * RET_HALF, (qlane % LANES) // RET_HALF, RET_HEADS)
    inner = []
    for h in range(RET_HEADS):
        qh = jnp.where(qhead == h, qb, jnp.zeros_like(qb))
        s = lax.dot_general(qh, kb, (((1,), (1,)), ((), ())), preferred_element_type=F32)
        p = (s * dm_ref[h]).astype(BF16)
        inner.append(jnp.dot(p, v[:, h * LANES:(h + 1) * LANES], preferred_element_type=F32))
    o = jnp.concatenate(inner, axis=-1) + cross
    kv = lax.dot_general(kz, v, (((0,), (0,)), ((), ())), preferred_element_type=F32)
    state_sc[...] = st * dec_ref[...] + kv * bm_ref[...]

    vlane = lax.broadcasted_iota(jnp.int32, (1, LANES), 1) < RET_DV
    normed = []
    for h in range(RET_HEADS):
        xh = o[:, h * LANES:(h + 1) * LANES]
        mu = jnp.sum(xh, axis=-1, keepdims=True) * (1.0 / RET_DV)
        d = jnp.where(vlane, xh - mu, 0.0)
        var = jnp.sum(d * d, axis=-1, keepdims=True) * (1.0 / RET_DV)
        normed.append(d * lax.rsqrt(var + LN_EPS))
    y = jnp.concatenate(normed, axis=-1) * gng_ref[...]
    gate = g_ref[...].astype(F32)
    or_ref[...] = (gate * _sigmoid(gate) * y).astype(BF16)

    u = u_ref[...].astype(F32)
    tail_rows = tail_sc.shape[0]
    ext = jnp.concatenate([tail_sc[...], u], axis=0)
    tail_sc[...] = u[ch - tail_rows:, :]
    e2 = ext + pltpu.roll(ext, 1, 0)
    e4 = e2 + pltpu.roll(e2, 2, 0)
    e8 = e4 + pltpu.roll(e4, 4, 0)
    e16 = e8 + pltpu.roll(e8, 8, 0)
    glane = lax.broadcasted_iota(jnp.int32, (1, POOL_W), 1) // POOL_GROUP
    win = jnp.where(glane == 0, e2, jnp.where(glane == 1, e4, jnp.where(glane == 2, e8, e16)))
    win = win[tail_rows:, :]
    wlen = jnp.where(glane == 0, 2, jnp.where(glane == 1, 4, jnp.where(glane == 2, 8, 16)))
    pos = j * ch + lax.broadcasted_iota(jnp.int32, (ch, POOL_W), 0)
    cnt = jnp.minimum(pos + 1, wlen).astype(F32)
    pooled = (win / cnt - u).astype(BF16)
    yp = jnp.dot(pooled, pw_ref[...], preferred_element_type=F32) * ps_ref[...]
    op_ref[...] = yp.astype(BF16)

    z = fl_ref[...] + bf_ref[...]
    logf = jnp.minimum(z, 0.0) - jnp.log1p(jnp.exp(-jnp.abs(z)))
    row = lax.broadcasted_iota(jnp.int32, (ch, LANES), 0)
    sh = 1
    while sh < ch:
        logf = logf + jnp.where(row >= sh, pltpu.roll(logf, sh, 0), 0.0)
        sh *= 2
    c = logf + carry_sc[0:1, :]
    carry_sc[...] = jnp.broadcast_to(c[ch - 1:ch, :], carry_sc.shape)
    c_ref[...] = c * LOG2E


def _seq_mix(proj, flog, consts, params, layer, bsz, lp):
    ch = SEQ_BLOCK
    nc = lp // ch
    rows = bsz * lp

    def rowblk(width, colblk):
        return pl.BlockSpec((ch, width), lambda bb, j: (bb * nc + j, colblk))

    def const(shape):
        nd = len(shape)
        return pl.BlockSpec(shape, lambda bb, j: (0,) * nd)

    def layer_param(shape):
        return pl.BlockSpec((None,) + shape, lambda bb, j: (layer,) + (0,) * len(shape))

    in_specs = [
        rowblk(4 * LANES, QR_OFF // (4 * LANES)),
        rowblk(RETV_W, VR_OFF // RETV_W),
        rowblk(RETV_W, GR_OFF // RETV_W),
        rowblk(POOL_W, UP_OFF // POOL_W),
        pl.BlockSpec((ch, LANES), lambda bb, j: (bb * nc + j, 0)),
        pl.BlockSpec((ch, LANES), lambda bb, j: (j, 0)),
        pl.BlockSpec((ch, LANES), lambda bb, j: (j, 0)),
        const((ch, 2 * LANES)), const((ch, 2 * LANES)), const((RET_HEADS, ch, ch)),
        const((1, RETV_W)), const((2 * LANES, RETV_W)),
        layer_param((1, RETV_W)), layer_param((1, LANES)), layer_param((POOL_W, POOL_W)),
        layer_param((1, POOL_W)),
    ]
    out_shape = (jax.ShapeDtypeStruct((rows, RETV_W), BF16),
                 jax.ShapeDtypeStruct((rows, POOL_W), BF16),
                 jax.ShapeDtypeStruct((rows, LANES), F32))
    out_specs = (pl.BlockSpec((ch, RETV_W), lambda bb, j: (bb * nc + j, 0)),
                 pl.BlockSpec((ch, POOL_W), lambda bb, j: (bb * nc + j, 0)),
                 pl.BlockSpec((ch, LANES), lambda bb, j: (bb * nc + j, 0)))
    return pl.pallas_call(
        _seq_kernel,
        out_shape=out_shape,
        grid=(bsz, nc),
        in_specs=in_specs,
        out_specs=out_specs,
        scratch_shapes=[pltpu.VMEM((2 * LANES, RETV_W), F32),
                        pltpu.VMEM((16, POOL_W), F32),
                        pltpu.VMEM((8, LANES), F32)],
        compiler_params=pltpu.CompilerParams(dimension_semantics=("parallel", "arbitrary"),
                                             vmem_limit_bytes=VMEM_LIMIT),
        name="seq_mixers",
    )(proj, proj, proj, proj, flog, consts["cos"], consts["sin"],
      consts["xiq"], consts["zk"], consts["dm"], consts["dec"], consts["bm"],
      params["gn_g"], params["b_f"], params["pool_w"], params["pool_scale"])


FOX_AUG = 3
FOX_VROWS = FOX_DH + 16
FOX_LATE_HEADS = 2
FOX_UNROLL = 4


def _fox_select_matrix():
    sel = np.zeros((LANES, FOX_HEADS * LANES), np.float32)
    for a in range(FOX_AUG):
        for h in range(FOX_HEADS):
            sel[8 * a + h, h * LANES + FOX_DH + a] = -1.0
    return jnp.asarray(sel, BF16)


def _fox_kernel(q_ref, k_ref, v_ref, c_ref, sel_ref, o_ref,
                ka_sc, vt_sc, qa_sc, st_sc, m_sc, acc_sc):
    i = pl.program_id(1)
    tq = SEQ_BLOCK
    tk = SEQ_BLOCK
    npair = FOX_HEADS // 2
    lane = lax.broadcasted_iota(jnp.int32, (1, LANES), 1)

    @pl.when(i == 0)
    def _():
        def chunk(t, carry):
            r0 = pl.multiple_of(t * tk, tk)
            cc = c_ref[pl.ds(r0, tk), :]
            pieces, rem = [], cc
            for a in range(FOX_AUG):
                piece = rem.astype(BF16).astype(F32)
                pieces.append(piece if a == 0 else pltpu.roll(piece, 8 * a, 1))
                rem = rem - piece
            packed = jnp.where(lane < 8, pieces[0], jnp.where(lane < 16, pieces[1], pieces[2]))
            c_aug = jnp.dot(packed.astype(BF16), sel_ref[...], preferred_element_type=F32)
            for pr in range(npair):
                kk = k_ref[pl.ds(r0, tk), pr * LANES:(pr + 1) * LANES].astype(F32)
                vv = v_ref[pl.ds(r0, tk), pr * LANES:(pr + 1) * LANES].astype(F32)
                vtt = vv.T
                ones = jnp.ones((FOX_VROWS - FOX_DH, tk), F32)
                k_heads = (kk, pltpu.roll(kk, FOX_DH, 1))
                v_heads = (vtt[:FOX_DH], vtt[FOX_DH:])
                for hh in range(2):
                    h = 2 * pr + hh
                    aug = jnp.where(lane < FOX_DH, k_heads[hh], c_aug[:, h * LANES:(h + 1) * LANES])
                    ka_sc[h, pl.ds(r0, tk), :] = aug.astype(BF16)
                    vt_sc[h, :, pl.ds(r0, tk)] = jnp.concatenate([v_heads[hh], ones], axis=0).astype(BF16)
            return carry

        lax.fori_loop(0, k_ref.shape[0] // tk, chunk, 0)

    for pr in range(npair):
        qq = q_ref[:, pr * LANES:(pr + 1) * LANES].astype(F32)
        tail = jnp.where(lane < FOX_DH + FOX_AUG, 1.0, 0.0)
        qa_sc[2 * pr] = jnp.where(lane < FOX_DH, qq, tail).astype(BF16)
        qa_sc[2 * pr + 1] = jnp.where(lane < FOX_DH, pltpu.roll(qq, FOX_DH, 1), tail).astype(BF16)
    m_sc[...] = jnp.full(m_sc.shape, NEG_INF, F32)
    acc_sc[...] = jnp.zeros_like(acc_sc)

    def score(t, h):
        ks = t * tk if isinstance(t, int) else pl.multiple_of(t * tk, tk)
        st_sc[h] = lax.dot_general(ka_sc[h, pl.ds(ks, tk), :], qa_sc[h], (((1,), (1,)), ((), ())),
                                   preferred_element_type=F32)

    def attend(t, masked, next_t):
        ks = t * tk if isinstance(t, int) else pl.multiple_of(t * tk, tk)

        def load(h, c0):
            s = st_sc[h, :, c0:c0 + LANES]
            if masked:
                k_id = lax.broadcasted_iota(jnp.int32, (tk, LANES), 0)
                q_id = c0 + lax.broadcasted_iota(jnp.int32, (tk, LANES), 1)
                s = jnp.where(k_id <= q_id, s, NEG_INF)
            return s

        for h in range(early, FOX_HEADS):
            score(t, h)
        for h in range(FOX_HEADS):
            p_halves, a_halves = [], []
            for c0 in range(0, tq, LANES):
                m_prev = m_sc[h, :, c0:c0 + LANES]
                m_new = jnp.maximum(m_prev, jnp.max(load(h, c0), axis=0, keepdims=True))
                m_sc[h, :, c0:c0 + LANES] = m_new
                a_halves.append(jnp.exp2(m_prev - m_new))
                p_halves.append(jnp.exp2(load(h, c0) - m_new).astype(BF16))
            p_t = jnp.concatenate(p_halves, axis=1)
            pv = jnp.dot(vt_sc[h, :, pl.ds(ks, tk)], p_t, preferred_element_type=F32)
            acc_sc[h] = jnp.concatenate(a_halves, axis=1) * acc_sc[h] + pv
            if next_t is not None and h < early:
                score(next_t, h)

    early = FOX_HEADS - FOX_LATE_HEADS
    for h in range(early):
        score(0, h)

    def run(t0, count):
        for d in range(count):
            attend(t0 + d, False, t0 + d + 1)

    def body(u, carry):
        run(FOX_UNROLL * u, FOX_UNROLL)
        return carry

    lax.fori_loop(0, i // FOX_UNROLL, body, 0)
    done = (i // FOX_UNROLL) * FOX_UNROLL
    span = FOX_UNROLL // 2
    while span >= 1:
        take = (i - done) >= span

        @pl.when(take)
        def _(done=done, span=span):
            run(done, span)

        done = done + jnp.where(take, span, 0)
        span //= 2

    attend(i, True, None)

    outs = []
    for pr in range(npair):
        acc_a = acc_sc[2 * pr]
        acc_b = acc_sc[2 * pr + 1]
        o_t = jnp.concatenate([acc_a[:FOX_DH] / acc_a[FOX_DH:FOX_DH + 1],
                               acc_b[:FOX_DH] / acc_b[FOX_DH:FOX_DH + 1]], axis=0)
        outs.append(o_t.T)
    o_ref[...] = jnp.concatenate(outs, axis=-1).astype(BF16)


def _fox(qkv, cum, bsz, lp):
    tq = SEQ_BLOCK
    nq = lp // tq
    rows = bsz * lp
    return pl.pallas_call(
        _fox_kernel,
        out_shape=jax.ShapeDtypeStruct((rows, FOX_W), BF16),
        grid=(bsz, nq),
        in_specs=[
            pl.BlockSpec((tq, FOX_W), lambda bb, i: (bb * nq + i, 0)),
            pl.BlockSpec((lp, FOX_W), lambda bb, i: (bb, 1)),
            pl.BlockSpec((lp, FOX_W), lambda bb, i: (bb, 2)),
            pl.BlockSpec((lp, LANES), lambda bb, i: (bb, 0)),
            pl.BlockSpec((LANES, FOX_HEADS * LANES), lambda bb, i: (0, 0)),
        ],
        out_specs=pl.BlockSpec((tq, FOX_W), lambda bb, i: (bb * nq + i, 0)),
        scratch_shapes=[pltpu.VMEM((FOX_HEADS, lp, LANES), BF16),
                        pltpu.VMEM((FOX_HEADS, FOX_VROWS, lp), BF16),
                        pltpu.VMEM((FOX_HEADS, tq, LANES), BF16),
                        pltpu.VMEM((FOX_HEADS, SEQ_BLOCK, tq), F32),
                        pltpu.VMEM((FOX_HEADS, 1, tq), F32),
                        pltpu.VMEM((FOX_HEADS, FOX_VROWS, tq), F32)],
        compiler_params=pltpu.CompilerParams(
            dimension_semantics=("parallel", "arbitrary"),
            vmem_limit_bytes=VMEM_LIMIT),
        name="fox_attention",
    )(qkv, qkv, qkv, cum, _fox_select_matrix())


def _mix_ffn_kernel(or_ref, op_ref, of_ref, h_ref, wo_ref, g1_ref, b1_ref,
                    w1_ref, w3_ref, w2_ref, g2_ref, b2_ref, o_ref):
    mix = jnp.concatenate([or_ref[...], op_ref[...], of_ref[...]], axis=-1)
    y = ALPHA * h_ref[...] + jnp.dot(mix, wo_ref[...], preferred_element_type=F32)
    h1 = _layer_norm(y, g1_ref[...], b1_ref[...])
    xb = h1.astype(BF16)
    acc = None
    for c0 in range(0, D_FF, FF_CHUNK):
        a = jnp.dot(xb, w1_ref[:, c0:c0 + FF_CHUNK], preferred_element_type=F32)
        b = jnp.dot(xb, w3_ref[:, c0:c0 + FF_CHUNK], preferred_element_type=F32)
        t = (a * _sigmoid(a) * b).astype(BF16)
        part = jnp.dot(t, w2_ref[c0:c0 + FF_CHUNK, :], preferred_element_type=F32)
        acc = part if acc is None else acc + part
    o_ref[...] = _layer_norm(ALPHA * h1 + acc, g2_ref[...], b2_ref[...])


def _mix_ffn(o_r, o_p, o_f, h, params, layer, tm):
    rows = h.shape[0]

    def row(width):
        return pl.BlockSpec((tm, width), lambda i: (i, 0))

    def resident(shape):
        return pl.BlockSpec((None,) + shape, lambda i: (layer,) + (0,) * len(shape),
                            pipeline_mode=pl.Buffered(1))

    vec = resident((1, D_MODEL))
    return pl.pallas_call(
        _mix_ffn_kernel,
        out_shape=jax.ShapeDtypeStruct((rows, D_MODEL), F32),
        grid=(rows // tm,),
        in_specs=[row(RETV_W), row(POOL_W), row(FOX_W), row(D_MODEL),
                  resident((MIX_W, D_MODEL)), vec, vec,
                  resident((D_MODEL, D_FF)), resident((D_MODEL, D_FF)), resident((D_FF, D_MODEL)), vec, vec],
        out_specs=row(D_MODEL),
        compiler_params=pltpu.CompilerParams(dimension_semantics=("parallel",),
                                             vmem_limit_bytes=VMEM_LIMIT),
        name="mix_ffn_ln",
    )(o_r, o_p, o_f, h, params["w_out"], params["ln1_g"], params["ln1_b"],
      params["w1"], params["w3"], params["w2"], params["ln2_g"], params["ln2_b"])


def _pack_w_in(w):
    d = w.shape[0]

    def rot_pack(seg):
        s = seg.reshape(d, RET_HEADS, RET_DK)
        x1 = s[:, :, :RET_HALF].reshape(d, RET_HEADS * RET_HALF)
        x2 = s[:, :, RET_HALF:].reshape(d, RET_HEADS * RET_HALF)
        z = jnp.zeros((d, LANES - RET_HEADS * RET_HALF), w.dtype)
        return jnp.concatenate([x1, z, x2, z], axis=1)

    def head_pad(seg):
        s = seg.reshape(d, RET_HEADS, RET_DV)
        return jnp.pad(s, ((0, 0), (0, 0), (0, RET_HEAD_PAD - RET_DV))).reshape(d, RETV_W)

    o = 0
    segs = []
    for sz in (RET_QK, RET_QK, RET_W, RET_W, POOL_W, FOX_W, FOX_W, FOX_W, FOX_HEADS):
        segs.append(w[:, o:o + sz])
        o += sz
    q_r, k_r, v_r, g_r, u_p, q_f, k_f, v_f, f_l = segs
    packed = jnp.concatenate(
        [rot_pack(q_r), rot_pack(k_r), head_pad(v_r), head_pad(g_r), u_p, q_f, k_f, v_f,
         jnp.pad(f_l, ((0, 0), (0, LANES - FOX_HEADS)))], axis=1)
    return packed.astype(BF16)


def _pack_w_out(w):
    d = w.shape[1]
    w_r = jnp.pad(w[:RET_W].reshape(RET_HEADS, RET_DV, d),
                  ((0, 0), (0, RET_HEAD_PAD - RET_DV), (0, 0))).reshape(RETV_W, d)
    return jnp.concatenate([w_r, w[RET_W:]], axis=0).astype(BF16)


def _pad_heads(vec):
    return jnp.pad(vec.reshape(RET_HEADS, RET_DV), ((0, 0), (0, RET_HEAD_PAD - RET_DV))).reshape(1, RETV_W)


def _block_diag(pw):
    g = len(POOL_WINDOWS)
    eye = jnp.eye(g, dtype=pw.dtype)
    return jnp.einsum("gij,gh->gihj", pw, eye).reshape(POOL_W, POOL_W).astype(BF16)


def _prepare_params(w_in, b_f, ret_gn_g, pool_w, pool_scale, w_out, ln1_g, ln1_b,
                    w_ffn1, w_ffn3, w_ffn2, ln2_g, ln2_b):
    depth = w_in.shape[0]
    vec = lambda a: a.reshape(depth, 1, a.shape[-1])
    return dict(
        w_in=jax.vmap(_pack_w_in)(w_in),
        gn_g=jax.vmap(_pad_heads)(ret_gn_g),
        b_f=vec(jnp.pad(b_f, ((0, 0), (0, LANES - FOX_HEADS)))),
        pool_w=jax.vmap(_block_diag)(pool_w),
        pool_scale=vec(pool_scale),
        w_out=jax.vmap(_pack_w_out)(w_out),
        ln1_g=vec(ln1_g), ln1_b=vec(ln1_b), ln2_g=vec(ln2_g), ln2_b=vec(ln2_b),
        w1=w_ffn1.astype(BF16), w3=w_ffn3.astype(BF16), w2=w_ffn2.astype(BF16))


def _retention_tables():
    ch = SEQ_BLOCK
    gamma = (1.0 - 2.0 ** (-5.0 - np.arange(RET_HEADS, dtype=np.float32))).astype(np.float32)
    lg = np.log(gamma).astype(np.float32)
    i = np.arange(ch, dtype=np.float32)
    diff = i[:, None] - i[None, :]
    dm = np.where(diff >= 0, np.exp(lg[:, None, None] * np.maximum(diff, 0.0)), 0.0).astype(np.float32)
    xi = np.exp(lg[:, None] * (i + 1.0)).astype(np.float32)
    zeta = np.exp(lg[:, None] * (ch - 1.0 - i)).astype(np.float32)
    lane = np.arange(2 * LANES)
    within = lane % LANES
    lane_head = np.where(within < RET_HEADS * RET_HALF, within // RET_HALF, -1)
    xiq = np.zeros((ch, 2 * LANES), np.float32)
    zk = np.zeros((ch, 2 * LANES), np.float32)
    bm = np.zeros((2 * LANES, RETV_W), np.float32)
    for h in range(RET_HEADS):
        sel = lane_head == h
        xiq[:, sel] = xi[h][:, None]
        zk[:, sel] = zeta[h][:, None]
        bm[sel, h * LANES:(h + 1) * LANES] = 1.0
    dec = np.repeat(np.exp(lg * ch).astype(np.float32), LANES)[None, :]
    return dict(xiq=jnp.asarray(xiq), zk=jnp.asarray(zk), dm=jnp.asarray(dm),
                dec=jnp.asarray(dec), bm=jnp.asarray(bm))


def _rotary_tables(lp):
    pos = jnp.arange(lp, dtype=F32)
    inv_freq = ROPE_BASE ** (-jnp.arange(RET_HALF, dtype=F32) / RET_HALF)
    ang = pos[:, None] * inv_freq[None, :]
    pad = LANES - RET_HEADS * RET_HALF
    cos = jnp.pad(jnp.tile(jnp.cos(ang), (1, RET_HEADS)), ((0, 0), (0, pad)))
    sin = jnp.pad(jnp.tile(jnp.sin(ang), (1, RET_HEADS)), ((0, 0), (0, pad)))
    return cos, sin


def kernel(x, meta, ln_emb_g, ln_emb_b, w_in, b_f, ret_gn_g, pool_w, pool_scale, w_out, ln1_g, ln1_b,
           w_ffn1, w_ffn3, w_ffn2, ln2_g, ln2_b):
    bsz, seq, d = x.shape
    assert d == D_MODEL and seq % SEQ_BLOCK == 0
    depth = w_in.shape[0]
    assert depth == DEPTH
    lp = seq + SEQ_BLOCK
    rows = bsz * lp
    tm = ROW_TILE if rows % ROW_TILE == 0 else SEQ_BLOCK

    consts = _retention_tables()
    consts["cos"], consts["sin"] = _rotary_tables(lp)
    params = _prepare_params(w_in, b_f, ret_gn_g, pool_w, pool_scale, w_out, ln1_g, ln1_b,
                             w_ffn1, w_ffn3, w_ffn2, ln2_g, ln2_b)

    h = _embed(x, meta, ln_emb_g, ln_emb_b, lp).reshape(rows, d)
    for l in range(depth):
        ret, qkv, flog = _inproj(h, params["w_in"], l, tm)
        o_r, o_p, cum = _seq_mix(ret, flog, consts, params, l, bsz, lp)
        o_f = _fox(qkv, cum, bsz, lp)
        h = _mix_ffn(o_r, o_p, o_f, h, params, l, tm)
    return h.reshape(bsz, lp, d)[:, N_META:N_META + seq]
```

```python
import functools

import numpy as np
import jax
import jax.numpy as jnp
from jax import lax
from jax.experimental import pallas as pl
from jax.experimental.pallas import tpu as pltpu

F32 = jnp.float32
BF16 = jnp.bfloat16

D_MODEL = 1024
N_META = 16
RET_HEADS = 4
RET_DK = 48
RET_HALF = RET_DK // 2
RET_DV = 96
RET_QK = RET_HEADS * RET_DK
RET_W = RET_HEADS * RET_DV
POOL_WINDOWS = (2, 4, 8, 16)
POOL_GROUP = 64
POOL_W = len(POOL_WINDOWS) * POOL_GROUP
FOX_HEADS = 6
FOX_DH = 64
FOX_W = FOX_HEADS * FOX_DH
D_FF = 2816
ROPE_BASE = 10000.0
LN_EPS = 1e-5
NEG_INF = -1e30
DEPTH = 2
ALPHA = (2.0 * DEPTH) ** 0.25
LOG2E = 1.4426950408889634

LANES = 128
MXU_DIM = 256
SEQ_BLOCK = 256
ROW_TILE = 1024
VMEM_LIMIT = 56 * 1024 * 1024

RET_HEAD_PAD = LANES
QR_OFF = 0
KR_OFF = 2 * LANES
VR_OFF = 4 * LANES
GR_OFF = VR_OFF + RET_HEADS * RET_HEAD_PAD
UP_OFF = GR_OFF + RET_HEADS * RET_HEAD_PAD
QF_OFF = UP_OFF + POOL_W
KF_OFF = QF_OFF + FOX_W
VF_OFF = KF_OFF + FOX_W
FL_OFF = VF_OFF + FOX_W
RET_COLS = QF_OFF
FOX_COLS = 3 * FOX_W
N_PACK = FL_OFF + LANES
RETV_W = RET_HEADS * RET_HEAD_PAD
MIX_W = RETV_W + POOL_W + FOX_W
PROJ_CHUNK = 512
FF_CHUNK = 256


def _layer_norm(x, g, b):
    mu = jnp.mean(x, axis=-1, keepdims=True)
    d = x - mu
    var = jnp.mean(d * d, axis=-1, keepdims=True)
    return d * lax.rsqrt(var + LN_EPS) * g + b


def _sigmoid(x):
    return 1.0 / (1.0 + jnp.exp(-x))


def _project(xb, w_ref, cs_ref, ret_ref, fox_ref, flog_ref):
    def cols(c0, width):
        r = jnp.dot(xb, w_ref[:, c0:c0 + width], preferred_element_type=F32)
        return r * cs_ref[:, c0:c0 + width]

    for c0 in range(0, RET_COLS, PROJ_CHUNK):
        width = min(PROJ_CHUNK, RET_COLS - c0)
        ret_ref[:, c0:c0 + width] = cols(c0, width).astype(BF16)
    for c0 in range(0, FOX_COLS - LANES, PROJ_CHUNK):
        fox_ref[:, c0:c0 + PROJ_CHUNK] = cols(QF_OFF + c0, PROJ_CHUNK).astype(BF16)
    tail = cols(FL_OFF - LANES, 2 * LANES)
    fox_ref[:, FOX_COLS - LANES:] = tail[:, :LANES].astype(BF16)
    flog_ref[...] = tail[:, LANES:]


def _embed_inproj_kernel(meta_ref, xm_ref, xe_ref, g_ref, b_ref, w_ref, cs_ref,
                         h_ref, ret_ref, fox_ref, flog_ref):
    j = pl.program_id(1)
    last = pl.num_programs(1) - 1
    top = jnp.where(j == 0, meta_ref[...], xe_ref[...])
    body = jnp.where(j == last, 0.0, xm_ref[0:SEQ_BLOCK - N_META, :])
    rows = jnp.concatenate([top, body], axis=0)
    h = _layer_norm(rows, g_ref[...], b_ref[...])
    h_ref[...] = h
    _project(h.astype(BF16), w_ref, cs_ref, ret_ref, fox_ref, flog_ref)


def _embed_inproj(x, meta, g, b, w, lp):
    bsz, seq, d = x.shape
    nblk = lp // SEQ_BLOCK
    n_xblk = seq // SEQ_BLOCK
    per = SEQ_BLOCK // N_META
    rows = bsz * lp
    out_row = lambda width: pl.BlockSpec((SEQ_BLOCK, width), lambda bb, j: (bb * nblk + j, 0))
    return pl.pallas_call(
        _embed_inproj_kernel,
        out_shape=(jax.ShapeDtypeStruct((rows, d), F32),
                   jax.ShapeDtypeStruct((rows, RET_COLS), BF16),
                   jax.ShapeDtypeStruct((rows, FOX_COLS), BF16),
                   jax.ShapeDtypeStruct((rows, LANES), F32)),
        grid=(bsz, nblk),
        in_specs=[
            pl.BlockSpec((N_META, d), lambda bb, j: (0, 0)),
            pl.BlockSpec((None, SEQ_BLOCK, d), lambda bb, j: (bb, jnp.minimum(j, n_xblk - 1), 0)),
            pl.BlockSpec((None, N_META, d), lambda bb, j: (bb, jnp.maximum(per * j - 1, 0), 0)),
            pl.BlockSpec((1, d), lambda bb, j: (0, 0)),
            pl.BlockSpec((1, d), lambda bb, j: (0, 0)),
            pl.BlockSpec((None, D_MODEL, N_PACK), lambda bb, j: (0, 0, 0)),
            pl.BlockSpec((1, N_PACK), lambda bb, j: (0, 0)),
        ],
        out_specs=(out_row(d), out_row(RET_COLS), out_row(FOX_COLS), out_row(LANES)),
        compiler_params=pltpu.CompilerParams(dimension_semantics=("parallel", "arbitrary"),
                                             vmem_limit_bytes=VMEM_LIMIT),
        name="embed_in_proj",
    )(meta, x, x, g.reshape(1, d), b.reshape(1, d), w, _inproj_col_scale())


def _inproj_kernel(h_ref, w_ref, cs_ref, ret_ref, fox_ref, flog_ref):
    _project(h_ref[...].astype(BF16), w_ref, cs_ref, ret_ref, fox_ref, flog_ref)


def _inproj_col_scale():
    cs = np.ones((1, N_PACK), np.float32)
    cs[0, KR_OFF:KR_OFF + 2 * LANES] = RET_DK ** -0.5
    cs[0, QF_OFF:QF_OFF + FOX_W] = FOX_DH ** -0.5 * LOG2E
    return jnp.asarray(cs)


def _inproj(h, w, layer, tm):
    rows = h.shape[0]
    return pl.pallas_call(
        _inproj_kernel,
        out_shape=(jax.ShapeDtypeStruct((rows, RET_COLS), BF16),
                   jax.ShapeDtypeStruct((rows, FOX_COLS), BF16),
                   jax.ShapeDtypeStruct((rows, LANES), F32)),
        grid=(rows // tm,),
        in_specs=[
            pl.BlockSpec((tm, D_MODEL), lambda i: (i, 0)),
            pl.BlockSpec((None, D_MODEL, N_PACK), lambda i: (layer, 0, 0)),
            pl.BlockSpec((1, N_PACK), lambda i: (0, 0)),
        ],
        out_specs=(pl.BlockSpec((tm, RET_COLS), lambda i: (i, 0)),
                   pl.BlockSpec((tm, FOX_COLS), lambda i: (i, 0)),
                   pl.BlockSpec((tm, LANES), lambda i: (i, 0))),
        compiler_params=pltpu.CompilerParams(dimension_semantics=("parallel",),
                                             vmem_limit_bytes=VMEM_LIMIT),
        name="in_proj",
    )(h, w, _inproj_col_scale())


def _seq_kernel(qk_ref, v_ref, g_ref, u_ref, fl_ref, cos_ref, sin_ref,
                xiq_ref, zk_ref, dm_ref, dec_ref, bm_ref, gng_ref, bf_ref, pw_ref, ps_ref,
                or_ref, op_ref, c_ref,
                state_sc, tail_sc, carry_sc):
    j = pl.program_id(1)
    ch = SEQ_BLOCK

    @pl.when(j == 0)
    def _():
        state_sc[...] = jnp.zeros_like(state_sc)
        tail_sc[...] = jnp.zeros_like(tail_sc)
        carry_sc[...] = jnp.zeros_like(carry_sc)

    qk = qk_ref[...].astype(F32)
    cs = cos_ref[...]
    sn = sin_ref[...]
    q1, q2 = qk[:, 0:LANES], qk[:, LANES:2 * LANES]
    k1, k2 = qk[:, 2 * LANES:3 * LANES], qk[:, 3 * LANES:4 * LANES]
    qr = jnp.concatenate([q1 * cs - q2 * sn, q1 * sn + q2 * cs], axis=-1)
    kr = jnp.concatenate([k1 * cs - k2 * sn, k1 * sn + k2 * cs], axis=-1)
    qb = qr.astype(BF16)
    qx = (qr * xiq_ref[...]).astype(BF16)
    kb = kr.astype(BF16)
    kz = (kr * zk_ref[...]).astype(BF16)
    v = v_ref[...]
    st = state_sc[...]
    cross = jnp.dot(qx, st.astype(BF16), preferred_element_type=F32)
    qlane = lax.broadcasted_iota(jnp.int32, (1, 2 * LANES), 1)
    qhead = jnp.where(qlane % LANES < RET_HEADS * RET_HALF, (qlane % LANES) // RET_HALF, RET_HEADS)
    inner = []
    for h in range(RET_HEADS):
        qh = jnp.where(qhead == h, qb, jnp.zeros_like(qb))
        s = lax.dot_general(qh, kb, (((1,), (1,)), ((), ())), preferred_element_type=F32)
        p = (s * dm_ref[h]).astype(BF16)
        inner.append(jnp.dot(p, v[:, h * LANES:(h + 1) * LANES], preferred_element_type=F32))
    o = jnp.concatenate(inner, axis=-1) + cross
    kv = lax.dot_general(kz, v, (((0,), (0,)), ((), ())), preferred_element_type=F32)
    state_sc[...] = st * dec_ref[...] + kv * bm_ref[...]

    vlane = lax.broadcasted_iota(jnp.int32, (1, LANES), 1) < RET_DV
    normed = []
    for h in range(RET_HEADS):
        xh = o[:, h * LANES:(h + 1) * LANES]
        mu = jnp.sum(xh, axis=-1, keepdims=True) * (1.0 / RET_DV)
        d = jnp.where(vlane, xh - mu, 0.0)
        var = jnp.sum(d * d, axis=-1, keepdims=True) * (1.0 / RET_DV)
        normed.append(d * lax.rsqrt(var + LN_EPS))
    y = jnp.concatenate(normed, axis=-1) * gng_ref[...]
    gate = g_ref[...].astype(F32)
    or_ref[...] = (gate * _sigmoid(gate) * y).astype(BF16)

    u = u_ref[...].astype(F32)
    tail_rows = tail_sc.shape[0]
    ext = jnp.concatenate([tail_sc[...], u], axis=0)
    tail_sc[...] = u[ch - tail_rows:, :]
    e2 = ext + pltpu.roll(ext, 1, 0)
    e4 = e2 + pltpu.roll(e2, 2, 0)
    e8 = e4 + pltpu.roll(e4, 4, 0)
    e16 = e8 + pltpu.roll(e8, 8, 0)
    glane = lax.broadcasted_iota(jnp.int32, (1, POOL_W), 1) // POOL_GROUP
    win = jnp.where(glane == 0, e2, jnp.where(glane == 1, e4, jnp.where(glane == 2, e8, e16)))
    win = win[tail_rows:, :]
    wlen = jnp.where(glane == 0, 2, jnp.where(glane == 1, 4, jnp.where(glane == 2, 8, 16)))
    pos = j * ch + lax.broadcasted_iota(jnp.int32, (ch, POOL_W), 0)
    cnt = jnp.minimum(pos + 1, wlen).astype(F32)
    pooled = (win / cnt - u).astype(BF16)
    yp = jnp.dot(pooled, pw_ref[...], preferred_element_type=F32) * ps_ref[...]
    op_ref[...] = yp.astype(BF16)

    z = fl_ref[...] + bf_ref[...]
    logf = jnp.minimum(z, 0.0) - jnp.log1p(jnp.exp(-jnp.abs(z)))
    row = lax.broadcasted_iota(jnp.int32, (ch, LANES), 0)
    sh = 1
    while sh < ch:
        logf = logf + jnp.where(row >= sh, pltpu.roll(logf, sh, 0), 0.0)
        sh *= 2
    c = logf + carry_sc[0:1, :]
    carry_sc[...] = jnp.broadcast_to(c[ch - 1:ch, :], carry_sc.shape)
    c_ref[...] = c * LOG2E


def _seq_mix(proj, flog, consts, params, layer, bsz, lp):
    ch = SEQ_BLOCK
    nc = lp // ch
    rows = bsz * lp

    def rowblk(width, colblk):
        return pl.BlockSpec((ch, width), lambda bb, j: (bb * nc + j, colblk))

    def const(shape):
        nd = len(shape)
        return pl.BlockSpec(shape, lambda bb, j: (0,) * nd)

    def layer_param(shape):
        return pl.BlockSpec((None,) + shape, lambda bb, j: (layer,) + (0,) * len(shape))

    in_specs = [
        rowblk(4 * LANES, QR_OFF // (4 * LANES)),
        rowblk(RETV_W, VR_OFF // RETV_W),
        rowblk(RETV_W, GR_OFF // RETV_W),
        rowblk(POOL_W, UP_OFF // POOL_W),
        pl.BlockSpec((ch, LANES), lambda bb, j: (bb * nc + j, 0)),
        pl.BlockSpec((ch, LANES), lambda bb, j: (j, 0)),
        pl.BlockSpec((ch, LANES), lambda bb, j: (j, 0)),
        const((ch, 2 * LANES)), const((ch, 2 * LANES)), const((RET_HEADS, ch, ch)),
        const((1, RETV_W)), const((2 * LANES, RETV_W)),
        layer_param((1, RETV_W)), layer_param((1, LANES)), layer_param((POOL_W, POOL_W)),
        layer_param((1, POOL_W)),
    ]
    out_shape = (jax.ShapeDtypeStruct((rows, RETV_W), BF16),
                 jax.ShapeDtypeStruct((rows, POOL_W), BF16),
                 jax.ShapeDtypeStruct((rows, LANES), F32))
    out_specs = (pl.BlockSpec((ch, RETV_W), lambda bb, j: (bb * nc + j, 0)),
                 pl.BlockSpec((ch, POOL_W), lambda bb, j: (bb * nc + j, 0)),
                 pl.BlockSpec((ch, LANES), lambda bb, j: (bb * nc + j, 0)))
    return pl.pallas_call(
        _seq_kernel,
        out_shape=out_shape,
        grid=(bsz, nc),
        in_specs=in_specs,
        out_specs=out_specs,
        scratch_shapes=[pltpu.VMEM((2 * LANES, RETV_W), F32),
                        pltpu.VMEM((16, POOL_W), F32),
                        pltpu.VMEM((8, LANES), F32)],
        compiler_params=pltpu.CompilerParams(dimension_semantics=("parallel", "arbitrary"),
                                             vmem_limit_bytes=VMEM_LIMIT),
        name="seq_mixers",
    )(proj, proj, proj, proj, flog, consts["cos"], consts["sin"],
      consts["xiq"], consts["zk"], consts["dm"], consts["dec"], consts["bm"],
      params["gn_g"], params["b_f"], params["pool_w"], params["pool_scale"])


FOX_AUG = 3
FOX_VROWS = FOX_DH + 16
FOX_LATE_HEADS = 2
FOX_UNROLL = 4


def _fox_select_matrix():
    sel = np.zeros((LANES, FOX_HEADS * LANES), np.float32)
    for a in range(FOX_AUG):
        for h in range(FOX_HEADS):
            sel[8 * a + h, h * LANES + FOX_DH + a] = -1.0
    return jnp.asarray(sel, BF16)


def _fox_kernel(q_ref, k_ref, v_ref, c_ref, sel_ref, o_ref,
                ka_sc, vt_sc, qa_sc, st_sc, m_sc, acc_sc):
    i = pl.program_id(1)
    tq = SEQ_BLOCK
    tk = SEQ_BLOCK
    npair = FOX_HEADS // 2
    lane = lax.broadcasted_iota(jnp.int32, (1, LANES), 1)

    @pl.when(i == 0)
    def _():
        def chunk(t, carry):
            r0 = pl.multiple_of(t * tk, tk)
            cc = c_ref[pl.ds(r0, tk), :]
            pieces, rem = [], cc
            for a in range(FOX_AUG):
                piece = rem.astype(BF16).astype(F32)
                pieces.append(piece if a == 0 else pltpu.roll(piece, 8 * a, 1))
                rem = rem - piece
            packed = jnp.where(lane < 8, pieces[0], jnp.where(lane < 16, pieces[1], pieces[2]))
            c_aug = jnp.dot(packed.astype(BF16), sel_ref[...], preferred_element_type=F32)
            for pr in range(npair):
                kk = k_ref[pl.ds(r0, tk), pr * LANES:(pr + 1) * LANES].astype(F32)
                vv = v_ref[pl.ds(r0, tk), pr * LANES:(pr + 1) * LANES].astype(F32)
                vtt = vv.T
                ones = jnp.ones((FOX_VROWS - FOX_DH, tk), F32)
                k_heads = (kk, pltpu.roll(kk, FOX_DH, 1))
                v_heads = (vtt[:FOX_DH], vtt[FOX_DH:])
                for hh in range(2):
                    h = 2 * pr + hh
                    aug = jnp.where(lane < FOX_DH, k_heads[hh], c_aug[:, h * LANES:(h + 1) * LANES])
                    ka_sc[h, pl.ds(r0, tk), :] = aug.astype(BF16)
                    vt_sc[h, :, pl.ds(r0, tk)] = jnp.concatenate([v_heads[hh], ones], axis=0).astype(BF16)
            return carry

        lax.fori_loop(0, k_ref.shape[0] // tk, chunk, 0)

    for pr in range(npair):
        qq = q_ref[:, pr * LANES:(pr + 1) * LANES].astype(F32)
        tail = jnp.where(lane < FOX_DH + FOX_AUG, 1.0, 0.0)
        qa_sc[2 * pr] = jnp.where(lane < FOX_DH, qq, tail).astype(BF16)
        qa_sc[2 * pr + 1] = jnp.where(lane < FOX_DH, pltpu.roll(qq, FOX_DH, 1), tail).astype(BF16)
    m_sc[...] = jnp.full(m_sc.shape, NEG_INF, F32)
    acc_sc[...] = jnp.zeros_like(acc_sc)

    def score(t, h):
        ks = t * tk if isinstance(t, int) else pl.multiple_of(t * tk, tk)
        st_sc[h] = lax.dot_general(ka_sc[h, pl.ds(ks, tk), :], qa_sc[h], (((1,), (1,)), ((), ())),
                                   preferred_element_type=F32)

    def attend(t, masked, next_t):
        ks = t * tk if isinstance(t, int) else pl.multiple_of(t * tk, tk)

        def load(h, c0):
            s = st_sc[h, :, c0:c0 + LANES]
            if masked:
                k_id = lax.broadcasted_iota(jnp.int32, (tk, LANES), 0)
                q_id = c0 + lax.broadcasted_iota(jnp.int32, (tk, LANES), 1)
                s = jnp.where(k_id <= q_id, s, NEG_INF)
            return s

        for h in range(early, FOX_HEADS):
            score(t, h)
        for h in range(FOX_HEADS):
            p_halves, a_halves = [], []
            for c0 in range(0, tq, LANES):
                m_prev = m_sc[h, :, c0:c0 + LANES]
                m_new = jnp.maximum(m_prev, jnp.max(load(h, c0), axis=0, keepdims=True))
                m_sc[h, :, c0:c0 + LANES] = m_new
                a_halves.append(jnp.exp2(m_prev - m_new))
                p_halves.append(jnp.exp2(load(h, c0) - m_new).astype(BF16))
            p_t = jnp.concatenate(p_halves, axis=1)
            pv = jnp.dot(vt_sc[h, :, pl.ds(ks, tk)], p_t, preferred_element_type=F32)
            acc_sc[h] = jnp.concatenate(a_halves, axis=1) * acc_sc[h] + pv
            if next_t is not None and h < early:
                score(next_t, h)

    early = FOX_HEADS - FOX_LATE_HEADS
    for h in range(early):
        score(0, h)

    def run(t0, count):
        for d in range(count):
            attend(t0 + d, False, t0 + d + 1)

    def body(u, carry):
        run(FOX_UNROLL * u, FOX_UNROLL)
        return carry

    lax.fori_loop(0, i // FOX_UNROLL, body, 0)
    done = (i // FOX_UNROLL) * FOX_UNROLL
    span = FOX_UNROLL // 2
    while span >= 1:
        take = (i - done) >= span

        @pl.when(take)
        def _(done=done, span=span):
            run(done, span)

        done = done + jnp.where(take, span, 0)
        span //= 2

    attend(i, True, None)

    outs = []
    for pr in range(npair):
        acc_a = acc_sc[2 * pr]
        acc_b = acc_sc[2 * pr + 1]
        o_t = jnp.concatenate([acc_a[:FOX_DH] / acc_a[FOX_DH:FOX_DH + 1],
                               acc_b[:FOX_DH] / acc_b[FOX_DH:FOX_DH + 1]], axis=0)
        outs.append(o_t.T)
    o_ref[...] = jnp.concatenate(outs, axis=-1).astype(BF16)


def _fox(qkv, cum, bsz, lp):
    tq = SEQ_BLOCK
    nq = lp // tq
    rows = bsz * lp
    return pl.pallas_call(
        _fox_kernel,
        out_shape=jax.ShapeDtypeStruct((rows, FOX_W), BF16),
        grid=(bsz, nq),
        in_specs=[
            pl.BlockSpec((tq, FOX_W), lambda bb, i: (bb * nq + i, 0)),
            pl.BlockSpec((lp, FOX_W), lambda bb, i: (bb, 1)),
            pl.BlockSpec((lp, FOX_W), lambda bb, i: (bb, 2)),
            pl.BlockSpec((lp, LANES), lambda bb, i: (bb, 0)),
            pl.BlockSpec((LANES, FOX_HEADS * LANES), lambda bb, i: (0, 0)),
        ],
        out_specs=pl.BlockSpec((tq, FOX_W), lambda bb, i: (bb * nq + i, 0)),
        scratch_shapes=[pltpu.VMEM((FOX_HEADS, lp, LANES), BF16),
                        pltpu.VMEM((FOX_HEADS, FOX_VROWS, lp), BF16),
                        pltpu.VMEM((FOX_HEADS, tq, LANES), BF16),
                        pltpu.VMEM((FOX_HEADS, SEQ_BLOCK, tq), F32),
                        pltpu.VMEM((FOX_HEADS, 1, tq), F32),
                        pltpu.VMEM((FOX_HEADS, FOX_VROWS, tq), F32)],
        compiler_params=pltpu.CompilerParams(
            dimension_semantics=("parallel", "arbitrary"),
            vmem_limit_bytes=VMEM_LIMIT),
        name="fox_attention",
    )(qkv, qkv, qkv, cum, _fox_select_matrix())


def _mix_ffn_kernel(or_ref, op_ref, of_ref, h_ref, wo_ref, g1_ref, b1_ref,
                    w1_ref, w3_ref, w2_ref, g2_ref, b2_ref, o_ref):
    mix = jnp.concatenate([or_ref[...], op_ref[...], of_ref[...]], axis=-1)
    y = ALPHA * h_ref[...] + jnp.dot(mix, wo_ref[...], preferred_element_type=F32)
    h1 = _layer_norm(y, g1_ref[...], b1_ref[...])
    xb = h1.astype(BF16)
    acc = None
    for c0 in range(0, D_FF, FF_CHUNK):
        a = jnp.dot(xb, w1_ref[:, c0:c0 + FF_CHUNK], preferred_element_type=F32)
        b = jnp.dot(xb, w3_ref[:, c0:c0 + FF_CHUNK], preferred_element_type=F32)
        t = (a * _sigmoid(a) * b).astype(BF16)
        part = jnp.dot(t, w2_ref[c0:c0 + FF_CHUNK, :], preferred_element_type=F32)
        acc = part if acc is None else acc + part
    o_ref[...] = _layer_norm(ALPHA * h1 + acc, g2_ref[...], b2_ref[...])


def _mix_ffn(o_r, o_p, o_f, h, params, layer, tm):
    rows = h.shape[0]

    def row(width):
        return pl.BlockSpec((tm, width), lambda i: (i, 0))

    def resident(shape):
        return pl.BlockSpec((None,) + shape, lambda i: (layer,) + (0,) * len(shape),
                            pipeline_mode=pl.Buffered(1))

    vec = resident((1, D_MODEL))
    return pl.pallas_call(
        _mix_ffn_kernel,
        out_shape=jax.ShapeDtypeStruct((rows, D_MODEL), F32),
        grid=(rows // tm,),
        in_specs=[row(RETV_W), row(POOL_W), row(FOX_W), row(D_MODEL),
                  resident((MIX_W, D_MODEL)), vec, vec,
                  resident((D_MODEL, D_FF)), resident((D_MODEL, D_FF)), resident((D_FF, D_MODEL)), vec, vec],
        out_specs=row(D_MODEL),
        compiler_params=pltpu.CompilerParams(dimension_semantics=("parallel",),
                                             vmem_limit_bytes=VMEM_LIMIT),
        name="mix_ffn_ln",
    )(o_r, o_p, o_f, h, params["w_out"], params["ln1_g"], params["ln1_b"],
      params["w1"], params["w3"], params["w2"], params["ln2_g"], params["ln2_b"])


def _pack_w_in(w):
    d = w.shape[0]

    def rot_pack(seg):
        s = seg.reshape(d, RET_HEADS, RET_DK)
        x1 = s[:, :, :RET_HALF].reshape(d, RET_HEADS * RET_HALF)
        x2 = s[:, :, RET_HALF:].reshape(d, RET_HEADS * RET_HALF)
        z = jnp.zeros((d, LANES - RET_HEADS * RET_HALF), w.dtype)
        return jnp.concatenate([x1, z, x2, z], axis=1)

    def head_pad(seg):
        s = seg.reshape(d, RET_HEADS, RET_DV)
        return jnp.pad(s, ((0, 0), (0, 0), (0, RET_HEAD_PAD - RET_DV))).reshape(d, RETV_W)

    o = 0
    segs = []
    for sz in (RET_QK, RET_QK, RET_W, RET_W, POOL_W, FOX_W, FOX_W, FOX_W, FOX_HEADS):
        segs.append(w[:, o:o + sz])
        o += sz
    q_r, k_r, v_r, g_r, u_p, q_f, k_f, v_f, f_l = segs
    packed = jnp.concatenate(
        [rot_pack(q_r), rot_pack(k_r), head_pad(v_r), head_pad(g_r), u_p, q_f, k_f, v_f,
         jnp.pad(f_l, ((0, 0), (0, LANES - FOX_HEADS)))], axis=1)
    return packed.astype(BF16)


def _pack_w_out(w):
    d = w.shape[1]
    w_r = jnp.pad(w[:RET_W].reshape(RET_HEADS, RET_DV, d),
                  ((0, 0), (0, RET_HEAD_PAD - RET_DV), (0, 0))).reshape(RETV_W, d)
    return jnp.concatenate([w_r, w[RET_W:]], axis=0).astype(BF16)


def _pad_heads(vec):
    return jnp.pad(vec.reshape(RET_HEADS, RET_DV), ((0, 0), (0, RET_HEAD_PAD - RET_DV))).reshape(1, RETV_W)


def _block_diag(pw):
    g = len(POOL_WINDOWS)
    eye = jnp.eye(g, dtype=pw.dtype)
    return jnp.einsum("gij,gh->gihj", pw, eye).reshape(POOL_W, POOL_W).astype(BF16)


def _cast_kernel(x_ref, o_ref):
    o_ref[...] = x_ref[...].astype(o_ref.dtype)


def _to_bf16(w):
    depth, k, n = w.shape
    slab = SEQ_BLOCK
    spec = pl.BlockSpec((None, slab, n), lambda l, i: (l, i, 0))
    return pl.pallas_call(
        _cast_kernel,
        out_shape=jax.ShapeDtypeStruct(w.shape, BF16),
        grid=(depth, k // slab),
        in_specs=[spec],
        out_specs=spec,
        compiler_params=pltpu.CompilerParams(dimension_semantics=("parallel", "parallel")),
        name="weights_to_bf16",
    )(w)


def _prepare_params(w_in, b_f, ret_gn_g, pool_w, pool_scale, w_out, ln1_g, ln1_b,
                    w_ffn1, w_ffn3, w_ffn2, ln2_g, ln2_b):
    depth = w_in.shape[0]
    vec = lambda a: a.reshape(depth, 1, a.shape[-1])
    return dict(
        w_in=jax.vmap(_pack_w_in)(w_in),
        gn_g=jax.vmap(_pad_heads)(ret_gn_g),
        b_f=vec(jnp.pad(b_f, ((0, 0), (0, LANES - FOX_HEADS)))),
        pool_w=jax.vmap(_block_diag)(pool_w),
        pool_scale=vec(pool_scale),
        w_out=jax.vmap(_pack_w_out)(w_out),
        ln1_g=vec(ln1_g), ln1_b=vec(ln1_b), ln2_g=vec(ln2_g), ln2_b=vec(ln2_b),
        w1=_to_bf16(w_ffn1), w3=_to_bf16(w_ffn3), w2=_to_bf16(w_ffn2))


def _retention_tables():
    ch = SEQ_BLOCK
    gamma = (1.0 - 2.0 ** (-5.0 - np.arange(RET_HEADS, dtype=np.float32))).astype(np.float32)
    lg = np.log(gamma).astype(np.float32)
    i = np.arange(ch, dtype=np.float32)
    diff = i[:, None] - i[None, :]
    dm = np.where(diff >= 0, np.exp(lg[:, None, None] * np.maximum(diff, 0.0)), 0.0).astype(np.float32)
    xi = np.exp(lg[:, None] * (i + 1.0)).astype(np.float32)
    zeta = np.exp(lg[:, None] * (ch - 1.0 - i)).astype(np.float32)
    lane = np.arange(2 * LANES)
    within = lane % LANES
    lane_head = np.where(within < RET_HEADS * RET_HALF, within // RET_HALF, -1)
    xiq = np.zeros((ch, 2 * LANES), np.float32)
    zk = np.zeros((ch, 2 * LANES), np.float32)
    bm = np.zeros((2 * LANES, RETV_W), np.float32)
    for h in range(RET_HEADS):
        sel = lane_head == h
        xiq[:, sel] = xi[h][:, None]
        zk[:, sel] = zeta[h][:, None]
        bm[sel, h * LANES:(h + 1) * LANES] = 1.0
    dec = np.repeat(np.exp(lg * ch).astype(np.float32), LANES)[None, :]
    return dict(xiq=jnp.asarray(xiq), zk=jnp.asarray(zk), dm=jnp.asarray(dm),
                dec=jnp.asarray(dec), bm=jnp.asarray(bm))


def _rotary_tables(lp):
    pos = jnp.arange(lp, dtype=F32)
    inv_freq = ROPE_BASE ** (-jnp.arange(RET_HALF, dtype=F32) / RET_HALF)
    ang = pos[:, None] * inv_freq[None, :]
    pad = LANES - RET_HEADS * RET_HALF
    cos = jnp.pad(jnp.tile(jnp.cos(ang), (1, RET_HEADS)), ((0, 0), (0, pad)))
    sin = jnp.pad(jnp.tile(jnp.sin(ang), (1, RET_HEADS)), ((0, 0), (0, pad)))
    return cos, sin


def kernel(x, meta, ln_emb_g, ln_emb_b, w_in, b_f, ret_gn_g, pool_w, pool_scale, w_out, ln1_g, ln1_b,
           w_ffn1, w_ffn3, w_ffn2, ln2_g, ln2_b):
    bsz, seq, d = x.shape
    assert d == D_MODEL and seq % SEQ_BLOCK == 0
    depth = w_in.shape[0]
    assert depth == DEPTH
    lp = seq + SEQ_BLOCK
    rows = bsz * lp
    tm = ROW_TILE if rows % ROW_TILE == 0 else SEQ_BLOCK

    consts = _retention_tables()
    consts["cos"], consts["sin"] = _rotary_tables(lp)
    params = _prepare_params(w_in, b_f, ret_gn_g, pool_w, pool_scale, w_out, ln1_g, ln1_b,
                             w_ffn1, w_ffn3, w_ffn2, ln2_g, ln2_b)

    h, ret, qkv, flog = _embed_inproj(x, meta, ln_emb_g, ln_emb_b, params["w_in"], lp)
    for l in range(depth):
        if l > 0:
            ret, qkv, flog = _inproj(h, params["w_in"], l, tm)
        o_r, o_p, cum = _seq_mix(ret, flog, consts, params, l, bsz, lp)
        o_f = _fox(qkv, cum, bsz, lp)
        h = _mix_ffn(o_r, o_p, o_f, h, params, l, tm)
    return h.reshape(bsz, lp, d)[:, N_META:N_META + seq]
```

```python
import functools

import numpy as np
import jax
import jax.numpy as jnp
from jax import lax
from jax.experimental import pallas as pl
from jax.experimental.pallas import tpu as pltpu

F32 = jnp.float32
BF16 = jnp.bfloat16

D_MODEL = 1024
N_META = 16
RET_HEADS = 4
RET_DK = 48
RET_HALF = RET_DK // 2
RET_DV = 96
RET_QK = RET_HEADS * RET_DK
RET_W = RET_HEADS * RET_DV
POOL_WINDOWS = (2, 4, 8, 16)
POOL_GROUP = 64
POOL_W = len(POOL_WINDOWS) * POOL_GROUP
FOX_HEADS = 6
FOX_DH = 64
FOX_W = FOX_HEADS * FOX_DH
D_FF = 2816
ROPE_BASE = 10000.0
LN_EPS = 1e-5
NEG_INF = -1e30
DEPTH = 2
ALPHA = (2.0 * DEPTH) ** 0.25
LOG2E = 1.4426950408889634

LANES = 128
MXU_DIM = 256
SEQ_BLOCK = 256
ROW_TILE = 1024
VMEM_LIMIT = 56 * 1024 * 1024

RET_HEAD_PAD = LANES
QR_OFF = 0
KR_OFF = 2 * LANES
VR_OFF = 4 * LANES
GR_OFF = VR_OFF + RET_HEADS * RET_HEAD_PAD
UP_OFF = GR_OFF + RET_HEADS * RET_HEAD_PAD
QF_OFF = UP_OFF + POOL_W
KF_OFF = QF_OFF + FOX_W
VF_OFF = KF_OFF + FOX_W
FL_OFF = VF_OFF + FOX_W
RET_COLS = QF_OFF
FOX_COLS = 3 * FOX_W
N_PACK = FL_OFF + LANES
RETV_W = RET_HEADS * RET_HEAD_PAD
MIX_W = RETV_W + POOL_W + FOX_W
PROJ_CHUNK = 512
FF_CHUNK = 256
CAST_SLAB_ELEMS = 768 * 1024


def _layer_norm(x, g, b):
    mu = jnp.mean(x, axis=-1, keepdims=True)
    d = x - mu
    var = jnp.mean(d * d, axis=-1, keepdims=True)
    return d * lax.rsqrt(var + LN_EPS) * g + b


def _sigmoid(x):
    return 1.0 / (1.0 + jnp.exp(-x))


def _project(xb, w_ref, cs_ref, ret_ref, fox_ref, flog_ref):
    def cols(c0, width):
        r = jnp.dot(xb, w_ref[:, c0:c0 + width], preferred_element_type=F32)
        return r * cs_ref[:, c0:c0 + width]

    for c0 in range(0, RET_COLS, PROJ_CHUNK):
        width = min(PROJ_CHUNK, RET_COLS - c0)
        ret_ref[:, c0:c0 + width] = cols(c0, width).astype(BF16)
    for c0 in range(0, FOX_COLS - LANES, PROJ_CHUNK):
        fox_ref[:, c0:c0 + PROJ_CHUNK] = cols(QF_OFF + c0, PROJ_CHUNK).astype(BF16)
    tail = cols(FL_OFF - LANES, 2 * LANES)
    fox_ref[:, FOX_COLS - LANES:] = tail[:, :LANES].astype(BF16)
    flog_ref[...] = tail[:, LANES:]


def _embed_inproj_kernel(meta_ref, xm_ref, xe_ref, g_ref, b_ref, w_ref, cs_ref,
                         h_ref, ret_ref, fox_ref, flog_ref):
    j = pl.program_id(1)
    last = pl.num_programs(1) - 1
    top = jnp.where(j == 0, meta_ref[...], xe_ref[...])
    body = jnp.where(j == last, 0.0, xm_ref[0:SEQ_BLOCK - N_META, :])
    rows = jnp.concatenate([top, body], axis=0)
    h = _layer_norm(rows, g_ref[...], b_ref[...])
    h_ref[...] = h
    _project(h.astype(BF16), w_ref, cs_ref, ret_ref, fox_ref, flog_ref)


def _embed_inproj(x, meta, g, b, w, lp):
    bsz, seq, d = x.shape
    nblk = lp // SEQ_BLOCK
    n_xblk = seq // SEQ_BLOCK
    per = SEQ_BLOCK // N_META
    rows = bsz * lp
    out_row = lambda width: pl.BlockSpec((SEQ_BLOCK, width), lambda bb, j: (bb * nblk + j, 0))
    return pl.pallas_call(
        _embed_inproj_kernel,
        out_shape=(jax.ShapeDtypeStruct((rows, d), F32),
                   jax.ShapeDtypeStruct((rows, RET_COLS), BF16),
                   jax.ShapeDtypeStruct((rows, FOX_COLS), BF16),
                   jax.ShapeDtypeStruct((rows, LANES), F32)),
        grid=(bsz, nblk),
        in_specs=[
            pl.BlockSpec((N_META, d), lambda bb, j: (0, 0)),
            pl.BlockSpec((None, SEQ_BLOCK, d), lambda bb, j: (bb, jnp.minimum(j, n_xblk - 1), 0)),
            pl.BlockSpec((None, N_META, d), lambda bb, j: (bb, jnp.maximum(per * j - 1, 0), 0)),
            pl.BlockSpec((1, d), lambda bb, j: (0, 0)),
            pl.BlockSpec((1, d), lambda bb, j: (0, 0)),
            pl.BlockSpec((None, D_MODEL, N_PACK), lambda bb, j: (0, 0, 0)),
            pl.BlockSpec((1, N_PACK), lambda bb, j: (0, 0)),
        ],
        out_specs=(out_row(d), out_row(RET_COLS), out_row(FOX_COLS), out_row(LANES)),
        compiler_params=pltpu.CompilerParams(dimension_semantics=("parallel", "arbitrary"),
                                             vmem_limit_bytes=VMEM_LIMIT),
        name="embed_in_proj",
    )(meta, x, x, g.reshape(1, d), b.reshape(1, d), w, _inproj_col_scale())


def _inproj_kernel(h_ref, w_ref, cs_ref, ret_ref, fox_ref, flog_ref):
    _project(h_ref[...].astype(BF16), w_ref, cs_ref, ret_ref, fox_ref, flog_ref)


def _inproj_col_scale():
    cs = np.ones((1, N_PACK), np.float32)
    cs[0, KR_OFF:KR_OFF + 2 * LANES] = RET_DK ** -0.5
    cs[0, QF_OFF:QF_OFF + FOX_W] = FOX_DH ** -0.5 * LOG2E
    return jnp.asarray(cs)


def _inproj(h, w, layer, tm):
    rows = h.shape[0]
    return pl.pallas_call(
        _inproj_kernel,
        out_shape=(jax.ShapeDtypeStruct((rows, RET_COLS), BF16),
                   jax.ShapeDtypeStruct((rows, FOX_COLS), BF16),
                   jax.ShapeDtypeStruct((rows, LANES), F32)),
        grid=(rows // tm,),
        in_specs=[
            pl.BlockSpec((tm, D_MODEL), lambda i: (i, 0)),
            pl.BlockSpec((None, D_MODEL, N_PACK), lambda i: (layer, 0, 0)),
            pl.BlockSpec((1, N_PACK), lambda i: (0, 0)),
        ],
        out_specs=(pl.BlockSpec((tm, RET_COLS), lambda i: (i, 0)),
                   pl.BlockSpec((tm, FOX_COLS), lambda i: (i, 0)),
                   pl.BlockSpec((tm, LANES), lambda i: (i, 0))),
        compiler_params=pltpu.CompilerParams(dimension_semantics=("parallel",),
                                             vmem_limit_bytes=VMEM_LIMIT),
        name="in_proj",
    )(h, w, _inproj_col_scale())


def _seq_kernel(qk_ref, v_ref, g_ref, u_ref, fl_ref, cos_ref, sin_ref,
                xiq_ref, zk_ref, dm_ref, dec_ref, bm_ref, gng_ref, bf_ref, pw_ref, ps_ref,
                or_ref, op_ref, c_ref,
                state_sc, tail_sc, carry_sc):
    j = pl.program_id(1)
    ch = SEQ_BLOCK

    @pl.when(j == 0)
    def _():
        state_sc[...] = jnp.zeros_like(state_sc)
        tail_sc[...] = jnp.zeros_like(tail_sc)
        carry_sc[...] = jnp.zeros_like(carry_sc)

    qk = qk_ref[...].astype(F32)
    cs = cos_ref[...]
    sn = sin_ref[...]
    q1, q2 = qk[:, 0:LANES], qk[:, LANES:2 * LANES]
    k1, k2 = qk[:, 2 * LANES:3 * LANES], qk[:, 3 * LANES:4 * LANES]
    qr = jnp.concatenate([q1 * cs - q2 * sn, q1 * sn + q2 * cs], axis=-1)
    kr = jnp.concatenate([k1 * cs - k2 * sn, k1 * sn + k2 * cs], axis=-1)
    qb = qr.astype(BF16)
    qx = (qr * xiq_ref[...]).astype(BF16)
    kb = kr.astype(BF16)
    kz = (kr * zk_ref[...]).astype(BF16)
    v = v_ref[...]
    st = state_sc[...]
    cross = jnp.dot(qx, st.astype(BF16), preferred_element_type=F32)
    qlane = lax.broadcasted_iota(jnp.int32, (1, 2 * LANES), 1)
    qhead = jnp.where(qlane % LANES < RET_HEADS * RET_HALF, (qlane % LANES) // RET_HALF, RET_HEADS)
    inner = []
    for h in range(RET_HEADS):
        qh = jnp.where(qhead == h, qb, jnp.zeros_like(qb))
        s = lax.dot_general(qh, kb, (((1,), (1,)), ((), ())), preferred_element_type=F32)
        p = (s * dm_ref[h]).astype(BF16)
        inner.append(jnp.dot(p, v[:, h * LANES:(h + 1) * LANES], preferred_element_type=F32))
    o = jnp.concatenate(inner, axis=-1) + cross
    kv = lax.dot_general(kz, v, (((0,), (0,)), ((), ())), preferred_element_type=F32)
    state_sc[...] = st * dec_ref[...] + kv * bm_ref[...]

    vlane = lax.broadcasted_iota(jnp.int32, (1, LANES), 1) < RET_DV
    normed = []
    for h in range(RET_HEADS):
        xh = o[:, h * LANES:(h + 1) * LANES]
        mu = jnp.sum(xh, axis=-1, keepdims=True) * (1.0 / RET_DV)
        d = jnp.where(vlane, xh - mu, 0.0)
        var = jnp.sum(d * d, axis=-1, keepdims=True) * (1.0 / RET_DV)
        normed.append(d * lax.rsqrt(var + LN_EPS))
    y = jnp.concatenate(normed, axis=-1) * gng_ref[...]
    gate = g_ref[...].astype(F32)
    or_ref[...] = (gate * _sigmoid(gate) * y).astype(BF16)

    u = u_ref[...].astype(F32)
    tail_rows = tail_sc.shape[0]
    ext = jnp.concatenate([tail_sc[...], u], axis=0)
    tail_sc[...] = u[ch - tail_rows:, :]
    e2 = ext + pltpu.roll(ext, 1, 0)
    e4 = e2 + pltpu.roll(e2, 2, 0)
    e8 = e4 + pltpu.roll(e4, 4, 0)
    e16 = e8 + pltpu.roll(e8, 8, 0)
    glane = lax.broadcasted_iota(jnp.int32, (1, POOL_W), 1) // POOL_GROUP
    win = jnp.where(glane == 0, e2, jnp.where(glane == 1, e4, jnp.where(glane == 2, e8, e16)))
    win = win[tail_rows:, :]
    wlen = jnp.where(glane == 0, 2, jnp.where(glane == 1, 4, jnp.where(glane == 2, 8, 16)))
    pos = j * ch + lax.broadcasted_iota(jnp.int32, (ch, POOL_W), 0)
    cnt = jnp.minimum(pos + 1, wlen).astype(F32)
    pooled = (win / cnt - u).astype(BF16)
    yp = jnp.dot(pooled, pw_ref[...], preferred_element_type=F32) * ps_ref[...]
    op_ref[...] = yp.astype(BF16)

    z = fl_ref[...] + bf_ref[...]
    logf = jnp.minimum(z, 0.0) - jnp.log1p(jnp.exp(-jnp.abs(z)))
    row = lax.broadcasted_iota(jnp.int32, (ch, LANES), 0)
    sh = 1
    while sh < ch:
        logf = logf + jnp.where(row >= sh, pltpu.roll(logf, sh, 0), 0.0)
        sh *= 2
    c = logf + carry_sc[0:1, :]
    carry_sc[...] = jnp.broadcast_to(c[ch - 1:ch, :], carry_sc.shape)
    c_ref[...] = c * LOG2E


def _seq_mix(proj, flog, consts, params, layer, bsz, lp):
    ch = SEQ_BLOCK
    nc = lp // ch
    rows = bsz * lp

    def rowblk(width, colblk):
        return pl.BlockSpec((ch, width), lambda bb, j: (bb * nc + j, colblk))

    def const(shape):
        nd = len(shape)
        return pl.BlockSpec(shape, lambda bb, j: (0,) * nd)

    def layer_param(shape):
        return pl.BlockSpec((None,) + shape, lambda bb, j: (layer,) + (0,) * len(shape))

    in_specs = [
        rowblk(4 * LANES, QR_OFF // (4 * LANES)),
        rowblk(RETV_W, VR_OFF // RETV_W),
        rowblk(RETV_W, GR_OFF // RETV_W),
        rowblk(POOL_W, UP_OFF // POOL_W),
        pl.BlockSpec((ch, LANES), lambda bb, j: (bb * nc + j, 0)),
        pl.BlockSpec((ch, LANES), lambda bb, j: (j, 0)),
        pl.BlockSpec((ch, LANES), lambda bb, j: (j, 0)),
        const((ch, 2 * LANES)), const((ch, 2 * LANES)), const((RET_HEADS, ch, ch)),
        const((1, RETV_W)), const((2 * LANES, RETV_W)),
        layer_param((1, RETV_W)), layer_param((1, LANES)), layer_param((POOL_W, POOL_W)),
        layer_param((1, POOL_W)),
    ]
    out_shape = (jax.ShapeDtypeStruct((rows, RETV_W), BF16),
                 jax.ShapeDtypeStruct((rows, POOL_W), BF16),
                 jax.ShapeDtypeStruct((rows, LANES), F32))
    out_specs = (pl.BlockSpec((ch, RETV_W), lambda bb, j: (bb * nc + j, 0)),
                 pl.BlockSpec((ch, POOL_W), lambda bb, j: (bb * nc + j, 0)),
                 pl.BlockSpec((ch, LANES), lambda bb, j: (bb * nc + j, 0)))
    return pl.pallas_call(
        _seq_kernel,
        out_shape=out_shape,
        grid=(bsz, nc),
        in_specs=in_specs,
        out_specs=out_specs,
        scratch_shapes=[pltpu.VMEM((2 * LANES, RETV_W), F32),
                        pltpu.VMEM((16, POOL_W), F32),
                        pltpu.VMEM((8, LANES), F32)],
        compiler_params=pltpu.CompilerParams(dimension_semantics=("parallel", "arbitrary"),
                                             vmem_limit_bytes=VMEM_LIMIT),
        name="seq_mixers",
    )(proj, proj, proj, proj, flog, consts["cos"], consts["sin"],
      consts["xiq"], consts["zk"], consts["dm"], consts["dec"], consts["bm"],
      params["gn_g"], params["b_f"], params["pool_w"], params["pool_scale"])


FOX_AUG = 3
FOX_VROWS = FOX_DH + 16
FOX_LATE_HEADS = 2
FOX_UNROLL = 8


def _fox_select_matrix():
    sel = np.zeros((LANES, FOX_HEADS * LANES), np.float32)
    for a in range(FOX_AUG):
        for h in range(FOX_HEADS):
            sel[8 * a + h, h * LANES + FOX_DH + a] = -1.0
    return jnp.asarray(sel, BF16)


def _fox_kernel(q_ref, k_ref, v_ref, c_ref, sel_ref, o_ref,
                ka_sc, vt_sc, qa_sc, st_sc, m_sc, acc_sc):
    i = pl.program_id(1)
    tq = SEQ_BLOCK
    tk = SEQ_BLOCK
    npair = FOX_HEADS // 2
    lane = lax.broadcasted_iota(jnp.int32, (1, LANES), 1)

    @pl.when(i == 0)
    def _():
        def chunk(t, carry):
            r0 = pl.multiple_of(t * tk, tk)
            cc = c_ref[pl.ds(r0, tk), :]
            pieces, rem = [], cc
            for a in range(FOX_AUG):
                piece = rem.astype(BF16).astype(F32)
                pieces.append(piece if a == 0 else pltpu.roll(piece, 8 * a, 1))
                rem = rem - piece
            packed = jnp.where(lane < 8, pieces[0], jnp.where(lane < 16, pieces[1], pieces[2]))
            c_aug = jnp.dot(packed.astype(BF16), sel_ref[...], preferred_element_type=F32)
            for pr in range(npair):
                kk = k_ref[pl.ds(r0, tk), pr * LANES:(pr + 1) * LANES].astype(F32)
                vv = v_ref[pl.ds(r0, tk), pr * LANES:(pr + 1) * LANES].astype(F32)
                vtt = vv.T
                ones = jnp.ones((FOX_VROWS - FOX_DH, tk), F32)
                k_heads = (kk, pltpu.roll(kk, FOX_DH, 1))
                v_heads = (vtt[:FOX_DH], vtt[FOX_DH:])
                for hh in range(2):
                    h = 2 * pr + hh
                    aug = jnp.where(lane < FOX_DH, k_heads[hh], c_aug[:, h * LANES:(h + 1) * LANES])
                    ka_sc[h, pl.ds(r0, tk), :] = aug.astype(BF16)
                    vt_sc[h, :, pl.ds(r0, tk)] = jnp.concatenate([v_heads[hh], ones], axis=0).astype(BF16)
            return carry

        lax.fori_loop(0, k_ref.shape[0] // tk, chunk, 0)

    for pr in range(npair):
        qq = q_ref[:, pr * LANES:(pr + 1) * LANES].astype(F32)
        tail = jnp.where(lane < FOX_DH + FOX_AUG, 1.0, 0.0)
        qa_sc[2 * pr] = jnp.where(lane < FOX_DH, qq, tail).astype(BF16)
        qa_sc[2 * pr + 1] = jnp.where(lane < FOX_DH, pltpu.roll(qq, FOX_DH, 1), tail).astype(BF16)
    m_sc[...] = jnp.full(m_sc.shape, NEG_INF, F32)
    acc_sc[...] = jnp.zeros_like(acc_sc)

    def score(t, h):
        ks = t * tk if isinstance(t, int) else pl.multiple_of(t * tk, tk)
        st_sc[h] = lax.dot_general(ka_sc[h, pl.ds(ks, tk), :], qa_sc[h], (((1,), (1,)), ((), ())),
                                   preferred_element_type=F32)

    def attend(t, masked, next_t):
        ks = t * tk if isinstance(t, int) else pl.multiple_of(t * tk, tk)

        def load(h, c0):
            s = st_sc[h, :, c0:c0 + LANES]
            if masked:
                k_id = lax.broadcasted_iota(jnp.int32, (tk, LANES), 0)
                q_id = c0 + lax.broadcasted_iota(jnp.int32, (tk, LANES), 1)
                s = jnp.where(k_id <= q_id, s, NEG_INF)
            return s

        for h in range(early, FOX_HEADS):
            score(t, h)
        for h in range(FOX_HEADS):
            p_halves, a_halves = [], []
            for c0 in range(0, tq, LANES):
                m_prev = m_sc[h, :, c0:c0 + LANES]
                m_new = jnp.maximum(m_prev, jnp.max(load(h, c0), axis=0, keepdims=True))
                m_sc[h, :, c0:c0 + LANES] = m_new
                a_halves.append(jnp.exp2(m_prev - m_new))
                p_halves.append(jnp.exp2(load(h, c0) - m_new).astype(BF16))
            p_t = jnp.concatenate(p_halves, axis=1)
            pv = jnp.dot(vt_sc[h, :, pl.ds(ks, tk)], p_t, preferred_element_type=F32)
            acc_sc[h] = jnp.concatenate(a_halves, axis=1) * acc_sc[h] + pv
            if next_t is not None and h < early:
                score(next_t, h)

    early = FOX_HEADS - FOX_LATE_HEADS
    for h in range(early):
        score(0, h)

    def run(t0, count):
        for d in range(count):
            attend(t0 + d, False, t0 + d + 1)

    def body(u, carry):
        run(FOX_UNROLL * u, FOX_UNROLL)
        return carry

    lax.fori_loop(0, i // FOX_UNROLL, body, 0)
    done = (i // FOX_UNROLL) * FOX_UNROLL
    span = FOX_UNROLL // 2
    while span >= 1:
        take = (i - done) >= span

        @pl.when(take)
        def _(done=done, span=span):
            run(done, span)

        done = done + jnp.where(take, span, 0)
        span //= 2

    attend(i, True, None)

    outs = []
    for pr in range(npair):
        acc_a = acc_sc[2 * pr]
        acc_b = acc_sc[2 * pr + 1]
        o_t = jnp.concatenate([acc_a[:FOX_DH] / acc_a[FOX_DH:FOX_DH + 1],
                               acc_b[:FOX_DH] / acc_b[FOX_DH:FOX_DH + 1]], axis=0)
        outs.append(o_t.T)
    o_ref[...] = jnp.concatenate(outs, axis=-1).astype(BF16)


def _fox(qkv, cum, bsz, lp):
    tq = SEQ_BLOCK
    nq = lp // tq
    rows = bsz * lp
    return pl.pallas_call(
        _fox_kernel,
        out_shape=jax.ShapeDtypeStruct((rows, FOX_W), BF16),
        grid=(bsz, nq),
        in_specs=[
            pl.BlockSpec((tq, FOX_W), lambda bb, i: (bb * nq + i, 0)),
            pl.BlockSpec((lp, FOX_W), lambda bb, i: (bb, 1)),
            pl.BlockSpec((lp, FOX_W), lambda bb, i: (bb, 2)),
            pl.BlockSpec((lp, LANES), lambda bb, i: (bb, 0)),
            pl.BlockSpec((LANES, FOX_HEADS * LANES), lambda bb, i: (0, 0)),
        ],
        out_specs=pl.BlockSpec((tq, FOX_W), lambda bb, i: (bb * nq + i, 0)),
        scratch_shapes=[pltpu.VMEM((FOX_HEADS, lp, LANES), BF16),
                        pltpu.VMEM((FOX_HEADS, FOX_VROWS, lp), BF16),
                        pltpu.VMEM((FOX_HEADS, tq, LANES), BF16),
                        pltpu.VMEM((FOX_HEADS, SEQ_BLOCK, tq), F32),
                        pltpu.VMEM((FOX_HEADS, 1, tq), F32),
                        pltpu.VMEM((FOX_HEADS, FOX_VROWS, tq), F32)],
        compiler_params=pltpu.CompilerParams(
            dimension_semantics=("parallel", "arbitrary"),
            vmem_limit_bytes=VMEM_LIMIT),
        name="fox_attention",
    )(qkv, qkv, qkv, cum, _fox_select_matrix())


def _mix_ffn_kernel(or_ref, op_ref, of_ref, h_ref, wo_ref, g1_ref, b1_ref,
                    w1_ref, w3_ref, w2_ref, g2_ref, b2_ref, o_ref):
    mix = jnp.concatenate([or_ref[...], op_ref[...], of_ref[...]], axis=-1)
    y = ALPHA * h_ref[...] + jnp.dot(mix, wo_ref[...], preferred_element_type=F32)
    h1 = _layer_norm(y, g1_ref[...], b1_ref[...])
    xb = h1.astype(BF16)
    acc = None
    for c0 in range(0, D_FF, FF_CHUNK):
        a = jnp.dot(xb, w1_ref[:, c0:c0 + FF_CHUNK], preferred_element_type=F32)
        b = jnp.dot(xb, w3_ref[:, c0:c0 + FF_CHUNK], preferred_element_type=F32)
        t = (a * _sigmoid(a) * b).astype(BF16)
        part = jnp.dot(t, w2_ref[c0:c0 + FF_CHUNK, :], preferred_element_type=F32)
        acc = part if acc is None else acc + part
    o_ref[...] = _layer_norm(ALPHA * h1 + acc, g2_ref[...], b2_ref[...])


def _mix_ffn(o_r, o_p, o_f, h, params, layer, tm):
    rows = h.shape[0]

    def row(width):
        return pl.BlockSpec((tm, width), lambda i: (i, 0))

    def resident(shape):
        return pl.BlockSpec((None,) + shape, lambda i: (layer,) + (0,) * len(shape),
                            pipeline_mode=pl.Buffered(1))

    vec = resident((1, D_MODEL))
    return pl.pallas_call(
        _mix_ffn_kernel,
        out_shape=jax.ShapeDtypeStruct((rows, D_MODEL), F32),
        grid=(rows // tm,),
        in_specs=[row(RETV_W), row(POOL_W), row(FOX_W), row(D_MODEL),
                  resident((MIX_W, D_MODEL)), vec, vec,
                  resident((D_MODEL, D_FF)), resident((D_MODEL, D_FF)), resident((D_FF, D_MODEL)), vec, vec],
        out_specs=row(D_MODEL),
        compiler_params=pltpu.CompilerParams(dimension_semantics=("parallel",),
                                             vmem_limit_bytes=VMEM_LIMIT),
        name="mix_ffn_ln",
    )(o_r, o_p, o_f, h, params["w_out"], params["ln1_g"], params["ln1_b"],
      params["w1"], params["w3"], params["w2"], params["ln2_g"], params["ln2_b"])


def _pack_w_in(w):
    lead = w.shape[:-1]

    def rot_pack(seg):
        s = seg.reshape(lead + (RET_HEADS, RET_DK))
        x1 = s[..., :RET_HALF].reshape(lead + (RET_HEADS * RET_HALF,))
        x2 = s[..., RET_HALF:].reshape(lead + (RET_HEADS * RET_HALF,))
        z = jnp.zeros(lead + (LANES - RET_HEADS * RET_HALF,), w.dtype)
        return jnp.concatenate([x1, z, x2, z], axis=-1)

    def head_pad(seg):
        s = seg.reshape(lead + (RET_HEADS, RET_DV))
        widths = [(0, 0)] * len(lead) + [(0, 0), (0, RET_HEAD_PAD - RET_DV)]
        return jnp.pad(s, widths).reshape(lead + (RETV_W,))

    o = 0
    segs = []
    for sz in (RET_QK, RET_QK, RET_W, RET_W, POOL_W, FOX_W, FOX_W, FOX_W, FOX_HEADS):
        segs.append(w[..., o:o + sz])
        o += sz
    q_r, k_r, v_r, g_r, u_p, q_f, k_f, v_f, f_l = segs
    f_pad = jnp.pad(f_l, [(0, 0)] * len(lead) + [(0, LANES - FOX_HEADS)])
    packed = jnp.concatenate(
        [rot_pack(q_r), rot_pack(k_r), head_pad(v_r), head_pad(g_r), u_p, q_f, k_f, v_f, f_pad], axis=-1)
    return packed.astype(BF16)


def _pack_w_out(w):
    depth, _, d = w.shape
    w_r = jnp.pad(w[:, :RET_W].reshape(depth, RET_HEADS, RET_DV, d),
                  ((0, 0), (0, 0), (0, RET_HEAD_PAD - RET_DV), (0, 0))).reshape(depth, RETV_W, d)
    return jnp.concatenate([w_r, w[:, RET_W:]], axis=1).astype(BF16)


def _pad_heads(vec):
    depth = vec.shape[0]
    padded = jnp.pad(vec.reshape(depth, RET_HEADS, RET_DV), ((0, 0), (0, 0), (0, RET_HEAD_PAD - RET_DV)))
    return padded.reshape(depth, 1, RETV_W)


def _block_diag(pw):
    g = len(POOL_WINDOWS)
    eye = jnp.eye(g, dtype=pw.dtype)
    return jnp.einsum("lgij,gh->lgihj", pw, eye).reshape(pw.shape[0], POOL_W, POOL_W).astype(BF16)


def _cast_kernel(x_ref, o_ref):
    o_ref[...] = x_ref[...].astype(o_ref.dtype)


def _to_bf16(w):
    depth, k, n = w.shape
    slab = max(s for s in range(16, k + 1, 16) if k % s == 0 and s * n <= CAST_SLAB_ELEMS)
    spec = pl.BlockSpec((None, slab, n), lambda l, i: (l, i, 0))
    return pl.pallas_call(
        _cast_kernel,
        out_shape=jax.ShapeDtypeStruct(w.shape, BF16),
        grid=(depth, k // slab),
        in_specs=[spec],
        out_specs=spec,
        compiler_params=pltpu.CompilerParams(dimension_semantics=("parallel", "parallel")),
        name="weights_to_bf16",
    )(w)


def _prepare_params(w_in, b_f, ret_gn_g, pool_w, pool_scale, w_out, ln1_g, ln1_b,
                    w_ffn1, w_ffn3, w_ffn2, ln2_g, ln2_b):
    depth = w_in.shape[0]
    vec = lambda a: a.reshape(depth, 1, a.shape[-1])
    return dict(
        w_in=_pack_w_in(w_in),
        gn_g=_pad_heads(ret_gn_g),
        b_f=vec(jnp.pad(b_f, ((0, 0), (0, LANES - FOX_HEADS)))),
        pool_w=_block_diag(pool_w),
        pool_scale=vec(pool_scale),
        w_out=_pack_w_out(w_out),
        ln1_g=vec(ln1_g), ln1_b=vec(ln1_b), ln2_g=vec(ln2_g), ln2_b=vec(ln2_b),
        w1=_to_bf16(w_ffn1), w3=_to_bf16(w_ffn3), w2=_to_bf16(w_ffn2))


def _retention_tables():
    ch = SEQ_BLOCK
    gamma = (1.0 - 2.0 ** (-5.0 - np.arange(RET_HEADS, dtype=np.float32))).astype(np.float32)
    lg = np.log(gamma).astype(np.float32)
    i = np.arange(ch, dtype=np.float32)
    diff = i[:, None] - i[None, :]
    dm = np.where(diff >= 0, np.exp(lg[:, None, None] * np.maximum(diff, 0.0)), 0.0).astype(np.float32)
    xi = np.exp(lg[:, None] * (i + 1.0)).astype(np.float32)
    zeta = np.exp(lg[:, None] * (ch - 1.0 - i)).astype(np.float32)
    lane = np.arange(2 * LANES)
    within = lane % LANES
    lane_head = np.where(within < RET_HEADS * RET_HALF, within // RET_HALF, -1)
    xiq = np.zeros((ch, 2 * LANES), np.float32)
    zk = np.zeros((ch, 2 * LANES), np.float32)
    bm = np.zeros((2 * LANES, RETV_W), np.float32)
    for h in range(RET_HEADS):
        sel = lane_head == h
        xiq[:, sel] = xi[h][:, None]
        zk[:, sel] = zeta[h][:, None]
        bm[sel, h * LANES:(h + 1) * LANES] = 1.0
    dec = np.repeat(np.exp(lg * ch).astype(np.float32), LANES)[None, :]
    return dict(xiq=jnp.asarray(xiq), zk=jnp.asarray(zk), dm=jnp.asarray(dm),
                dec=jnp.asarray(dec), bm=jnp.asarray(bm))


def _rotary_tables(lp):
    pos = jnp.arange(lp, dtype=F32)
    inv_freq = ROPE_BASE ** (-jnp.arange(RET_HALF, dtype=F32) / RET_HALF)
    ang = pos[:, None] * inv_freq[None, :]
    pad = LANES - RET_HEADS * RET_HALF
    cos = jnp.pad(jnp.tile(jnp.cos(ang), (1, RET_HEADS)), ((0, 0), (0, pad)))
    sin = jnp.pad(jnp.tile(jnp.sin(ang), (1, RET_HEADS)), ((0, 0), (0, pad)))
    return cos, sin


def kernel(x, meta, ln_emb_g, ln_emb_b, w_in, b_f, ret_gn_g, pool_w, pool_scale, w_out, ln1_g, ln1_b,
           w_ffn1, w_ffn3, w_ffn2, ln2_g, ln2_b):
    bsz, seq, d = x.shape
    assert d == D_MODEL and seq % SEQ_BLOCK == 0
    depth = w_in.shape[0]
    assert depth == DEPTH
    lp = seq + SEQ_BLOCK
    rows = bsz * lp
    tm = ROW_TILE if rows % ROW_TILE == 0 else SEQ_BLOCK

    consts = _retention_tables()
    consts["cos"], consts["sin"] = _rotary_tables(lp)
    params = _prepare_params(w_in, b_f, ret_gn_g, pool_w, pool_scale, w_out, ln1_g, ln1_b,
                             w_ffn1, w_ffn3, w_ffn2, ln2_g, ln2_b)

    h, ret, qkv, flog = _embed_inproj(x, meta, ln_emb_g, ln_emb_b, params["w_in"], lp)
    for l in range(depth):
        if l > 0:
            ret, qkv, flog = _inproj(h, params["w_in"], l, tm)
        o_r, o_p, cum = _seq_mix(ret, flog, consts, params, l, bsz, lp)
        o_f = _fox(qkv, cum, bsz, lp)
        h = _mix_ffn(o_r, o_p, o_f, h, params, l, tm)
    return h.reshape(bsz, lp, d)[:, N_META:N_META + seq]
```

```python
import functools

import numpy as np
import jax
import jax.numpy as jnp
from jax import lax
from jax.experimental import pallas as pl
from jax.experimental.pallas import tpu as pltpu

F32 = jnp.float32
BF16 = jnp.bfloat16

D_MODEL = 1024
N_META = 16
RET_HEADS = 4
RET_DK = 48
RET_HALF = RET_DK // 2
RET_DV = 96
RET_QK = RET_HEADS * RET_DK
RET_W = RET_HEADS * RET_DV
POOL_WINDOWS = (2, 4, 8, 16)
POOL_GROUP = 64
POOL_W = len(POOL_WINDOWS) * POOL_GROUP
FOX_HEADS = 6
FOX_DH = 64
FOX_W = FOX_HEADS * FOX_DH
D_FF = 2816
ROPE_BASE = 10000.0
LN_EPS = 1e-5
NEG_INF = -1e30
DEPTH = 2
ALPHA = (2.0 * DEPTH) ** 0.25
LOG2E = 1.4426950408889634

LANES = 128
MXU_DIM = 256
SEQ_BLOCK = 256
ROW_TILE = 1024
VMEM_LIMIT = 56 * 1024 * 1024

RET_HEAD_PAD = LANES
QR_OFF = 0
KR_OFF = 2 * LANES
VR_OFF = 4 * LANES
GR_OFF = VR_OFF + RET_HEADS * RET_HEAD_PAD
UP_OFF = GR_OFF + RET_HEADS * RET_HEAD_PAD
QF_OFF = UP_OFF + POOL_W
KF_OFF = QF_OFF + FOX_W
VF_OFF = KF_OFF + FOX_W
FL_OFF = VF_OFF + FOX_W
RET_COLS = QF_OFF
FOX_COLS = 3 * FOX_W
N_PACK = FL_OFF + LANES
RETV_W = RET_HEADS * RET_HEAD_PAD
MIX_W = RETV_W + POOL_W + FOX_W
PROJ_CHUNK = 512
FF_CHUNK = 256
CAST_SLAB_ELEMS = 768 * 1024
MIX_FFN_SPLIT = 2


def _layer_norm(x, g, b):
    mu = jnp.mean(x, axis=-1, keepdims=True)
    d = x - mu
    var = jnp.mean(d * d, axis=-1, keepdims=True)
    return d * lax.rsqrt(var + LN_EPS) * g + b


def _sigmoid(x):
    return 1.0 / (1.0 + jnp.exp(-x))


def _project(xb, w_ref, cs_ref, ret_ref, fox_ref, flog_ref):
    def cols(c0, width):
        r = jnp.dot(xb, w_ref[:, c0:c0 + width], preferred_element_type=F32)
        return r * cs_ref[:, c0:c0 + width]

    for c0 in range(0, RET_COLS, PROJ_CHUNK):
        width = min(PROJ_CHUNK, RET_COLS - c0)
        ret_ref[:, c0:c0 + width] = cols(c0, width).astype(BF16)
    for c0 in range(0, FOX_COLS - LANES, PROJ_CHUNK):
        fox_ref[:, c0:c0 + PROJ_CHUNK] = cols(QF_OFF + c0, PROJ_CHUNK).astype(BF16)
    tail = cols(FL_OFF - LANES, 2 * LANES)
    fox_ref[:, FOX_COLS - LANES:] = tail[:, :LANES].astype(BF16)
    flog_ref[...] = tail[:, LANES:]


def _embed_inproj_kernel(meta_ref, xm_ref, xe_ref, g_ref, b_ref, w_ref, cs_ref,
                         h_ref, ret_ref, fox_ref, flog_ref):
    j = pl.program_id(1)
    last = pl.num_programs(1) - 1
    top = jnp.where(j == 0, meta_ref[...], xe_ref[...])
    body = jnp.where(j == last, 0.0, xm_ref[0:SEQ_BLOCK - N_META, :])
    rows = jnp.concatenate([top, body], axis=0)
    h = _layer_norm(rows, g_ref[...], b_ref[...])
    h_ref[...] = h
    _project(h.astype(BF16), w_ref, cs_ref, ret_ref, fox_ref, flog_ref)


def _embed_inproj(x, meta, g, b, w, lp):
    bsz, seq, d = x.shape
    nblk = lp // SEQ_BLOCK
    n_xblk = seq // SEQ_BLOCK
    per = SEQ_BLOCK // N_META
    rows = bsz * lp
    out_row = lambda width: pl.BlockSpec((SEQ_BLOCK, width), lambda bb, j: (bb * nblk + j, 0))
    return pl.pallas_call(
        _embed_inproj_kernel,
        out_shape=(jax.ShapeDtypeStruct((rows, d), F32),
                   jax.ShapeDtypeStruct((rows, RET_COLS), BF16),
                   jax.ShapeDtypeStruct((rows, FOX_COLS), BF16),
                   jax.ShapeDtypeStruct((rows, LANES), F32)),
        grid=(bsz, nblk),
        in_specs=[
            pl.BlockSpec((N_META, d), lambda bb, j: (0, 0)),
            pl.BlockSpec((None, SEQ_BLOCK, d), lambda bb, j: (bb, jnp.minimum(j, n_xblk - 1), 0)),
            pl.BlockSpec((None, N_META, d), lambda bb, j: (bb, jnp.maximum(per * j - 1, 0), 0)),
            pl.BlockSpec((1, d), lambda bb, j: (0, 0)),
            pl.BlockSpec((1, d), lambda bb, j: (0, 0)),
            pl.BlockSpec((None, D_MODEL, N_PACK), lambda bb, j: (0, 0, 0)),
            pl.BlockSpec((1, N_PACK), lambda bb, j: (0, 0)),
        ],
        out_specs=(out_row(d), out_row(RET_COLS), out_row(FOX_COLS), out_row(LANES)),
        compiler_params=pltpu.CompilerParams(dimension_semantics=("parallel", "arbitrary"),
                                             vmem_limit_bytes=VMEM_LIMIT),
        name="embed_in_proj",
    )(meta, x, x, g.reshape(1, d), b.reshape(1, d), w, _inproj_col_scale())


def _inproj_kernel(h_ref, w_ref, cs_ref, ret_ref, fox_ref, flog_ref):
    _project(h_ref[...].astype(BF16), w_ref, cs_ref, ret_ref, fox_ref, flog_ref)


def _inproj_col_scale():
    cs = np.ones((1, N_PACK), np.float32)
    cs[0, KR_OFF:KR_OFF + 2 * LANES] = RET_DK ** -0.5
    cs[0, QF_OFF:QF_OFF + FOX_W] = FOX_DH ** -0.5 * LOG2E
    return jnp.asarray(cs)


def _inproj(h, w, layer, tm):
    rows = h.shape[0]
    return pl.pallas_call(
        _inproj_kernel,
        out_shape=(jax.ShapeDtypeStruct((rows, RET_COLS), BF16),
                   jax.ShapeDtypeStruct((rows, FOX_COLS), BF16),
                   jax.ShapeDtypeStruct((rows, LANES), F32)),
        grid=(rows // tm,),
        in_specs=[
            pl.BlockSpec((tm, D_MODEL), lambda i: (i, 0)),
            pl.BlockSpec((None, D_MODEL, N_PACK), lambda i: (layer, 0, 0)),
            pl.BlockSpec((1, N_PACK), lambda i: (0, 0)),
        ],
        out_specs=(pl.BlockSpec((tm, RET_COLS), lambda i: (i, 0)),
                   pl.BlockSpec((tm, FOX_COLS), lambda i: (i, 0)),
                   pl.BlockSpec((tm, LANES), lambda i: (i, 0))),
        compiler_params=pltpu.CompilerParams(dimension_semantics=("parallel",),
                                             vmem_limit_bytes=VMEM_LIMIT),
        name="in_proj",
    )(h, w, _inproj_col_scale())


def _seq_kernel(qk_ref, v_ref, g_ref, u_ref, fl_ref, cos_ref, sin_ref,
                xiq_ref, zk_ref, dm_ref, dec_ref, bm_ref, gng_ref, bf_ref, pw_ref, ps_ref,
                or_ref, op_ref, c_ref,
                state_sc, tail_sc, carry_sc):
    j = pl.program_id(1)
    ch = SEQ_BLOCK

    @pl.when(j == 0)
    def _():
        state_sc[...] = jnp.zeros_like(state_sc)
        tail_sc[...] = jnp.zeros_like(tail_sc)
        carry_sc[...] = jnp.zeros_like(carry_sc)

    qk = qk_ref[...].astype(F32)
    cs = cos_ref[...]
    sn = sin_ref[...]
    q1, q2 = qk[:, 0:LANES], qk[:, LANES:2 * LANES]
    k1, k2 = qk[:, 2 * LANES:3 * LANES], qk[:, 3 * LANES:4 * LANES]
    qr = jnp.concatenate([q1 * cs - q2 * sn, q1 * sn + q2 * cs], axis=-1)
    kr = jnp.concatenate([k1 * cs - k2 * sn, k1 * sn + k2 * cs], axis=-1)
    qb = qr.astype(BF16)
    qx = (qr * xiq_ref[...]).astype(BF16)
    kb = kr.astype(BF16)
    kz = (kr * zk_ref[...]).astype(BF16)
    v = v_ref[...]
    st = state_sc[...]
    cross = jnp.dot(qx, st.astype(BF16), preferred_element_type=F32)
    qlane = lax.broadcasted_iota(jnp.int32, (1, 2 * LANES), 1)
    qhead = jnp.where(qlane % LANES < RET_HEADS * RET_HALF, (qlane % LANES) // RET_HALF, RET_HEADS)
    inner = []
    for h in range(RET_HEADS):
        qh = jnp.where(qhead == h, qb, jnp.zeros_like(qb))
        s = lax.dot_general(qh, kb, (((1,), (1,)), ((), ())), preferred_element_type=F32)
        p = (s * dm_ref[h]).astype(BF16)
        inner.append(jnp.dot(p, v[:, h * LANES:(h + 1) * LANES], preferred_element_type=F32))
    o = jnp.concatenate(inner, axis=-1) + cross
    kv = lax.dot_general(kz, v, (((0,), (0,)), ((), ())), preferred_element_type=F32)
    state_sc[...] = st * dec_ref[...] + kv * bm_ref[...]

    vlane = lax.broadcasted_iota(jnp.int32, (1, LANES), 1) < RET_DV
    normed = []
    for h in range(RET_HEADS):
        xh = o[:, h * LANES:(h + 1) * LANES]
        mu = jnp.sum(xh, axis=-1, keepdims=True) * (1.0 / RET_DV)
        d = jnp.where(vlane, xh - mu, 0.0)
        var = jnp.sum(d * d, axis=-1, keepdims=True) * (1.0 / RET_DV)
        normed.append(d * lax.rsqrt(var + LN_EPS))
    y = jnp.concatenate(normed, axis=-1) * gng_ref[...]
    gate = g_ref[...].astype(F32)
    or_ref[...] = (gate * _sigmoid(gate) * y).astype(BF16)

    u = u_ref[...].astype(F32)
    tail_rows = tail_sc.shape[0]
    ext = jnp.concatenate([tail_sc[...], u], axis=0)
    tail_sc[...] = u[ch - tail_rows:, :]
    e2 = ext + pltpu.roll(ext, 1, 0)
    e4 = e2 + pltpu.roll(e2, 2, 0)
    e8 = e4 + pltpu.roll(e4, 4, 0)
    e16 = e8 + pltpu.roll(e8, 8, 0)
    glane = lax.broadcasted_iota(jnp.int32, (1, POOL_W), 1) // POOL_GROUP
    win = jnp.where(glane == 0, e2, jnp.where(glane == 1, e4, jnp.where(glane == 2, e8, e16)))
    win = win[tail_rows:, :]
    wlen = jnp.where(glane == 0, 2, jnp.where(glane == 1, 4, jnp.where(glane == 2, 8, 16)))
    pos = j * ch + lax.broadcasted_iota(jnp.int32, (ch, POOL_W), 0)
    cnt = jnp.minimum(pos + 1, wlen).astype(F32)
    pooled = (win / cnt - u).astype(BF16)
    yp = jnp.dot(pooled, pw_ref[...], preferred_element_type=F32) * ps_ref[...]
    op_ref[...] = yp.astype(BF16)

    z = fl_ref[...] + bf_ref[...]
    logf = jnp.minimum(z, 0.0) - jnp.log1p(jnp.exp(-jnp.abs(z)))
    row = lax.broadcasted_iota(jnp.int32, (ch, LANES), 0)
    sh = 1
    while sh < ch:
        logf = logf + jnp.where(row >= sh, pltpu.roll(logf, sh, 0), 0.0)
        sh *= 2
    c = logf + carry_sc[0:1, :]
    carry_sc[...] = jnp.broadcast_to(c[ch - 1:ch, :], carry_sc.shape)
    c_ref[...] = c * LOG2E


def _seq_mix(proj, flog, consts, params, layer, bsz, lp):
    ch = SEQ_BLOCK
    nc = lp // ch
    rows = bsz * lp

    def rowblk(width, colblk):
        return pl.BlockSpec((ch, width), lambda bb, j: (bb * nc + j, colblk))

    def const(shape):
        nd = len(shape)
        return pl.BlockSpec(shape, lambda bb, j: (0,) * nd)

    def layer_param(shape):
        return pl.BlockSpec((None,) + shape, lambda bb, j: (layer,) + (0,) * len(shape))

    in_specs = [
        rowblk(4 * LANES, QR_OFF // (4 * LANES)),
        rowblk(RETV_W, VR_OFF // RETV_W),
        rowblk(RETV_W, GR_OFF // RETV_W),
        rowblk(POOL_W, UP_OFF // POOL_W),
        pl.BlockSpec((ch, LANES), lambda bb, j: (bb * nc + j, 0)),
        pl.BlockSpec((ch, LANES), lambda bb, j: (j, 0)),
        pl.BlockSpec((ch, LANES), lambda bb, j: (j, 0)),
        const((ch, 2 * LANES)), const((ch, 2 * LANES)), const((RET_HEADS, ch, ch)),
        const((1, RETV_W)), const((2 * LANES, RETV_W)),
        layer_param((1, RETV_W)), layer_param((1, LANES)), layer_param((POOL_W, POOL_W)),
        layer_param((1, POOL_W)),
    ]
    out_shape = (jax.ShapeDtypeStruct((rows, RETV_W), BF16),
                 jax.ShapeDtypeStruct((rows, POOL_W), BF16),
                 jax.ShapeDtypeStruct((rows, LANES), F32))
    out_specs = (pl.BlockSpec((ch, RETV_W), lambda bb, j: (bb * nc + j, 0)),
                 pl.BlockSpec((ch, POOL_W), lambda bb, j: (bb * nc + j, 0)),
                 pl.BlockSpec((ch, LANES), lambda bb, j: (bb * nc + j, 0)))
    return pl.pallas_call(
        _seq_kernel,
        out_shape=out_shape,
        grid=(bsz, nc),
        in_specs=in_specs,
        out_specs=out_specs,
        scratch_shapes=[pltpu.VMEM((2 * LANES, RETV_W), F32),
                        pltpu.VMEM((16, POOL_W), F32),
                        pltpu.VMEM((8, LANES), F32)],
        compiler_params=pltpu.CompilerParams(dimension_semantics=("parallel", "arbitrary"),
                                             vmem_limit_bytes=VMEM_LIMIT),
        name="seq_mixers",
    )(proj, proj, proj, proj, flog, consts["cos"], consts["sin"],
      consts["xiq"], consts["zk"], consts["dm"], consts["dec"], consts["bm"],
      params["gn_g"], params["b_f"], params["pool_w"], params["pool_scale"])


FOX_AUG = 3
FOX_VROWS = FOX_DH + 16
FOX_LATE_HEADS = 2
FOX_UNROLL = 8


def _fox_select_matrix():
    sel = np.zeros((LANES, FOX_HEADS * LANES), np.float32)
    for a in range(FOX_AUG):
        for h in range(FOX_HEADS):
            sel[8 * a + h, h * LANES + FOX_DH + a] = -1.0
    return jnp.asarray(sel, BF16)


def _fox_kernel(q_ref, k_ref, v_ref, c_ref, sel_ref, o_ref,
                ka_sc, vt_sc, qa_sc, st_sc, m_sc, acc_sc):
    i = pl.program_id(1)
    tq = SEQ_BLOCK
    tk = SEQ_BLOCK
    npair = FOX_HEADS // 2
    lane = lax.broadcasted_iota(jnp.int32, (1, LANES), 1)

    @pl.when(i == 0)
    def _():
        def chunk(t, carry):
            r0 = pl.multiple_of(t * tk, tk)
            cc = c_ref[pl.ds(r0, tk), :]
            pieces, rem = [], cc
            for a in range(FOX_AUG):
                piece = rem.astype(BF16).astype(F32)
                pieces.append(piece if a == 0 else pltpu.roll(piece, 8 * a, 1))
                rem = rem - piece
            packed = jnp.where(lane < 8, pieces[0], jnp.where(lane < 16, pieces[1], pieces[2]))
            c_aug = jnp.dot(packed.astype(BF16), sel_ref[...], preferred_element_type=F32)
            for pr in range(npair):
                kk = k_ref[pl.ds(r0, tk), pr * LANES:(pr + 1) * LANES].astype(F32)
                vv = v_ref[pl.ds(r0, tk), pr * LANES:(pr + 1) * LANES].astype(F32)
                vtt = vv.T
                ones = jnp.ones((FOX_VROWS - FOX_DH, tk), F32)
                k_heads = (kk, pltpu.roll(kk, FOX_DH, 1))
                v_heads = (vtt[:FOX_DH], vtt[FOX_DH:])
                for hh in range(2):
                    h = 2 * pr + hh
                    aug = jnp.where(lane < FOX_DH, k_heads[hh], c_aug[:, h * LANES:(h + 1) * LANES])
                    ka_sc[h, pl.ds(r0, tk), :] = aug.astype(BF16)
                    vt_sc[h, :, pl.ds(r0, tk)] = jnp.concatenate([v_heads[hh], ones], axis=0).astype(BF16)
            return carry

        lax.fori_loop(0, k_ref.shape[0] // tk, chunk, 0)

    for pr in range(npair):
        qq = q_ref[:, pr * LANES:(pr + 1) * LANES].astype(F32)
        tail = jnp.where(lane < FOX_DH + FOX_AUG, 1.0, 0.0)
        qa_sc[2 * pr] = jnp.where(lane < FOX_DH, qq, tail).astype(BF16)
        qa_sc[2 * pr + 1] = jnp.where(lane < FOX_DH, pltpu.roll(qq, FOX_DH, 1), tail).astype(BF16)
    m_sc[...] = jnp.full(m_sc.shape, NEG_INF, F32)
    acc_sc[...] = jnp.zeros_like(acc_sc)

    def score(t, h):
        ks = t * tk if isinstance(t, int) else pl.multiple_of(t * tk, tk)
        st_sc[h] = lax.dot_general(ka_sc[h, pl.ds(ks, tk), :], qa_sc[h], (((1,), (1,)), ((), ())),
                                   preferred_element_type=F32)

    def attend(t, masked, next_t):
        ks = t * tk if isinstance(t, int) else pl.multiple_of(t * tk, tk)

        def load(h, c0):
            s = st_sc[h, :, c0:c0 + LANES]
            if masked:
                k_id = lax.broadcasted_iota(jnp.int32, (tk, LANES), 0)
                q_id = c0 + lax.broadcasted_iota(jnp.int32, (tk, LANES), 1)
                s = jnp.where(k_id <= q_id, s, NEG_INF)
            return s

        for h in range(early, FOX_HEADS):
            score(t, h)
        for h in range(FOX_HEADS):
            p_halves, a_halves = [], []
            for c0 in range(0, tq, LANES):
                m_prev = m_sc[h, :, c0:c0 + LANES]
                m_new = jnp.maximum(m_prev, jnp.max(load(h, c0), axis=0, keepdims=True))
                m_sc[h, :, c0:c0 + LANES] = m_new
                a_halves.append(jnp.exp2(m_prev - m_new))
                p_halves.append(jnp.exp2(load(h, c0) - m_new).astype(BF16))
            p_t = jnp.concatenate(p_halves, axis=1)
            pv = jnp.dot(vt_sc[h, :, pl.ds(ks, tk)], p_t, preferred_element_type=F32)
            acc_sc[h] = jnp.concatenate(a_halves, axis=1) * acc_sc[h] + pv
            if next_t is not None and h < early:
                score(next_t, h)

    early = FOX_HEADS - FOX_LATE_HEADS
    for h in range(early):
        score(0, h)

    def run(t0, count):
        for d in range(count):
            attend(t0 + d, False, t0 + d + 1)

    def body(u, carry):
        run(FOX_UNROLL * u, FOX_UNROLL)
        return carry

    lax.fori_loop(0, i // FOX_UNROLL, body, 0)
    done = (i // FOX_UNROLL) * FOX_UNROLL
    span = FOX_UNROLL // 2
    while span >= 1:
        take = (i - done) >= span

        @pl.when(take)
        def _(done=done, span=span):
            run(done, span)

        done = done + jnp.where(take, span, 0)
        span //= 2

    attend(i, True, None)

    outs = []
    for pr in range(npair):
        acc_a = acc_sc[2 * pr]
        acc_b = acc_sc[2 * pr + 1]
        o_t = jnp.concatenate([acc_a[:FOX_DH] / acc_a[FOX_DH:FOX_DH + 1],
                               acc_b[:FOX_DH] / acc_b[FOX_DH:FOX_DH + 1]], axis=0)
        outs.append(o_t.T)
    o_ref[...] = jnp.concatenate(outs, axis=-1).astype(BF16)


def _fox(qkv, cum, bsz, lp):
    tq = SEQ_BLOCK
    nq = lp // tq
    rows = bsz * lp
    return pl.pallas_call(
        _fox_kernel,
        out_shape=jax.ShapeDtypeStruct((rows, FOX_W), BF16),
        grid=(bsz, nq),
        in_specs=[
            pl.BlockSpec((tq, FOX_W), lambda bb, i: (bb * nq + i, 0)),
            pl.BlockSpec((lp, FOX_W), lambda bb, i: (bb, 1)),
            pl.BlockSpec((lp, FOX_W), lambda bb, i: (bb, 2)),
            pl.BlockSpec((lp, LANES), lambda bb, i: (bb, 0)),
            pl.BlockSpec((LANES, FOX_HEADS * LANES), lambda bb, i: (0, 0)),
        ],
        out_specs=pl.BlockSpec((tq, FOX_W), lambda bb, i: (bb * nq + i, 0)),
        scratch_shapes=[pltpu.VMEM((FOX_HEADS, lp, LANES), BF16),
                        pltpu.VMEM((FOX_HEADS, FOX_VROWS, lp), BF16),
                        pltpu.VMEM((FOX_HEADS, tq, LANES), BF16),
                        pltpu.VMEM((FOX_HEADS, SEQ_BLOCK, tq), F32),
                        pltpu.VMEM((FOX_HEADS, 1, tq), F32),
                        pltpu.VMEM((FOX_HEADS, FOX_VROWS, tq), F32)],
        compiler_params=pltpu.CompilerParams(
            dimension_semantics=("parallel", "arbitrary"),
            vmem_limit_bytes=VMEM_LIMIT),
        name="fox_attention",
    )(qkv, qkv, qkv, cum, _fox_select_matrix())


def _mix_ffn_kernel(or_ref, op_ref, of_ref, h_ref, wo_ref, g1_ref, b1_ref,
                    w1_ref, w3_ref, w2_ref, g2_ref, b2_ref, o_ref):
    tm = h_ref.shape[0]
    sub = tm // MIX_FFN_SPLIT
    parts = [slice(s * sub, (s + 1) * sub) for s in range(MIX_FFN_SPLIT)]
    ys = []
    for rows in parts:
        mix = jnp.concatenate([or_ref[rows, :], op_ref[rows, :], of_ref[rows, :]], axis=-1)
        ys.append(ALPHA * h_ref[rows, :] + jnp.dot(mix, wo_ref[...], preferred_element_type=F32))
    h1s = [_layer_norm(y, g1_ref[...], b1_ref[...]) for y in ys]
    accs = []
    for h1 in h1s:
        xb = h1.astype(BF16)
        acc = None
        for c0 in range(0, D_FF, FF_CHUNK):
            a = jnp.dot(xb, w1_ref[:, c0:c0 + FF_CHUNK], preferred_element_type=F32)
            b = jnp.dot(xb, w3_ref[:, c0:c0 + FF_CHUNK], preferred_element_type=F32)
            t = (a * _sigmoid(a) * b).astype(BF16)
            part = jnp.dot(t, w2_ref[c0:c0 + FF_CHUNK, :], preferred_element_type=F32)
            acc = part if acc is None else acc + part
        accs.append(acc)
    for rows, h1, acc in zip(parts, h1s, accs):
        o_ref[rows, :] = _layer_norm(ALPHA * h1 + acc, g2_ref[...], b2_ref[...])


def _mix_ffn(o_r, o_p, o_f, h, params, layer, tm):
    rows = h.shape[0]

    def row(width):
        return pl.BlockSpec((tm, width), lambda i: (i, 0))

    def resident(shape):
        return pl.BlockSpec((None,) + shape, lambda i: (layer,) + (0,) * len(shape),
                            pipeline_mode=pl.Buffered(1))

    vec = resident((1, D_MODEL))
    return pl.pallas_call(
        _mix_ffn_kernel,
        out_shape=jax.ShapeDtypeStruct((rows, D_MODEL), F32),
        grid=(rows // tm,),
        in_specs=[row(RETV_W), row(POOL_W), row(FOX_W), row(D_MODEL),
                  resident((MIX_W, D_MODEL)), vec, vec,
                  resident((D_MODEL, D_FF)), resident((D_MODEL, D_FF)), resident((D_FF, D_MODEL)), vec, vec],
        out_specs=row(D_MODEL),
        compiler_params=pltpu.CompilerParams(dimension_semantics=("parallel",),
                                             vmem_limit_bytes=VMEM_LIMIT),
        name="mix_ffn_ln",
    )(o_r, o_p, o_f, h, params["w_out"], params["ln1_g"], params["ln1_b"],
      params["w1"], params["w3"], params["w2"], params["ln2_g"], params["ln2_b"])


def _pack_w_in(w):
    lead = w.shape[:-1]

    def rot_pack(seg):
        s = seg.reshape(lead + (RET_HEADS, RET_DK))
        x1 = s[..., :RET_HALF].reshape(lead + (RET_HEADS * RET_HALF,))
        x2 = s[..., RET_HALF:].reshape(lead + (RET_HEADS * RET_HALF,))
        z = jnp.zeros(lead + (LANES - RET_HEADS * RET_HALF,), w.dtype)
        return jnp.concatenate([x1, z, x2, z], axis=-1)

    def head_pad(seg):
        s = seg.reshape(lead + (RET_HEADS, RET_DV))
        widths = [(0, 0)] * len(lead) + [(0, 0), (0, RET_HEAD_PAD - RET_DV)]
        return jnp.pad(s, widths).reshape(lead + (RETV_W,))

    o = 0
    segs = []
    for sz in (RET_QK, RET_QK, RET_W, RET_W, POOL_W, FOX_W, FOX_W, FOX_W, FOX_HEADS):
        segs.append(w[..., o:o + sz])
        o += sz
    q_r, k_r, v_r, g_r, u_p, q_f, k_f, v_f, f_l = segs
    f_pad = jnp.pad(f_l, [(0, 0)] * len(lead) + [(0, LANES - FOX_HEADS)])
    packed = jnp.concatenate(
        [rot_pack(q_r), rot_pack(k_r), head_pad(v_r), head_pad(g_r), u_p, q_f, k_f, v_f, f_pad], axis=-1)
    return packed.astype(BF16)


def _pack_w_out(w):
    depth, _, d = w.shape
    w_r = jnp.pad(w[:, :RET_W].reshape(depth, RET_HEADS, RET_DV, d),
                  ((0, 0), (0, 0), (0, RET_HEAD_PAD - RET_DV), (0, 0))).reshape(depth, RETV_W, d)
    return jnp.concatenate([w_r, w[:, RET_W:]], axis=1).astype(BF16)


def _pad_heads(vec):
    depth = vec.shape[0]
    padded = jnp.pad(vec.reshape(depth, RET_HEADS, RET_DV), ((0, 0), (0, 0), (0, RET_HEAD_PAD - RET_DV)))
    return padded.reshape(depth, 1, RETV_W)


def _block_diag(pw):
    g = len(POOL_WINDOWS)
    eye = jnp.eye(g, dtype=pw.dtype)
    return jnp.einsum("lgij,gh->lgihj", pw, eye).reshape(pw.shape[0], POOL_W, POOL_W).astype(BF16)


def _cast_kernel(x_ref, o_ref):
    o_ref[...] = x_ref[...].astype(o_ref.dtype)


def _to_bf16(w):
    depth, k, n = w.shape
    slab = max(s for s in range(16, k + 1, 16) if k % s == 0 and s * n <= CAST_SLAB_ELEMS)
    spec = pl.BlockSpec((None, slab, n), lambda l, i: (l, i, 0))
    return pl.pallas_call(
        _cast_kernel,
        out_shape=jax.ShapeDtypeStruct(w.shape, BF16),
        grid=(depth, k // slab),
        in_specs=[spec],
        out_specs=spec,
        compiler_params=pltpu.CompilerParams(dimension_semantics=("parallel", "parallel")),
        name="weights_to_bf16",
    )(w)


def _prepare_params(w_in, b_f, ret_gn_g, pool_w, pool_scale, w_out, ln1_g, ln1_b,
                    w_ffn1, w_ffn3, w_ffn2, ln2_g, ln2_b):
    depth = w_in.shape[0]
    vec = lambda a: a.reshape(depth, 1, a.shape[-1])
    return dict(
        w_in=_pack_w_in(w_in),
        gn_g=_pad_heads(ret_gn_g),
        b_f=vec(jnp.pad(b_f, ((0, 0), (0, LANES - FOX_HEADS)))),
        pool_w=_block_diag(pool_w),
        pool_scale=vec(pool_scale),
        w_out=_pack_w_out(w_out),
        ln1_g=vec(ln1_g), ln1_b=vec(ln1_b), ln2_g=vec(ln2_g), ln2_b=vec(ln2_b),
        w1=_to_bf16(w_ffn1), w3=_to_bf16(w_ffn3), w2=_to_bf16(w_ffn2))


def _retention_tables():
    ch = SEQ_BLOCK
    gamma = (1.0 - 2.0 ** (-5.0 - np.arange(RET_HEADS, dtype=np.float32))).astype(np.float32)
    lg = np.log(gamma).astype(np.float32)
    i = np.arange(ch, dtype=np.float32)
    diff = i[:, None] - i[None, :]
    dm = np.where(diff >= 0, np.exp(lg[:, None, None] * np.maximum(diff, 0.0)), 0.0).astype(np.float32)
    xi = np.exp(lg[:, None] * (i + 1.0)).astype(np.float32)
    zeta = np.exp(lg[:, None] * (ch - 1.0 - i)).astype(np.float32)
    lane = np.arange(2 * LANES)
    within = lane % LANES
    lane_head = np.where(within < RET_HEADS * RET_HALF, within // RET_HALF, -1)
    xiq = np.zeros((ch, 2 * LANES), np.float32)
    zk = np.zeros((ch, 2 * LANES), np.float32)
    bm = np.zeros((2 * LANES, RETV_W), np.float32)
    for h in range(RET_HEADS):
        sel = lane_head == h
        xiq[:, sel] = xi[h][:, None]
        zk[:, sel] = zeta[h][:, None]
        bm[sel, h * LANES:(h + 1) * LANES] = 1.0
    dec = np.repeat(np.exp(lg * ch).astype(np.float32), LANES)[None, :]
    return dict(xiq=jnp.asarray(xiq), zk=jnp.asarray(zk), dm=jnp.asarray(dm),
                dec=jnp.asarray(dec), bm=jnp.asarray(bm))


def _rotary_tables(lp):
    pos = jnp.arange(lp, dtype=F32)
    inv_freq = ROPE_BASE ** (-jnp.arange(RET_HALF, dtype=F32) / RET_HALF)
    ang = pos[:, None] * inv_freq[None, :]
    pad = LANES - RET_HEADS * RET_HALF
    cos = jnp.pad(jnp.tile(jnp.cos(ang), (1, RET_HEADS)), ((0, 0), (0, pad)))
    sin = jnp.pad(jnp.tile(jnp.sin(ang), (1, RET_HEADS)), ((0, 0), (0, pad)))
    return cos, sin


def kernel(x, meta, ln_emb_g, ln_emb_b, w_in, b_f, ret_gn_g, pool_w, pool_scale, w_out, ln1_g, ln1_b,
           w_ffn1, w_ffn3, w_ffn2, ln2_g, ln2_b):
    bsz, seq, d = x.shape
    assert d == D_MODEL and seq % SEQ_BLOCK == 0
    depth = w_in.shape[0]
    assert depth == DEPTH
    lp = seq + SEQ_BLOCK
    rows = bsz * lp
    tm = ROW_TILE if rows % ROW_TILE == 0 else SEQ_BLOCK

    consts = _retention_tables()
    consts["cos"], consts["sin"] = _rotary_tables(lp)
    params = _prepare_params(w_in, b_f, ret_gn_g, pool_w, pool_scale, w_out, ln1_g, ln1_b,
                             w_ffn1, w_ffn3, w_ffn2, ln2_g, ln2_b)

    h, ret, qkv, flog = _embed_inproj(x, meta, ln_emb_g, ln_emb_b, params["w_in"], lp)
    for l in range(depth):
        if l > 0:
            ret, qkv, flog = _inproj(h, params["w_in"], l, tm)
        o_r, o_p, cum = _seq_mix(ret, flog, consts, params, l, bsz, lp)
        o_f = _fox(qkv, cum, bsz, lp)
        h = _mix_ffn(o_r, o_p, o_f, h, params, l, tm)
    return h.reshape(bsz, lp, d)[:, N_META:N_META + seq]
```

```python
import functools

import numpy as np
import jax
import jax.numpy as jnp
from jax import lax
from jax.experimental import pallas as pl
from jax.experimental.pallas import tpu as pltpu

F32 = jnp.float32
BF16 = jnp.bfloat16

D_MODEL = 1024
N_META = 16
RET_HEADS = 4
RET_DK = 48
RET_HALF = RET_DK // 2
RET_DV = 96
RET_QK = RET_HEADS * RET_DK
RET_W = RET_HEADS * RET_DV
POOL_WINDOWS = (2, 4, 8, 16)
POOL_GROUP = 64
POOL_W = len(POOL_WINDOWS) * POOL_GROUP
FOX_HEADS = 6
FOX_DH = 64
FOX_W = FOX_HEADS * FOX_DH
D_FF = 2816
ROPE_BASE = 10000.0
LN_EPS = 1e-5
NEG_INF = -1e30
DEPTH = 2
ALPHA = (2.0 * DEPTH) ** 0.25
LOG2E = 1.4426950408889634

LANES = 128
MXU_DIM = 256
SEQ_BLOCK = 256
ROW_TILE = 1024
VMEM_LIMIT = 56 * 1024 * 1024

RET_HEAD_PAD = LANES
QR_OFF = 0
KR_OFF = 2 * LANES
VR_OFF = 4 * LANES
GR_OFF = VR_OFF + RET_HEADS * RET_HEAD_PAD
UP_OFF = GR_OFF + RET_HEADS * RET_HEAD_PAD
QF_OFF = UP_OFF + POOL_W
KF_OFF = QF_OFF + FOX_W
VF_OFF = KF_OFF + FOX_W
FL_OFF = VF_OFF + FOX_W
RET_COLS = QF_OFF
FOX_COLS = 3 * FOX_W
N_PACK = FL_OFF + LANES
RETV_W = RET_HEADS * RET_HEAD_PAD
MIX_W = RETV_W + POOL_W + FOX_W
PROJ_CHUNK = 512
FF_CHUNK = 256
CAST_SLAB_ELEMS = 768 * 1024
MIX_FFN_SPLIT = 2


def _layer_norm(x, g, b):
    mu = jnp.mean(x, axis=-1, keepdims=True)
    d = x - mu
    var = jnp.mean(d * d, axis=-1, keepdims=True)
    return d * lax.rsqrt(var + LN_EPS) * g + b


def _sigmoid(x):
    return 1.0 / (1.0 + jnp.exp(-x))


def _project(xb, w_ref, cs_ref, ret_ref, fox_ref, flog_ref):
    def cols(c0, width):
        r = jnp.dot(xb, w_ref[:, c0:c0 + width], preferred_element_type=F32)
        return r * cs_ref[:, c0:c0 + width]

    for c0 in range(0, RET_COLS, PROJ_CHUNK):
        width = min(PROJ_CHUNK, RET_COLS - c0)
        ret_ref[:, c0:c0 + width] = cols(c0, width).astype(BF16)
    for c0 in range(0, FOX_COLS - LANES, PROJ_CHUNK):
        fox_ref[:, c0:c0 + PROJ_CHUNK] = cols(QF_OFF + c0, PROJ_CHUNK).astype(BF16)
    tail = cols(FL_OFF - LANES, 2 * LANES)
    fox_ref[:, FOX_COLS - LANES:] = tail[:, :LANES].astype(BF16)
    flog_ref[...] = tail[:, LANES:]


def _embed_inproj_kernel(meta_ref, xm_ref, xe_ref, g_ref, b_ref, w_ref, cs_ref,
                         h_ref, ret_ref, fox_ref, flog_ref):
    j = pl.program_id(1)
    last = pl.num_programs(1) - 1
    top = jnp.where(j == 0, meta_ref[...], xe_ref[...])
    body = jnp.where(j == last, 0.0, xm_ref[0:SEQ_BLOCK - N_META, :])
    rows = jnp.concatenate([top, body], axis=0)
    h = _layer_norm(rows, g_ref[...], b_ref[...])
    h_ref[...] = h
    _project(h.astype(BF16), w_ref, cs_ref, ret_ref, fox_ref, flog_ref)


def _embed_inproj(x, meta, g, b, w, lp):
    bsz, seq, d = x.shape
    nblk = lp // SEQ_BLOCK
    n_xblk = seq // SEQ_BLOCK
    per = SEQ_BLOCK // N_META
    rows = bsz * lp
    out_row = lambda width: pl.BlockSpec((SEQ_BLOCK, width), lambda bb, j: (bb * nblk + j, 0))
    return pl.pallas_call(
        _embed_inproj_kernel,
        out_shape=(jax.ShapeDtypeStruct((rows, d), F32),
                   jax.ShapeDtypeStruct((rows, RET_COLS), BF16),
                   jax.ShapeDtypeStruct((rows, FOX_COLS), BF16),
                   jax.ShapeDtypeStruct((rows, LANES), F32)),
        grid=(bsz, nblk),
        in_specs=[
            pl.BlockSpec((N_META, d), lambda bb, j: (0, 0)),
            pl.BlockSpec((None, SEQ_BLOCK, d), lambda bb, j: (bb, jnp.minimum(j, n_xblk - 1), 0)),
            pl.BlockSpec((None, N_META, d), lambda bb, j: (bb, jnp.maximum(per * j - 1, 0), 0)),
            pl.BlockSpec((1, d), lambda bb, j: (0, 0)),
            pl.BlockSpec((1, d), lambda bb, j: (0, 0)),
            pl.BlockSpec((None, D_MODEL, N_PACK), lambda bb, j: (0, 0, 0)),
            pl.BlockSpec((1, N_PACK), lambda bb, j: (0, 0)),
        ],
        out_specs=(out_row(d), out_row(RET_COLS), out_row(FOX_COLS), out_row(LANES)),
        compiler_params=pltpu.CompilerParams(dimension_semantics=("parallel", "arbitrary"),
                                             vmem_limit_bytes=VMEM_LIMIT),
        name="embed_in_proj",
    )(meta, x, x, g.reshape(1, d), b.reshape(1, d), w, _inproj_col_scale())


def _inproj_kernel(h_ref, w_ref, cs_ref, ret_ref, fox_ref, flog_ref):
    _project(h_ref[...].astype(BF16), w_ref, cs_ref, ret_ref, fox_ref, flog_ref)


def _inproj_col_scale():
    cs = np.ones((1, N_PACK), np.float32)
    cs[0, KR_OFF:KR_OFF + 2 * LANES] = RET_DK ** -0.5
    cs[0, QF_OFF:QF_OFF + FOX_W] = FOX_DH ** -0.5 * LOG2E
    return jnp.asarray(cs)


def _inproj(h, w, layer, tm):
    rows = h.shape[0]
    return pl.pallas_call(
        _inproj_kernel,
        out_shape=(jax.ShapeDtypeStruct((rows, RET_COLS), BF16),
                   jax.ShapeDtypeStruct((rows, FOX_COLS), BF16),
                   jax.ShapeDtypeStruct((rows, LANES), F32)),
        grid=(rows // tm,),
        in_specs=[
            pl.BlockSpec((tm, D_MODEL), lambda i: (i, 0)),
            pl.BlockSpec((None, D_MODEL, N_PACK), lambda i: (layer, 0, 0)),
            pl.BlockSpec((1, N_PACK), lambda i: (0, 0)),
        ],
        out_specs=(pl.BlockSpec((tm, RET_COLS), lambda i: (i, 0)),
                   pl.BlockSpec((tm, FOX_COLS), lambda i: (i, 0)),
                   pl.BlockSpec((tm, LANES), lambda i: (i, 0))),
        compiler_params=pltpu.CompilerParams(dimension_semantics=("parallel",),
                                             vmem_limit_bytes=VMEM_LIMIT),
        name="in_proj",
    )(h, w, _inproj_col_scale())


def _seq_kernel(qk_ref, v_ref, g_ref, u_ref, fl_ref, cos_ref, sin_ref,
                xiq_ref, zk_ref, dm_ref, dec_ref, bm_ref, gng_ref, bf_ref, pw_ref, ps_ref,
                or_ref, op_ref, c_ref,
                state_sc, tail_sc, carry_sc):
    j = pl.program_id(1)
    ch = SEQ_BLOCK

    @pl.when(j == 0)
    def _():
        state_sc[...] = jnp.zeros_like(state_sc)
        tail_sc[...] = jnp.zeros_like(tail_sc)
        carry_sc[...] = jnp.zeros_like(carry_sc)

    qk = qk_ref[...].astype(F32)
    cs = cos_ref[...]
    sn = sin_ref[...]
    q1, q2 = qk[:, 0:LANES], qk[:, LANES:2 * LANES]
    k1, k2 = qk[:, 2 * LANES:3 * LANES], qk[:, 3 * LANES:4 * LANES]
    qr = jnp.concatenate([q1 * cs - q2 * sn, q1 * sn + q2 * cs], axis=-1)
    kr = jnp.concatenate([k1 * cs - k2 * sn, k1 * sn + k2 * cs], axis=-1)
    qb = qr.astype(BF16)
    qx = (qr * xiq_ref[...]).astype(BF16)
    kb = kr.astype(BF16)
    kz = (kr * zk_ref[...]).astype(BF16)
    v = v_ref[...]
    st = state_sc[...]
    cross = jnp.dot(qx, st.astype(BF16), preferred_element_type=F32)
    qlane = lax.broadcasted_iota(jnp.int32, (1, 2 * LANES), 1)
    qhead = jnp.where(qlane % LANES < RET_HEADS * RET_HALF, (qlane % LANES) // RET_HALF, RET_HEADS)
    inner = []
    for h in range(RET_HEADS):
        qh = jnp.where(qhead == h, qb, jnp.zeros_like(qb))
        s = lax.dot_general(qh, kb, (((1,), (1,)), ((), ())), preferred_element_type=F32)
        p = (s * dm_ref[h]).astype(BF16)
        inner.append(jnp.dot(p, v[:, h * LANES:(h + 1) * LANES], preferred_element_type=F32))
    o = jnp.concatenate(inner, axis=-1) + cross
    kv = lax.dot_general(kz, v, (((0,), (0,)), ((), ())), preferred_element_type=F32)
    state_sc[...] = st * dec_ref[...] + kv * bm_ref[...]

    vlane = lax.broadcasted_iota(jnp.int32, (1, LANES), 1) < RET_DV
    normed = []
    for h in range(RET_HEADS):
        xh = o[:, h * LANES:(h + 1) * LANES]
        mu = jnp.sum(xh, axis=-1, keepdims=True) * (1.0 / RET_DV)
        d = jnp.where(vlane, xh - mu, 0.0)
        var = jnp.sum(d * d, axis=-1, keepdims=True) * (1.0 / RET_DV)
        normed.append(d * lax.rsqrt(var + LN_EPS))
    y = jnp.concatenate(normed, axis=-1) * gng_ref[...]
    gate = g_ref[...].astype(F32)
    or_ref[...] = (gate * _sigmoid(gate) * y).astype(BF16)

    u = u_ref[...].astype(F32)
    tail_rows = tail_sc.shape[0]
    ext = jnp.concatenate([tail_sc[...], u], axis=0)
    tail_sc[...] = u[ch - tail_rows:, :]
    e2 = ext + pltpu.roll(ext, 1, 0)
    e4 = e2 + pltpu.roll(e2, 2, 0)
    e8 = e4 + pltpu.roll(e4, 4, 0)
    e16 = e8 + pltpu.roll(e8, 8, 0)
    glane = lax.broadcasted_iota(jnp.int32, (1, POOL_W), 1) // POOL_GROUP
    win = jnp.where(glane == 0, e2, jnp.where(glane == 1, e4, jnp.where(glane == 2, e8, e16)))
    win = win[tail_rows:, :]
    wlen = jnp.where(glane == 0, 2, jnp.where(glane == 1, 4, jnp.where(glane == 2, 8, 16)))
    pos = j * ch + lax.broadcasted_iota(jnp.int32, (ch, POOL_W), 0)
    cnt = jnp.minimum(pos + 1, wlen).astype(F32)
    pooled = (win / cnt - u).astype(BF16)
    yp = jnp.dot(pooled, pw_ref[...], preferred_element_type=F32) * ps_ref[...]
    op_ref[...] = yp.astype(BF16)

    z = fl_ref[...] + bf_ref[...]
    logf = jnp.minimum(z, 0.0) - jnp.log1p(jnp.exp(-jnp.abs(z)))
    row = lax.broadcasted_iota(jnp.int32, (ch, LANES), 0)
    sh = 1
    while sh < ch:
        logf = logf + jnp.where(row >= sh, pltpu.roll(logf, sh, 0), 0.0)
        sh *= 2
    c = logf + carry_sc[0:1, :]
    carry_sc[...] = jnp.broadcast_to(c[ch - 1:ch, :], carry_sc.shape)
    c_ref[...] = c * LOG2E


def _seq_mix(proj, flog, consts, params, layer, bsz, lp):
    ch = SEQ_BLOCK
    nc = lp // ch
    rows = bsz * lp

    def rowblk(width, colblk):
        return pl.BlockSpec((ch, width), lambda bb, j: (bb * nc + j, colblk))

    def const(shape):
        nd = len(shape)
        return pl.BlockSpec(shape, lambda bb, j: (0,) * nd)

    def layer_param(shape):
        return pl.BlockSpec((None,) + shape, lambda bb, j: (layer,) + (0,) * len(shape))

    in_specs = [
        rowblk(4 * LANES, QR_OFF // (4 * LANES)),
        rowblk(RETV_W, VR_OFF // RETV_W),
        rowblk(RETV_W, GR_OFF // RETV_W),
        rowblk(POOL_W, UP_OFF // POOL_W),
        pl.BlockSpec((ch, LANES), lambda bb, j: (bb * nc + j, 0)),
        pl.BlockSpec((ch, LANES), lambda bb, j: (j, 0)),
        pl.BlockSpec((ch, LANES), lambda bb, j: (j, 0)),
        const((ch, 2 * LANES)), const((ch, 2 * LANES)), const((RET_HEADS, ch, ch)),
        const((1, RETV_W)), const((2 * LANES, RETV_W)),
        layer_param((1, RETV_W)), layer_param((1, LANES)), layer_param((POOL_W, POOL_W)),
        layer_param((1, POOL_W)),
    ]
    out_shape = (jax.ShapeDtypeStruct((rows, RETV_W), BF16),
                 jax.ShapeDtypeStruct((rows, POOL_W), BF16),
                 jax.ShapeDtypeStruct((rows, LANES), F32))
    out_specs = (pl.BlockSpec((ch, RETV_W), lambda bb, j: (bb * nc + j, 0)),
                 pl.BlockSpec((ch, POOL_W), lambda bb, j: (bb * nc + j, 0)),
                 pl.BlockSpec((ch, LANES), lambda bb, j: (bb * nc + j, 0)))
    return pl.pallas_call(
        _seq_kernel,
        out_shape=out_shape,
        grid=(bsz, nc),
        in_specs=in_specs,
        out_specs=out_specs,
        scratch_shapes=[pltpu.VMEM((2 * LANES, RETV_W), F32),
                        pltpu.VMEM((16, POOL_W), F32),
                        pltpu.VMEM((8, LANES), F32)],
        compiler_params=pltpu.CompilerParams(dimension_semantics=("parallel", "arbitrary"),
                                             vmem_limit_bytes=VMEM_LIMIT),
        name="seq_mixers",
    )(proj, proj, proj, proj, flog, consts["cos"], consts["sin"],
      consts["xiq"], consts["zk"], consts["dm"], consts["dec"], consts["bm"],
      params["gn_g"], params["b_f"], params["pool_w"], params["pool_scale"])


FOX_AUG = 3
FOX_VROWS = FOX_DH + 16
FOX_LATE_HEADS = 2
FOX_UNROLL = 8


def _fox_select_matrix():
    sel = np.zeros((LANES, FOX_HEADS * LANES), np.float32)
    for a in range(FOX_AUG):
        for h in range(FOX_HEADS):
            sel[8 * a + h, h * LANES + FOX_DH + a] = -1.0
    return jnp.asarray(sel, BF16)


def _fox_kernel(q_ref, k_ref, v_ref, c_ref, sel_ref, o_ref,
                ka_sc, vt_sc, qa_sc, st_sc, m_sc, acc_sc):
    i = pl.program_id(1)
    tq = SEQ_BLOCK
    tk = SEQ_BLOCK
    npair = FOX_HEADS // 2
    lane = lax.broadcasted_iota(jnp.int32, (1, LANES), 1)

    @pl.when(i == 0)
    def _():
        def chunk(t, carry):
            r0 = pl.multiple_of(t * tk, tk)
            cc = c_ref[pl.ds(r0, tk), :]
            pieces, rem = [], cc
            for a in range(FOX_AUG):
                piece = rem.astype(BF16).astype(F32)
                pieces.append(piece if a == 0 else pltpu.roll(piece, 8 * a, 1))
                rem = rem - piece
            packed = jnp.where(lane < 8, pieces[0], jnp.where(lane < 16, pieces[1], pieces[2]))
            c_aug = jnp.dot(packed.astype(BF16), sel_ref[...], preferred_element_type=F32)
            for pr in range(npair):
                kk = k_ref[pl.ds(r0, tk), pr * LANES:(pr + 1) * LANES].astype(F32)
                vv = v_ref[pl.ds(r0, tk), pr * LANES:(pr + 1) * LANES].astype(F32)
                vtt = vv.T
                ones = jnp.ones((FOX_VROWS - FOX_DH, tk), F32)
                k_heads = (kk, pltpu.roll(kk, FOX_DH, 1))
                v_heads = (vtt[:FOX_DH], vtt[FOX_DH:])
                for hh in range(2):
                    h = 2 * pr + hh
                    aug = jnp.where(lane < FOX_DH, k_heads[hh], c_aug[:, h * LANES:(h + 1) * LANES])
                    ka_sc[h, pl.ds(r0, tk), :] = aug.astype(BF16)
                    vt_sc[h, :, pl.ds(r0, tk)] = jnp.concatenate([v_heads[hh], ones], axis=0).astype(BF16)
            return carry

        lax.fori_loop(0, k_ref.shape[0] // tk, chunk, 0)

    for pr in range(npair):
        qq = q_ref[:, pr * LANES:(pr + 1) * LANES].astype(F32)
        tail = jnp.where(lane < FOX_DH + FOX_AUG, 1.0, 0.0)
        qa_sc[2 * pr] = jnp.where(lane < FOX_DH, qq, tail).astype(BF16)
        qa_sc[2 * pr + 1] = jnp.where(lane < FOX_DH, pltpu.roll(qq, FOX_DH, 1), tail).astype(BF16)
    m_sc[...] = jnp.full(m_sc.shape, NEG_INF, F32)
    acc_sc[...] = jnp.zeros_like(acc_sc)

    def score(t, h):
        ks = t * tk if isinstance(t, int) else pl.multiple_of(t * tk, tk)
        st_sc[h] = lax.dot_general(ka_sc[h, pl.ds(ks, tk), :], qa_sc[h], (((1,), (1,)), ((), ())),
                                   preferred_element_type=F32)

    def attend(t, masked, next_t):
        ks = t * tk if isinstance(t, int) else pl.multiple_of(t * tk, tk)

        def load(h, c0):
            s = st_sc[h, :, c0:c0 + LANES]
            if masked:
                k_id = lax.broadcasted_iota(jnp.int32, (tk, LANES), 0)
                q_id = c0 + lax.broadcasted_iota(jnp.int32, (tk, LANES), 1)
                s = jnp.where(k_id <= q_id, s, NEG_INF)
            return s

        for h in range(early, FOX_HEADS):
            score(t, h)
        for h in range(FOX_HEADS):
            p_halves, a_halves = [], []
            for c0 in range(0, tq, LANES):
                m_prev = m_sc[h, :, c0:c0 + LANES]
                m_new = jnp.maximum(m_prev, jnp.max(load(h, c0), axis=0, keepdims=True))
                m_sc[h, :, c0:c0 + LANES] = m_new
                a_halves.append(jnp.exp2(m_prev - m_new))
                p_halves.append(jnp.exp2(load(h, c0) - m_new).astype(BF16))
            p_t = jnp.concatenate(p_halves, axis=1)
            pv = jnp.dot(vt_sc[h, :, pl.ds(ks, tk)], p_t, preferred_element_type=F32)
            acc_sc[h] = jnp.concatenate(a_halves, axis=1) * acc_sc[h] + pv
            if next_t is not None and h < early:
                score(next_t, h)

    early = FOX_HEADS - FOX_LATE_HEADS
    for h in range(early):
        score(0, h)

    def run(t0, count):
        for d in range(count):
            attend(t0 + d, False, t0 + d + 1)

    def body(u, carry):
        run(FOX_UNROLL * u, FOX_UNROLL)
        return carry

    lax.fori_loop(0, i // FOX_UNROLL, body, 0)
    done = (i // FOX_UNROLL) * FOX_UNROLL
    span = FOX_UNROLL // 2
    while span >= 1:
        take = (i - done) >= span

        @pl.when(take)
        def _(done=done, span=span):
            run(done, span)

        done = done + jnp.where(take, span, 0)
        span //= 2

    attend(i, True, None)

    outs = []
    for pr in range(npair):
        acc_a = acc_sc[2 * pr]
        acc_b = acc_sc[2 * pr + 1]
        o_t = jnp.concatenate([acc_a[:FOX_DH] / acc_a[FOX_DH:FOX_DH + 1],
                               acc_b[:FOX_DH] / acc_b[FOX_DH:FOX_DH + 1]], axis=0)
        outs.append(o_t.T)
    o_ref[...] = jnp.concatenate(outs, axis=-1).astype(BF16)


def _fox(qkv, cum, bsz, lp):
    tq = SEQ_BLOCK
    nq = lp // tq
    rows = bsz * lp
    return pl.pallas_call(
        _fox_kernel,
        out_shape=jax.ShapeDtypeStruct((rows, FOX_W), BF16),
        grid=(bsz, nq),
        in_specs=[
            pl.BlockSpec((tq, FOX_W), lambda bb, i: (bb * nq + i, 0)),
            pl.BlockSpec((lp, FOX_W), lambda bb, i: (bb, 1)),
            pl.BlockSpec((lp, FOX_W), lambda bb, i: (bb, 2)),
            pl.BlockSpec((lp, LANES), lambda bb, i: (bb, 0)),
            pl.BlockSpec((LANES, FOX_HEADS * LANES), lambda bb, i: (0, 0)),
        ],
        out_specs=pl.BlockSpec((tq, FOX_W), lambda bb, i: (bb * nq + i, 0)),
        scratch_shapes=[pltpu.VMEM((FOX_HEADS, lp, LANES), BF16),
                        pltpu.VMEM((FOX_HEADS, FOX_VROWS, lp), BF16),
                        pltpu.VMEM((FOX_HEADS, tq, LANES), BF16),
                        pltpu.VMEM((FOX_HEADS, SEQ_BLOCK, tq), F32),
                        pltpu.VMEM((FOX_HEADS, 1, tq), F32),
                        pltpu.VMEM((FOX_HEADS, FOX_VROWS, tq), F32)],
        compiler_params=pltpu.CompilerParams(
            dimension_semantics=("parallel", "arbitrary"),
            vmem_limit_bytes=VMEM_LIMIT),
        name="fox_attention",
    )(qkv, qkv, qkv, cum, _fox_select_matrix())


def _mix_ffn_kernel(or_ref, op_ref, of_ref, h_ref, wo_ref, g1_ref, b1_ref,
                    w1_ref, w3_ref, w2_ref, g2_ref, b2_ref, o_ref):
    tm = h_ref.shape[0]
    sub = tm // MIX_FFN_SPLIT
    parts = [slice(s * sub, (s + 1) * sub) for s in range(MIX_FFN_SPLIT)]
    ys = []
    for rows in parts:
        mix = jnp.concatenate([or_ref[rows, :], op_ref[rows, :], of_ref[rows, :]], axis=-1)
        ys.append(ALPHA * h_ref[rows, :] + jnp.dot(mix, wo_ref[...], preferred_element_type=F32))
    h1s = [_layer_norm(y, g1_ref[...], b1_ref[...]) for y in ys]
    accs = []
    for h1 in h1s:
        xb = h1.astype(BF16)
        acc = None
        for c0 in range(0, D_FF, FF_CHUNK):
            a = jnp.dot(xb, w1_ref[:, c0:c0 + FF_CHUNK], preferred_element_type=F32)
            b = jnp.dot(xb, w3_ref[:, c0:c0 + FF_CHUNK], preferred_element_type=F32)
            t = (a * _sigmoid(a) * b).astype(BF16)
            part = jnp.dot(t, w2_ref[c0:c0 + FF_CHUNK, :], preferred_element_type=F32)
            acc = part if acc is None else acc + part
        accs.append(acc)
    for rows, h1, acc in zip(parts, h1s, accs):
        o_ref[rows, :] = _layer_norm(ALPHA * h1 + acc, g2_ref[...], b2_ref[...])


def _mix_ffn(o_r, o_p, o_f, h, params, layer, tm):
    rows = h.shape[0]

    def row(width):
        return pl.BlockSpec((tm, width), lambda i: (i, 0))

    def resident(shape):
        return pl.BlockSpec((None,) + shape, lambda i: (layer,) + (0,) * len(shape),
                            pipeline_mode=pl.Buffered(1))

    vec = resident((1, D_MODEL))
    return pl.pallas_call(
        _mix_ffn_kernel,
        out_shape=jax.ShapeDtypeStruct((rows, D_MODEL), F32),
        grid=(rows // tm,),
        in_specs=[row(RETV_W), row(POOL_W), row(FOX_W), row(D_MODEL),
                  resident((MIX_W, D_MODEL)), vec, vec,
                  resident((D_MODEL, D_FF)), resident((D_MODEL, D_FF)), resident((D_FF, D_MODEL)), vec, vec],
        out_specs=row(D_MODEL),
        compiler_params=pltpu.CompilerParams(dimension_semantics=("parallel",),
                                             vmem_limit_bytes=VMEM_LIMIT),
        name="mix_ffn_ln",
    )(o_r, o_p, o_f, h, params["w_out"], params["ln1_g"], params["ln1_b"],
      params["w1"], params["w3"], params["w2"], params["ln2_g"], params["ln2_b"])


def _pack_w_in(w):
    lead = w.shape[:-1]

    def rot_pack(seg):
        s = seg.reshape(lead + (RET_HEADS, RET_DK))
        x1 = s[..., :RET_HALF].reshape(lead + (RET_HEADS * RET_HALF,))
        x2 = s[..., RET_HALF:].reshape(lead + (RET_HEADS * RET_HALF,))
        z = jnp.zeros(lead + (LANES - RET_HEADS * RET_HALF,), w.dtype)
        return jnp.concatenate([x1, z, x2, z], axis=-1)

    def head_pad(seg):
        s = seg.reshape(lead + (RET_HEADS, RET_DV))
        widths = [(0, 0)] * len(lead) + [(0, 0), (0, RET_HEAD_PAD - RET_DV)]
        return jnp.pad(s, widths).reshape(lead + (RETV_W,))

    o = 0
    segs = []
    for sz in (RET_QK, RET_QK, RET_W, RET_W, POOL_W, FOX_W, FOX_W, FOX_W, FOX_HEADS):
        segs.append(w[..., o:o + sz])
        o += sz
    q_r, k_r, v_r, g_r, u_p, q_f, k_f, v_f, f_l = segs
    f_pad = jnp.pad(f_l, [(0, 0)] * len(lead) + [(0, LANES - FOX_HEADS)])
    packed = jnp.concatenate(
        [rot_pack(q_r), rot_pack(k_r), head_pad(v_r), head_pad(g_r), u_p, q_f, k_f, v_f, f_pad], axis=-1)
    return packed.astype(BF16)


def _pack_w_in_kernel(w_ref, o_ref):
    w = w_ref[...]
    rows = w.shape[0]
    pad = jnp.zeros((rows, LANES - RET_HEADS * RET_HALF), w.dtype)
    pieces = []
    for base in (0, RET_QK):
        for half in range(2):
            for h in range(RET_HEADS):
                lo = base + h * RET_DK + half * RET_HALF
                pieces.append(w[:, lo:lo + RET_HALF])
            pieces.append(pad)
    for base in (2 * RET_QK, 2 * RET_QK + RET_W):
        for h in range(RET_HEADS):
            pieces.append(w[:, base + h * RET_DV:base + (h + 1) * RET_DV])
            pieces.append(pad)
    aligned = 2 * RET_QK + 2 * RET_W
    pieces.append(w[:, aligned:aligned + POOL_W + 3 * FOX_W])
    pieces.append(w[:, aligned + POOL_W + 3 * FOX_W:])
    pieces.append(jnp.zeros((rows, LANES - FOX_HEADS), w.dtype))
    o_ref[...] = jnp.concatenate(pieces, axis=1).astype(BF16)


def _pack_w_in_call(w):
    depth, d, d_in = w.shape
    slab = SEQ_BLOCK
    return pl.pallas_call(
        _pack_w_in_kernel,
        out_shape=jax.ShapeDtypeStruct((depth, d, N_PACK), BF16),
        grid=(depth, d // slab),
        in_specs=[pl.BlockSpec((None, slab, d_in), lambda l, i: (l, i, 0))],
        out_specs=pl.BlockSpec((None, slab, N_PACK), lambda l, i: (l, i, 0)),
        compiler_params=pltpu.CompilerParams(dimension_semantics=("parallel", "parallel")),
        name="pack_w_in",
    )(w)


def _pack_w_out(w):
    depth, _, d = w.shape
    w_r = jnp.pad(w[:, :RET_W].reshape(depth, RET_HEADS, RET_DV, d),
                  ((0, 0), (0, 0), (0, RET_HEAD_PAD - RET_DV), (0, 0))).reshape(depth, RETV_W, d)
    return jnp.concatenate([w_r, w[:, RET_W:]], axis=1).astype(BF16)


def _pad_heads(vec):
    depth = vec.shape[0]
    padded = jnp.pad(vec.reshape(depth, RET_HEADS, RET_DV), ((0, 0), (0, 0), (0, RET_HEAD_PAD - RET_DV)))
    return padded.reshape(depth, 1, RETV_W)


def _block_diag(pw):
    g = len(POOL_WINDOWS)
    eye = jnp.eye(g, dtype=pw.dtype)
    return jnp.einsum("lgij,gh->lgihj", pw, eye).reshape(pw.shape[0], POOL_W, POOL_W).astype(BF16)


def _cast_kernel(x_ref, o_ref):
    o_ref[...] = x_ref[...].astype(o_ref.dtype)


def _to_bf16(w):
    depth, k, n = w.shape
    slab = max(s for s in range(16, k + 1, 16) if k % s == 0 and s * n <= CAST_SLAB_ELEMS)
    spec = pl.BlockSpec((None, slab, n), lambda l, i: (l, i, 0))
    return pl.pallas_call(
        _cast_kernel,
        out_shape=jax.ShapeDtypeStruct(w.shape, BF16),
        grid=(depth, k // slab),
        in_specs=[spec],
        out_specs=spec,
        compiler_params=pltpu.CompilerParams(dimension_semantics=("parallel", "parallel")),
        name="weights_to_bf16",
    )(w)


def _prepare_params(w_in, b_f, ret_gn_g, pool_w, pool_scale, w_out, ln1_g, ln1_b,
                    w_ffn1, w_ffn3, w_ffn2, ln2_g, ln2_b):
    depth = w_in.shape[0]
    vec = lambda a: a.reshape(depth, 1, a.shape[-1])
    return dict(
        w_in=_pack_w_in_call(w_in),
        gn_g=_pad_heads(ret_gn_g),
        b_f=vec(jnp.pad(b_f, ((0, 0), (0, LANES - FOX_HEADS)))),
        pool_w=_block_diag(pool_w),
        pool_scale=vec(pool_scale),
        w_out=_pack_w_out(w_out),
        ln1_g=vec(ln1_g), ln1_b=vec(ln1_b), ln2_g=vec(ln2_g), ln2_b=vec(ln2_b),
        w1=_to_bf16(w_ffn1), w3=_to_bf16(w_ffn3), w2=_to_bf16(w_ffn2))


def _retention_tables():
    ch = SEQ_BLOCK
    gamma = (1.0 - 2.0 ** (-5.0 - np.arange(RET_HEADS, dtype=np.float32))).astype(np.float32)
    lg = np.log(gamma).astype(np.float32)
    i = np.arange(ch, dtype=np.float32)
    diff = i[:, None] - i[None, :]
    dm = np.where(diff >= 0, np.exp(lg[:, None, None] * np.maximum(diff, 0.0)), 0.0).astype(np.float32)
    xi = np.exp(lg[:, None] * (i + 1.0)).astype(np.float32)
    zeta = np.exp(lg[:, None] * (ch - 1.0 - i)).astype(np.float32)
    lane = np.arange(2 * LANES)
    within = lane % LANES
    lane_head = np.where(within < RET_HEADS * RET_HALF, within // RET_HALF, -1)
    xiq = np.zeros((ch, 2 * LANES), np.float32)
    zk = np.zeros((ch, 2 * LANES), np.float32)
    bm = np.zeros((2 * LANES, RETV_W), np.float32)
    for h in range(RET_HEADS):
        sel = lane_head == h
        xiq[:, sel] = xi[h][:, None]
        zk[:, sel] = zeta[h][:, None]
        bm[sel, h * LANES:(h + 1) * LANES] = 1.0
    dec = np.repeat(np.exp(lg * ch).astype(np.float32), LANES)[None, :]
    return dict(xiq=jnp.asarray(xiq), zk=jnp.asarray(zk), dm=jnp.asarray(dm),
                dec=jnp.asarray(dec), bm=jnp.asarray(bm))


def _rotary_tables(lp):
    pos = jnp.arange(lp, dtype=F32)
    inv_freq = ROPE_BASE ** (-jnp.arange(RET_HALF, dtype=F32) / RET_HALF)
    ang = pos[:, None] * inv_freq[None, :]
    pad = LANES - RET_HEADS * RET_HALF
    cos = jnp.pad(jnp.tile(jnp.cos(ang), (1, RET_HEADS)), ((0, 0), (0, pad)))
    sin = jnp.pad(jnp.tile(jnp.sin(ang), (1, RET_HEADS)), ((0, 0), (0, pad)))
    return cos, sin


def kernel(x, meta, ln_emb_g, ln_emb_b, w_in, b_f, ret_gn_g, pool_w, pool_scale, w_out, ln1_g, ln1_b,
           w_ffn1, w_ffn3, w_ffn2, ln2_g, ln2_b):
    bsz, seq, d = x.shape
    assert d == D_MODEL and seq % SEQ_BLOCK == 0
    depth = w_in.shape[0]
    assert depth == DEPTH
    lp = seq + SEQ_BLOCK
    rows = bsz * lp
    tm = ROW_TILE if rows % ROW_TILE == 0 else SEQ_BLOCK

    consts = _retention_tables()
    consts["cos"], consts["sin"] = _rotary_tables(lp)
    params = _prepare_params(w_in, b_f, ret_gn_g, pool_w, pool_scale, w_out, ln1_g, ln1_b,
                             w_ffn1, w_ffn3, w_ffn2, ln2_g, ln2_b)

    h, ret, qkv, flog = _embed_inproj(x, meta, ln_emb_g, ln_emb_b, params["w_in"], lp)
    for l in range(depth):
        if l > 0:
            ret, qkv, flog = _inproj(h, params["w_in"], l, tm)
        o_r, o_p, cum = _seq_mix(ret, flog, consts, params, l, bsz, lp)
        o_f = _fox(qkv, cum, bsz, lp)
        h = _mix_ffn(o_r, o_p, o_f, h, params, l, tm)
    return h.reshape(bsz, lp, d)[:, N_META:N_META + seq]
```

```python
import functools

import numpy as np
import jax
import jax.numpy as jnp
from jax import lax
from jax.experimental import pallas as pl
from jax.experimental.pallas import tpu as pltpu

F32 = jnp.float32
BF16 = jnp.bfloat16

D_MODEL = 1024
N_META = 16
RET_HEADS = 4
RET_DK = 48
RET_HALF = RET_DK // 2
RET_DV = 96
RET_QK = RET_HEADS * RET_DK
RET_W = RET_HEADS * RET_DV
POOL_WINDOWS = (2, 4, 8, 16)
POOL_GROUP = 64
POOL_W = len(POOL_WINDOWS) * POOL_GROUP
FOX_HEADS = 6
FOX_DH = 64
FOX_W = FOX_HEADS * FOX_DH
D_FF = 2816
ROPE_BASE = 10000.0
LN_EPS = 1e-5
NEG_INF = -1e30
DEPTH = 2
ALPHA = (2.0 * DEPTH) ** 0.25
LOG2E = 1.4426950408889634

LANES = 128
MXU_DIM = 256
SEQ_BLOCK = 256
ROW_TILE = 1024
VMEM_LIMIT = 56 * 1024 * 1024

RET_HEAD_PAD = LANES
QR_OFF = 0
KR_OFF = 2 * LANES
VR_OFF = 4 * LANES
GR_OFF = VR_OFF + RET_HEADS * RET_HEAD_PAD
UP_OFF = GR_OFF + RET_HEADS * RET_HEAD_PAD
QF_OFF = UP_OFF + POOL_W
KF_OFF = QF_OFF + FOX_W
VF_OFF = KF_OFF + FOX_W
FL_OFF = VF_OFF + FOX_W
RET_COLS = QF_OFF
FOX_COLS = 3 * FOX_W
N_PACK = FL_OFF + LANES
RETV_W = RET_HEADS * RET_HEAD_PAD
MIX_W = RETV_W + POOL_W + FOX_W
PROJ_CHUNK = 512
FF_CHUNK = 256
CAST_SLAB_ELEMS = 768 * 1024
SEQ_PAR = 2
MIX_FFN_SPLIT = 2


def _layer_norm(x, g, b):
    mu = jnp.mean(x, axis=-1, keepdims=True)
    d = x - mu
    var = jnp.mean(d * d, axis=-1, keepdims=True)
    return d * lax.rsqrt(var + LN_EPS) * g + b


def _sigmoid(x):
    return 1.0 / (1.0 + jnp.exp(-x))


def _project(xb, w_ref, cs_ref, ret_ref, fox_ref, flog_ref):
    def cols(c0, width):
        r = jnp.dot(xb, w_ref[:, c0:c0 + width], preferred_element_type=F32)
        return r * cs_ref[:, c0:c0 + width]

    for c0 in range(0, RET_COLS, PROJ_CHUNK):
        width = min(PROJ_CHUNK, RET_COLS - c0)
        ret_ref[:, c0:c0 + width] = cols(c0, width).astype(BF16)
    for c0 in range(0, FOX_COLS - LANES, PROJ_CHUNK):
        fox_ref[:, c0:c0 + PROJ_CHUNK] = cols(QF_OFF + c0, PROJ_CHUNK).astype(BF16)
    tail = cols(FL_OFF - LANES, 2 * LANES)
    fox_ref[:, FOX_COLS - LANES:] = tail[:, :LANES].astype(BF16)
    flog_ref[...] = tail[:, LANES:]


def _embed_inproj_kernel(meta_ref, xm_ref, xe_ref, g_ref, b_ref, w_ref, cs_ref,
                         h_ref, ret_ref, fox_ref, flog_ref):
    j = pl.program_id(1)
    last = pl.num_programs(1) - 1
    top = jnp.where(j == 0, meta_ref[...], xe_ref[...])
    body = jnp.where(j == last, 0.0, xm_ref[0:SEQ_BLOCK - N_META, :])
    rows = jnp.concatenate([top, body], axis=0)
    h = _layer_norm(rows, g_ref[...], b_ref[...])
    h_ref[...] = h
    _project(h.astype(BF16), w_ref, cs_ref, ret_ref, fox_ref, flog_ref)


def _embed_inproj(x, meta, g, b, w, lp):
    bsz, seq, d = x.shape
    nblk = lp // SEQ_BLOCK
    n_xblk = seq // SEQ_BLOCK
    per = SEQ_BLOCK // N_META
    rows = bsz * lp
    out_row = lambda width: pl.BlockSpec((SEQ_BLOCK, width), lambda bb, j: (bb * nblk + j, 0))
    return pl.pallas_call(
        _embed_inproj_kernel,
        out_shape=(jax.ShapeDtypeStruct((rows, d), F32),
                   jax.ShapeDtypeStruct((rows, RET_COLS), BF16),
                   jax.ShapeDtypeStruct((rows, FOX_COLS), BF16),
                   jax.ShapeDtypeStruct((rows, LANES), F32)),
        grid=(bsz, nblk),
        in_specs=[
            pl.BlockSpec((N_META, d), lambda bb, j: (0, 0)),
            pl.BlockSpec((None, SEQ_BLOCK, d), lambda bb, j: (bb, jnp.minimum(j, n_xblk - 1), 0)),
            pl.BlockSpec((None, N_META, d), lambda bb, j: (bb, jnp.maximum(per * j - 1, 0), 0)),
            pl.BlockSpec((1, d), lambda bb, j: (0, 0)),
            pl.BlockSpec((1, d), lambda bb, j: (0, 0)),
            pl.BlockSpec((None, D_MODEL, N_PACK), lambda bb, j: (0, 0, 0)),
            pl.BlockSpec((1, N_PACK), lambda bb, j: (0, 0)),
        ],
        out_specs=(out_row(d), out_row(RET_COLS), out_row(FOX_COLS), out_row(LANES)),
        compiler_params=pltpu.CompilerParams(dimension_semantics=("parallel", "arbitrary"),
                                             vmem_limit_bytes=VMEM_LIMIT),
        name="embed_in_proj",
    )(meta, x, x, g.reshape(1, d), b.reshape(1, d), w, _inproj_col_scale())


def _inproj_kernel(h_ref, w_ref, cs_ref, ret_ref, fox_ref, flog_ref):
    _project(h_ref[...].astype(BF16), w_ref, cs_ref, ret_ref, fox_ref, flog_ref)


def _inproj_col_scale():
    cs = np.ones((1, N_PACK), np.float32)
    cs[0, KR_OFF:KR_OFF + 2 * LANES] = RET_DK ** -0.5
    cs[0, QF_OFF:QF_OFF + FOX_W] = FOX_DH ** -0.5 * LOG2E
    return jnp.asarray(cs)


def _inproj(h, w, layer, tm):
    rows = h.shape[0]
    return pl.pallas_call(
        _inproj_kernel,
        out_shape=(jax.ShapeDtypeStruct((rows, RET_COLS), BF16),
                   jax.ShapeDtypeStruct((rows, FOX_COLS), BF16),
                   jax.ShapeDtypeStruct((rows, LANES), F32)),
        grid=(rows // tm,),
        in_specs=[
            pl.BlockSpec((tm, D_MODEL), lambda i: (i, 0)),
            pl.BlockSpec((None, D_MODEL, N_PACK), lambda i: (layer, 0, 0)),
            pl.BlockSpec((1, N_PACK), lambda i: (0, 0)),
        ],
        out_specs=(pl.BlockSpec((tm, RET_COLS), lambda i: (i, 0)),
                   pl.BlockSpec((tm, FOX_COLS), lambda i: (i, 0)),
                   pl.BlockSpec((tm, LANES), lambda i: (i, 0))),
        compiler_params=pltpu.CompilerParams(dimension_semantics=("parallel",),
                                             vmem_limit_bytes=VMEM_LIMIT),
        name="in_proj",
    )(h, w, _inproj_col_scale())


def _seq_kernel(qk_ref, v_ref, g_ref, u_ref, fl_ref, cos_ref, sin_ref,
                xiq_ref, zk_ref, dm_ref, dec_ref, bm_ref, gng_ref, bf_ref, pw_ref, ps_ref,
                or_ref, op_ref, c_ref,
                state_sc, tail_sc, carry_sc):
    for s in range(qk_ref.shape[0]):
        _seq_one(qk_ref.at[s], v_ref.at[s], g_ref.at[s], u_ref.at[s], fl_ref.at[s], cos_ref, sin_ref,
                 xiq_ref, zk_ref, dm_ref, dec_ref, bm_ref, gng_ref, bf_ref, pw_ref, ps_ref,
                 or_ref.at[s], op_ref.at[s], c_ref.at[s],
                 state_sc.at[s], tail_sc.at[s], carry_sc.at[s])


def _seq_one(qk_ref, v_ref, g_ref, u_ref, fl_ref, cos_ref, sin_ref,
             xiq_ref, zk_ref, dm_ref, dec_ref, bm_ref, gng_ref, bf_ref, pw_ref, ps_ref,
             or_ref, op_ref, c_ref,
             state_sc, tail_sc, carry_sc):
    j = pl.program_id(1)
    ch = SEQ_BLOCK

    @pl.when(j == 0)
    def _():
        state_sc[...] = jnp.zeros_like(state_sc)
        tail_sc[...] = jnp.zeros_like(tail_sc)
        carry_sc[...] = jnp.zeros_like(carry_sc)

    qk = qk_ref[...].astype(F32)
    cs = cos_ref[...]
    sn = sin_ref[...]
    q1, q2 = qk[:, 0:LANES], qk[:, LANES:2 * LANES]
    k1, k2 = qk[:, 2 * LANES:3 * LANES], qk[:, 3 * LANES:4 * LANES]
    qr = jnp.concatenate([q1 * cs - q2 * sn, q1 * sn + q2 * cs], axis=-1)
    kr = jnp.concatenate([k1 * cs - k2 * sn, k1 * sn + k2 * cs], axis=-1)
    qb = qr.astype(BF16)
    qx = (qr * xiq_ref[...]).astype(BF16)
    kb = kr.astype(BF16)
    kz = (kr * zk_ref[...]).astype(BF16)
    v = v_ref[...]
    st = state_sc[...]
    cross = jnp.dot(qx, st.astype(BF16), preferred_element_type=F32)
    qlane = lax.broadcasted_iota(jnp.int32, (1, 2 * LANES), 1)
    qhead = jnp.where(qlane % LANES < RET_HEADS * RET_HALF, (qlane % LANES) // RET_HALF, RET_HEADS)
    inner = []
    for h in range(RET_HEADS):
        qh = jnp.where(qhead == h, qb, jnp.zeros_like(qb))
        s = lax.dot_general(qh, kb, (((1,), (1,)), ((), ())), preferred_element_type=F32)
        p = (s * dm_ref[h]).astype(BF16)
        inner.append(jnp.dot(p, v[:, h * LANES:(h + 1) * LANES], preferred_element_type=F32))
    o = jnp.concatenate(inner, axis=-1) + cross
    kv = lax.dot_general(kz, v, (((0,), (0,)), ((), ())), preferred_element_type=F32)
    state_sc[...] = st * dec_ref[...] + kv * bm_ref[...]

    vlane = lax.broadcasted_iota(jnp.int32, (1, LANES), 1) < RET_DV
    normed = []
    for h in range(RET_HEADS):
        xh = o[:, h * LANES:(h + 1) * LANES]
        mu = jnp.sum(xh, axis=-1, keepdims=True) * (1.0 / RET_DV)
        d = jnp.where(vlane, xh - mu, 0.0)
        var = jnp.sum(d * d, axis=-1, keepdims=True) * (1.0 / RET_DV)
        normed.append(d * lax.rsqrt(var + LN_EPS))
    y = jnp.concatenate(normed, axis=-1) * gng_ref[...]
    gate = g_ref[...].astype(F32)
    or_ref[...] = (gate * _sigmoid(gate) * y).astype(BF16)

    u = u_ref[...].astype(F32)
    tail_rows = tail_sc.shape[0]
    ext = jnp.concatenate([tail_sc[...], u], axis=0)
    tail_sc[...] = u[ch - tail_rows:, :]
    e2 = ext + pltpu.roll(ext, 1, 0)
    e4 = e2 + pltpu.roll(e2, 2, 0)
    e8 = e4 + pltpu.roll(e4, 4, 0)
    e16 = e8 + pltpu.roll(e8, 8, 0)
    glane = lax.broadcasted_iota(jnp.int32, (1, POOL_W), 1) // POOL_GROUP
    win = jnp.where(glane == 0, e2, jnp.where(glane == 1, e4, jnp.where(glane == 2, e8, e16)))
    win = win[tail_rows:, :]
    wlen = jnp.where(glane == 0, 2, jnp.where(glane == 1, 4, jnp.where(glane == 2, 8, 16)))
    pos = j * ch + lax.broadcasted_iota(jnp.int32, (ch, POOL_W), 0)
    cnt = jnp.minimum(pos + 1, wlen).astype(F32)
    pooled = (win / cnt - u).astype(BF16)
    yp = jnp.dot(pooled, pw_ref[...], preferred_element_type=F32) * ps_ref[...]
    op_ref[...] = yp.astype(BF16)

    z = (fl_ref[...] + bf_ref[...]).T[0:8, :]
    logf = jnp.minimum(z, 0.0) - jnp.log1p(jnp.exp(-jnp.abs(z)))
    pos_in = lax.broadcasted_iota(jnp.int32, (8, ch), 1)
    sh = 1
    while sh < ch:
        logf = logf + jnp.where(pos_in >= sh, pltpu.roll(logf, sh, 1), 0.0)
        sh *= 2
    c_t = logf + jnp.concatenate([carry_sc[...]] * (ch // LANES), axis=1)
    carry_sc[...] = jnp.broadcast_to(c_t[:, ch - 1:ch], carry_sc.shape)
    c = jnp.concatenate([c_t, jnp.zeros((LANES - 8, ch), F32)], axis=0).T
    c_ref[...] = c * LOG2E


def _seq_mix(proj, flog, consts, params, layer, bsz, lp):
    ch = SEQ_BLOCK
    nc = lp // ch
    rows = bsz * lp

    par = SEQ_PAR if bsz % SEQ_PAR == 0 else 1

    def rowblk(width, colblk):
        return pl.BlockSpec((par, ch, width), lambda g, j: (g, j, colblk))

    def const(shape):
        nd = len(shape)
        return pl.BlockSpec(shape, lambda bb, j: (0,) * nd)

    def layer_param(shape):
        return pl.BlockSpec((None,) + shape, lambda bb, j: (layer,) + (0,) * len(shape))

    in_specs = [
        rowblk(4 * LANES, QR_OFF // (4 * LANES)),
        rowblk(RETV_W, VR_OFF // RETV_W),
        rowblk(RETV_W, GR_OFF // RETV_W),
        rowblk(POOL_W, UP_OFF // POOL_W),
        rowblk(LANES, 0),
        pl.BlockSpec((ch, LANES), lambda bb, j: (j, 0)),
        pl.BlockSpec((ch, LANES), lambda bb, j: (j, 0)),
        const((ch, 2 * LANES)), const((ch, 2 * LANES)), const((RET_HEADS, ch, ch)),
        const((1, RETV_W)), const((2 * LANES, RETV_W)),
        layer_param((1, RETV_W)), layer_param((1, LANES)), layer_param((POOL_W, POOL_W)),
        layer_param((1, POOL_W)),
    ]
    out_shape = (jax.ShapeDtypeStruct((bsz, lp, RETV_W), BF16),
                 jax.ShapeDtypeStruct((bsz, lp, POOL_W), BF16),
                 jax.ShapeDtypeStruct((bsz, lp, LANES), F32))
    out_specs = (rowblk(RETV_W, 0), rowblk(POOL_W, 0), rowblk(LANES, 0))
    proj = proj.reshape(bsz, lp, proj.shape[-1])
    flog = flog.reshape(bsz, lp, LANES)
    o_r, o_p, cum = pl.pallas_call(
        _seq_kernel,
        out_shape=out_shape,
        grid=(bsz // par, nc),
        in_specs=in_specs,
        out_specs=out_specs,
        scratch_shapes=[pltpu.VMEM((par, 2 * LANES, RETV_W), F32),
                        pltpu.VMEM((par, 16, POOL_W), F32),
                        pltpu.VMEM((par, 8, LANES), F32)],
        compiler_params=pltpu.CompilerParams(dimension_semantics=("parallel", "arbitrary"),
                                             vmem_limit_bytes=VMEM_LIMIT),
        name="seq_mixers",
    )(proj, proj, proj, proj, flog, consts["cos"], consts["sin"],
      consts["xiq"], consts["zk"], consts["dm"], consts["dec"], consts["bm"],
      params["gn_g"], params["b_f"], params["pool_w"], params["pool_scale"])
    return o_r.reshape(rows, RETV_W), o_p.reshape(rows, POOL_W), cum.reshape(rows, LANES)


FOX_AUG = 3
FOX_VROWS = FOX_DH + 16
FOX_LATE_HEADS = 2
FOX_UNROLL = 8


def _fox_select_matrix():
    sel = np.zeros((LANES, FOX_HEADS * LANES), np.float32)
    for a in range(FOX_AUG):
        for h in range(FOX_HEADS):
            sel[8 * a + h, h * LANES + FOX_DH + a] = -1.0
    return jnp.asarray(sel, BF16)


def _fox_kernel(q_ref, k_ref, v_ref, c_ref, sel_ref, o_ref,
                ka_sc, vt_sc, qa_sc, st_sc, m_sc, acc_sc):
    i = pl.program_id(1)
    tq = SEQ_BLOCK
    tk = SEQ_BLOCK
    npair = FOX_HEADS // 2
    lane = lax.broadcasted_iota(jnp.int32, (1, LANES), 1)

    @pl.when(i == 0)
    def _():
        def chunk(t, carry):
            r0 = pl.multiple_of(t * tk, tk)
            cc = c_ref[pl.ds(r0, tk), :]
            pieces, rem = [], cc
            for a in range(FOX_AUG):
                piece = rem.astype(BF16).astype(F32)
                pieces.append(piece if a == 0 else pltpu.roll(piece, 8 * a, 1))
                rem = rem - piece
            packed = jnp.where(lane < 8, pieces[0], jnp.where(lane < 16, pieces[1], pieces[2]))
            c_aug = jnp.dot(packed.astype(BF16), sel_ref[...], preferred_element_type=F32)
            for pr in range(npair):
                kk = k_ref[pl.ds(r0, tk), pr * LANES:(pr + 1) * LANES].astype(F32)
                vv = v_ref[pl.ds(r0, tk), pr * LANES:(pr + 1) * LANES].astype(F32)
                vtt = vv.T
                ones = jnp.ones((FOX_VROWS - FOX_DH, tk), F32)
                k_heads = (kk, pltpu.roll(kk, FOX_DH, 1))
                v_heads = (vtt[:FOX_DH], vtt[FOX_DH:])
                for hh in range(2):
                    h = 2 * pr + hh
                    aug = jnp.where(lane < FOX_DH, k_heads[hh], c_aug[:, h * LANES:(h + 1) * LANES])
                    ka_sc[h, pl.ds(r0, tk), :] = aug.astype(BF16)
                    vt_sc[h, :, pl.ds(r0, tk)] = jnp.concatenate([v_heads[hh], ones], axis=0).astype(BF16)
            return carry

        lax.fori_loop(0, k_ref.shape[0] // tk, chunk, 0)

    for pr in range(npair):
        qq = q_ref[:, pr * LANES:(pr + 1) * LANES].astype(F32)
        tail = jnp.where(lane < FOX_DH + FOX_AUG, 1.0, 0.0)
        qa_sc[2 * pr] = jnp.where(lane < FOX_DH, qq, tail).astype(BF16)
        qa_sc[2 * pr + 1] = jnp.where(lane < FOX_DH, pltpu.roll(qq, FOX_DH, 1), tail).astype(BF16)
    m_sc[...] = jnp.full(m_sc.shape, NEG_INF, F32)
    acc_sc[...] = jnp.zeros_like(acc_sc)

    def score(t, h):
        ks = t * tk if isinstance(t, int) else pl.multiple_of(t * tk, tk)
        st_sc[h] = lax.dot_general(ka_sc[h, pl.ds(ks, tk), :], qa_sc[h], (((1,), (1,)), ((), ())),
                                   preferred_element_type=F32)

    def attend(t, masked, next_t):
        ks = t * tk if isinstance(t, int) else pl.multiple_of(t * tk, tk)

        def load(h, c0):
            s = st_sc[h, :, c0:c0 + LANES]
            if masked:
                k_id = lax.broadcasted_iota(jnp.int32, (tk, LANES), 0)
                q_id = c0 + lax.broadcasted_iota(jnp.int32, (tk, LANES), 1)
                s = jnp.where(k_id <= q_id, s, NEG_INF)
            return s

        for h in range(early, FOX_HEADS):
            score(t, h)
        for h in range(FOX_HEADS):
            p_halves, a_halves = [], []
            for c0 in range(0, tq, LANES):
                m_prev = m_sc[h, :, c0:c0 + LANES]
                m_new = jnp.maximum(m_prev, jnp.max(load(h, c0), axis=0, keepdims=True))
                m_sc[h, :, c0:c0 + LANES] = m_new
                a_halves.append(jnp.exp2(m_prev - m_new))
                p_halves.append(jnp.exp2(load(h, c0) - m_new).astype(BF16))
            p_t = jnp.concatenate(p_halves, axis=1)
            pv = jnp.dot(vt_sc[h, :, pl.ds(ks, tk)], p_t, preferred_element_type=F32)
            acc_sc[h] = jnp.concatenate(a_halves, axis=1) * acc_sc[h] + pv
            if next_t is not None and h < early:
                score(next_t, h)

    early = FOX_HEADS - FOX_LATE_HEADS
    for h in range(early):
        score(0, h)

    def run(t0, count):
        for d in range(count):
            attend(t0 + d, False, t0 + d + 1)

    def body(u, carry):
        run(FOX_UNROLL * u, FOX_UNROLL)
        return carry

    lax.fori_loop(0, i // FOX_UNROLL, body, 0)
    done = (i // FOX_UNROLL) * FOX_UNROLL
    span = FOX_UNROLL // 2
    while span >= 1:
        take = (i - done) >= span

        @pl.when(take)
        def _(done=done, span=span):
            run(done, span)

        done = done + jnp.where(take, span, 0)
        span //= 2

    attend(i, True, None)

    outs = []
    for pr in range(npair):
        acc_a = acc_sc[2 * pr]
        acc_b = acc_sc[2 * pr + 1]
        o_t = jnp.concatenate([acc_a[:FOX_DH] / acc_a[FOX_DH:FOX_DH + 1],
                               acc_b[:FOX_DH] / acc_b[FOX_DH:FOX_DH + 1]], axis=0)
        outs.append(o_t.T)
    o_ref[...] = jnp.concatenate(outs, axis=-1).astype(BF16)


def _fox(qkv, cum, bsz, lp):
    tq = SEQ_BLOCK
    nq = lp // tq
    rows = bsz * lp
    return pl.pallas_call(
        _fox_kernel,
        out_shape=jax.ShapeDtypeStruct((rows, FOX_W), BF16),
        grid=(bsz, nq),
        in_specs=[
            pl.BlockSpec((tq, FOX_W), lambda bb, i: (bb * nq + i, 0)),
            pl.BlockSpec((lp, FOX_W), lambda bb, i: (bb, 1)),
            pl.BlockSpec((lp, FOX_W), lambda bb, i: (bb, 2)),
            pl.BlockSpec((lp, LANES), lambda bb, i: (bb, 0)),
            pl.BlockSpec((LANES, FOX_HEADS * LANES), lambda bb, i: (0, 0)),
        ],
        out_specs=pl.BlockSpec((tq, FOX_W), lambda bb, i: (bb * nq + i, 0)),
        scratch_shapes=[pltpu.VMEM((FOX_HEADS, lp, LANES), BF16),
                        pltpu.VMEM((FOX_HEADS, FOX_VROWS, lp), BF16),
                        pltpu.VMEM((FOX_HEADS, tq, LANES), BF16),
                        pltpu.VMEM((FOX_HEADS, SEQ_BLOCK, tq), F32),
                        pltpu.VMEM((FOX_HEADS, 1, tq), F32),
                        pltpu.VMEM((FOX_HEADS, FOX_VROWS, tq), F32)],
        compiler_params=pltpu.CompilerParams(
            dimension_semantics=("parallel", "arbitrary"),
            vmem_limit_bytes=VMEM_LIMIT),
        name="fox_attention",
    )(qkv, qkv, qkv, cum, _fox_select_matrix())


def _mix_ffn_kernel(or_ref, op_ref, of_ref, h_ref, wo_ref, g1_ref, b1_ref,
                    w1_ref, w3_ref, w2_ref, g2_ref, b2_ref, o_ref):
    tm = h_ref.shape[0]
    sub = tm // MIX_FFN_SPLIT
    parts = [slice(s * sub, (s + 1) * sub) for s in range(MIX_FFN_SPLIT)]
    ys = []
    for rows in parts:
        mix = jnp.concatenate([or_ref[rows, :], op_ref[rows, :], of_ref[rows, :]], axis=-1)
        ys.append(ALPHA * h_ref[rows, :] + jnp.dot(mix, wo_ref[...], preferred_element_type=F32))
    h1s = [_layer_norm(y, g1_ref[...], b1_ref[...]) for y in ys]
    accs = []
    for h1 in h1s:
        xb = h1.astype(BF16)
        acc = None
        for c0 in range(0, D_FF, FF_CHUNK):
            a = jnp.dot(xb, w1_ref[:, c0:c0 + FF_CHUNK], preferred_element_type=F32)
            b = jnp.dot(xb, w3_ref[:, c0:c0 + FF_CHUNK], preferred_element_type=F32)
            t = (a * _sigmoid(a) * b).astype(BF16)
            part = jnp.dot(t, w2_ref[c0:c0 + FF_CHUNK, :], preferred_element_type=F32)
            acc = part if acc is None else acc + part
        accs.append(acc)
    for rows, h1, acc in zip(parts, h1s, accs):
        o_ref[rows, :] = _layer_norm(ALPHA * h1 + acc, g2_ref[...], b2_ref[...])


def _mix_ffn(o_r, o_p, o_f, h, params, layer, tm):
    rows = h.shape[0]

    def row(width):
        return pl.BlockSpec((tm, width), lambda i: (i, 0))

    def resident(shape):
        return pl.BlockSpec((None,) + shape, lambda i: (layer,) + (0,) * len(shape),
                            pipeline_mode=pl.Buffered(1))

    vec = resident((1, D_MODEL))
    return pl.pallas_call(
        _mix_ffn_kernel,
        out_shape=jax.ShapeDtypeStruct((rows, D_MODEL), F32),
        grid=(rows // tm,),
        in_specs=[row(RETV_W), row(POOL_W), row(FOX_W), row(D_MODEL),
                  resident((MIX_W, D_MODEL)), vec, vec,
                  resident((D_MODEL, D_FF)), resident((D_MODEL, D_FF)), resident((D_FF, D_MODEL)), vec, vec],
        out_specs=row(D_MODEL),
        compiler_params=pltpu.CompilerParams(dimension_semantics=("parallel",),
                                             vmem_limit_bytes=VMEM_LIMIT),
        name="mix_ffn_ln",
    )(o_r, o_p, o_f, h, params["w_out"], params["ln1_g"], params["ln1_b"],
      params["w1"], params["w3"], params["w2"], params["ln2_g"], params["ln2_b"])


def _pack_w_in(w):
    lead = w.shape[:-1]

    def rot_pack(seg):
        s = seg.reshape(lead + (RET_HEADS, RET_DK))
        x1 = s[..., :RET_HALF].reshape(lead + (RET_HEADS * RET_HALF,))
        x2 = s[..., RET_HALF:].reshape(lead + (RET_HEADS * RET_HALF,))
        z = jnp.zeros(lead + (LANES - RET_HEADS * RET_HALF,), w.dtype)
        return jnp.concatenate([x1, z, x2, z], axis=-1)

    def head_pad(seg):
        s = seg.reshape(lead + (RET_HEADS, RET_DV))
        widths = [(0, 0)] * len(lead) + [(0, 0), (0, RET_HEAD_PAD - RET_DV)]
        return jnp.pad(s, widths).reshape(lead + (RETV_W,))

    o = 0
    segs = []
    for sz in (RET_QK, RET_QK, RET_W, RET_W, POOL_W, FOX_W, FOX_W, FOX_W, FOX_HEADS):
        segs.append(w[..., o:o + sz])
        o += sz
    q_r, k_r, v_r, g_r, u_p, q_f, k_f, v_f, f_l = segs
    f_pad = jnp.pad(f_l, [(0, 0)] * len(lead) + [(0, LANES - FOX_HEADS)])
    packed = jnp.concatenate(
        [rot_pack(q_r), rot_pack(k_r), head_pad(v_r), head_pad(g_r), u_p, q_f, k_f, v_f, f_pad], axis=-1)
    return packed.astype(BF16)


def _pack_w_in_kernel(w_ref, o_ref):
    w = w_ref[...]
    rows = w.shape[0]
    pad = jnp.zeros((rows, LANES - RET_HEADS * RET_HALF), w.dtype)
    pieces = []
    for base in (0, RET_QK):
        for half in range(2):
            for h in range(RET_HEADS):
                lo = base + h * RET_DK + half * RET_HALF
                pieces.append(w[:, lo:lo + RET_HALF])
            pieces.append(pad)
    for base in (2 * RET_QK, 2 * RET_QK + RET_W):
        for h in range(RET_HEADS):
            pieces.append(w[:, base + h * RET_DV:base + (h + 1) * RET_DV])
            pieces.append(pad)
    aligned = 2 * RET_QK + 2 * RET_W
    pieces.append(w[:, aligned:aligned + POOL_W + 3 * FOX_W])
    pieces.append(w[:, aligned + POOL_W + 3 * FOX_W:])
    pieces.append(jnp.zeros((rows, LANES - FOX_HEADS), w.dtype))
    o_ref[...] = jnp.concatenate(pieces, axis=1).astype(BF16)


def _pack_w_in_call(w):
    depth, d, d_in = w.shape
    slab = SEQ_BLOCK
    return pl.pallas_call(
        _pack_w_in_kernel,
        out_shape=jax.ShapeDtypeStruct((depth, d, N_PACK), BF16),
        grid=(depth, d // slab),
        in_specs=[pl.BlockSpec((None, slab, d_in), lambda l, i: (l, i, 0))],
        out_specs=pl.BlockSpec((None, slab, N_PACK), lambda l, i: (l, i, 0)),
        compiler_params=pltpu.CompilerParams(dimension_semantics=("parallel", "parallel")),
        name="pack_w_in",
    )(w)


def _pack_w_out(w):
    depth, _, d = w.shape
    w_r = jnp.pad(w[:, :RET_W].reshape(depth, RET_HEADS, RET_DV, d),
                  ((0, 0), (0, 0), (0, RET_HEAD_PAD - RET_DV), (0, 0))).reshape(depth, RETV_W, d)
    return jnp.concatenate([w_r, w[:, RET_W:]], axis=1).astype(BF16)


def _pad_heads(vec):
    depth = vec.shape[0]
    padded = jnp.pad(vec.reshape(depth, RET_HEADS, RET_DV), ((0, 0), (0, 0), (0, RET_HEAD_PAD - RET_DV)))
    return padded.reshape(depth, 1, RETV_W)


def _block_diag(pw):
    g = len(POOL_WINDOWS)
    eye = jnp.eye(g, dtype=pw.dtype)
    return jnp.einsum("lgij,gh->lgihj", pw, eye).reshape(pw.shape[0], POOL_W, POOL_W).astype(BF16)


def _cast_kernel(x_ref, o_ref):
    o_ref[...] = x_ref[...].astype(o_ref.dtype)


def _to_bf16(w):
    depth, k, n = w.shape
    slab = max(s for s in range(16, k + 1, 16) if k % s == 0 and s * n <= CAST_SLAB_ELEMS)
    spec = pl.BlockSpec((None, slab, n), lambda l, i: (l, i, 0))
    return pl.pallas_call(
        _cast_kernel,
        out_shape=jax.ShapeDtypeStruct(w.shape, BF16),
        grid=(depth, k // slab),
        in_specs=[spec],
        out_specs=spec,
        compiler_params=pltpu.CompilerParams(dimension_semantics=("parallel", "parallel")),
        name="weights_to_bf16",
    )(w)


def _prepare_params(w_in, b_f, ret_gn_g, pool_w, pool_scale, w_out, ln1_g, ln1_b,
                    w_ffn1, w_ffn3, w_ffn2, ln2_g, ln2_b):
    depth = w_in.shape[0]
    vec = lambda a: a.reshape(depth, 1, a.shape[-1])
    return dict(
        w_in=_pack_w_in_call(w_in),
        gn_g=_pad_heads(ret_gn_g),
        b_f=vec(jnp.pad(b_f, ((0, 0), (0, LANES - FOX_HEADS)))),
        pool_w=_block_diag(pool_w),
        pool_scale=vec(pool_scale),
        w_out=_pack_w_out(w_out),
        ln1_g=vec(ln1_g), ln1_b=vec(ln1_b), ln2_g=vec(ln2_g), ln2_b=vec(ln2_b),
        w1=_to_bf16(w_ffn1), w3=_to_bf16(w_ffn3), w2=_to_bf16(w_ffn2))


def _retention_tables():
    ch = SEQ_BLOCK
    gamma = (1.0 - 2.0 ** (-5.0 - np.arange(RET_HEADS, dtype=np.float32))).astype(np.float32)
    lg = np.log(gamma).astype(np.float32)
    i = np.arange(ch, dtype=np.float32)
    diff = i[:, None] - i[None, :]
    dm = np.where(diff >= 0, np.exp(lg[:, None, None] * np.maximum(diff, 0.0)), 0.0).astype(np.float32)
    xi = np.exp(lg[:, None] * (i + 1.0)).astype(np.float32)
    zeta = np.exp(lg[:, None] * (ch - 1.0 - i)).astype(np.float32)
    lane = np.arange(2 * LANES)
    within = lane % LANES
    lane_head = np.where(within < RET_HEADS * RET_HALF, within // RET_HALF, -1)
    xiq = np.zeros((ch, 2 * LANES), np.float32)
    zk = np.zeros((ch, 2 * LANES), np.float32)
    bm = np.zeros((2 * LANES, RETV_W), np.float32)
    for h in range(RET_HEADS):
        sel = lane_head == h
        xiq[:, sel] = xi[h][:, None]
        zk[:, sel] = zeta[h][:, None]
        bm[sel, h * LANES:(h + 1) * LANES] = 1.0
    dec = np.repeat(np.exp(lg * ch).astype(np.float32), LANES)[None, :]
    return dict(xiq=jnp.asarray(xiq), zk=jnp.asarray(zk), dm=jnp.asarray(dm),
                dec=jnp.asarray(dec), bm=jnp.asarray(bm))


def _rotary_tables(lp):
    pos = jnp.arange(lp, dtype=F32)
    inv_freq = ROPE_BASE ** (-jnp.arange(RET_HALF, dtype=F32) / RET_HALF)
    ang = pos[:, None] * inv_freq[None, :]
    pad = LANES - RET_HEADS * RET_HALF
    cos = jnp.pad(jnp.tile(jnp.cos(ang), (1, RET_HEADS)), ((0, 0), (0, pad)))
    sin = jnp.pad(jnp.tile(jnp.sin(ang), (1, RET_HEADS)), ((0, 0), (0, pad)))
    return cos, sin


def kernel(x, meta, ln_emb_g, ln_emb_b, w_in, b_f, ret_gn_g, pool_w, pool_scale, w_out, ln1_g, ln1_b,
           w_ffn1, w_ffn3, w_ffn2, ln2_g, ln2_b):
    bsz, seq, d = x.shape
    assert d == D_MODEL and seq % SEQ_BLOCK == 0
    depth = w_in.shape[0]
    assert depth == DEPTH
    lp = seq + SEQ_BLOCK
    rows = bsz * lp
    tm = ROW_TILE if rows % ROW_TILE == 0 else SEQ_BLOCK

    consts = _retention_tables()
    consts["cos"], consts["sin"] = _rotary_tables(lp)
    params = _prepare_params(w_in, b_f, ret_gn_g, pool_w, pool_scale, w_out, ln1_g, ln1_b,
                             w_ffn1, w_ffn3, w_ffn2, ln2_g, ln2_b)

    h, ret, qkv, flog = _embed_inproj(x, meta, ln_emb_g, ln_emb_b, params["w_in"], lp)
    for l in range(depth):
        if l > 0:
            ret, qkv, flog = _inproj(h, params["w_in"], l, tm)
        o_r, o_p, cum = _seq_mix(ret, flog, consts, params, l, bsz, lp)
        o_f = _fox(qkv, cum, bsz, lp)
        h = _mix_ffn(o_r, o_p, o_f, h, params, l, tm)
    return h.reshape(bsz, lp, d)[:, N_META:N_META + seq]
```

```python
import functools

import numpy as np
import jax
import jax.numpy as jnp
from jax import lax
from jax.experimental import pallas as pl
from jax.experimental.pallas import tpu as pltpu

F32 = jnp.float32
BF16 = jnp.bfloat16

D_MODEL = 1024
N_META = 16
RET_HEADS = 4
RET_DK = 48
RET_HALF = RET_DK // 2
RET_DV = 96
RET_QK = RET_HEADS * RET_DK
RET_W = RET_HEADS * RET_DV
POOL_WINDOWS = (2, 4, 8, 16)
POOL_GROUP = 64
POOL_W = len(POOL_WINDOWS) * POOL_GROUP
FOX_HEADS = 6
FOX_DH = 64
FOX_W = FOX_HEADS * FOX_DH
D_FF = 2816
ROPE_BASE = 10000.0
LN_EPS = 1e-5
NEG_INF = -1e30
DEPTH = 2
ALPHA = (2.0 * DEPTH) ** 0.25
LOG2E = 1.4426950408889634

LANES = 128
MXU_DIM = 256
SEQ_BLOCK = 256
ROW_TILE = 1024
VMEM_LIMIT = 56 * 1024 * 1024

RET_HEAD_PAD = LANES
QR_OFF = 0
KR_OFF = 2 * LANES
VR_OFF = 4 * LANES
GR_OFF = VR_OFF + RET_HEADS * RET_HEAD_PAD
UP_OFF = GR_OFF + RET_HEADS * RET_HEAD_PAD
QF_OFF = UP_OFF + POOL_W
KF_OFF = QF_OFF + FOX_W
VF_OFF = KF_OFF + FOX_W
FL_OFF = VF_OFF + FOX_W
RET_COLS = QF_OFF
FOX_COLS = 3 * FOX_W
N_PACK = FL_OFF + LANES
RETV_W = RET_HEADS * RET_HEAD_PAD
MIX_W = RETV_W + POOL_W + FOX_W
PROJ_CHUNK = 512
FF_CHUNK = 256
CAST_SLAB_ELEMS = 768 * 1024
SEQ_PAR = 2
MIX_FFN_SPLIT = 2


def _layer_norm(x, g, b):
    mu = jnp.mean(x, axis=-1, keepdims=True)
    d = x - mu
    var = jnp.mean(d * d, axis=-1, keepdims=True)
    return d * lax.rsqrt(var + LN_EPS) * g + b


def _sigmoid(x):
    return 1.0 / (1.0 + jnp.exp(-x))


def _project(xb, w_ref, cs_ref, ret_ref, fox_ref, flog_ref):
    def cols(c0, width):
        r = lax.dot_general(xb, w_ref[c0:c0 + width, :], (((1,), (1,)), ((), ())),
                            preferred_element_type=F32)
        return r * cs_ref[:, c0:c0 + width]

    for c0 in range(0, RET_COLS, PROJ_CHUNK):
        width = min(PROJ_CHUNK, RET_COLS - c0)
        ret_ref[:, c0:c0 + width] = cols(c0, width).astype(BF16)
    for c0 in range(0, FOX_COLS - LANES, PROJ_CHUNK):
        fox_ref[:, c0:c0 + PROJ_CHUNK] = cols(QF_OFF + c0, PROJ_CHUNK).astype(BF16)
    tail = cols(FL_OFF - LANES, 2 * LANES)
    fox_ref[:, FOX_COLS - LANES:] = tail[:, :LANES].astype(BF16)
    flog_ref[...] = tail[:, LANES:]


def _embed_inproj_kernel(meta_ref, xm_ref, xe_ref, g_ref, b_ref, w_ref, cs_ref,
                         h_ref, ret_ref, fox_ref, flog_ref):
    j = pl.program_id(1)
    last = pl.num_programs(1) - 1
    top = jnp.where(j == 0, meta_ref[...], xe_ref[...])
    body = jnp.where(j == last, 0.0, xm_ref[0:SEQ_BLOCK - N_META, :])
    rows = jnp.concatenate([top, body], axis=0)
    h = _layer_norm(rows, g_ref[...], b_ref[...])
    h_ref[...] = h
    _project(h.astype(BF16), w_ref, cs_ref, ret_ref, fox_ref, flog_ref)


def _embed_inproj(x, meta, g, b, w, lp):
    bsz, seq, d = x.shape
    nblk = lp // SEQ_BLOCK
    n_xblk = seq // SEQ_BLOCK
    per = SEQ_BLOCK // N_META
    rows = bsz * lp
    out_row = lambda width: pl.BlockSpec((SEQ_BLOCK, width), lambda bb, j: (bb * nblk + j, 0))
    return pl.pallas_call(
        _embed_inproj_kernel,
        out_shape=(jax.ShapeDtypeStruct((rows, d), F32),
                   jax.ShapeDtypeStruct((rows, RET_COLS), BF16),
                   jax.ShapeDtypeStruct((rows, FOX_COLS), BF16),
                   jax.ShapeDtypeStruct((rows, LANES), F32)),
        grid=(bsz, nblk),
        in_specs=[
            pl.BlockSpec((N_META, d), lambda bb, j: (0, 0)),
            pl.BlockSpec((None, SEQ_BLOCK, d), lambda bb, j: (bb, jnp.minimum(j, n_xblk - 1), 0)),
            pl.BlockSpec((None, N_META, d), lambda bb, j: (bb, jnp.maximum(per * j - 1, 0), 0)),
            pl.BlockSpec((1, d), lambda bb, j: (0, 0)),
            pl.BlockSpec((1, d), lambda bb, j: (0, 0)),
            pl.BlockSpec((None, N_PACK, D_MODEL), lambda bb, j: (0, 0, 0)),
            pl.BlockSpec((1, N_PACK), lambda bb, j: (0, 0)),
        ],
        out_specs=(out_row(d), out_row(RET_COLS), out_row(FOX_COLS), out_row(LANES)),
        compiler_params=pltpu.CompilerParams(dimension_semantics=("parallel", "arbitrary"),
                                             vmem_limit_bytes=VMEM_LIMIT),
        name="embed_in_proj",
    )(meta, x, x, g.reshape(1, d), b.reshape(1, d), w, _inproj_col_scale())


def _inproj_kernel(h_ref, w_ref, cs_ref, ret_ref, fox_ref, flog_ref):
    _project(h_ref[...].astype(BF16), w_ref, cs_ref, ret_ref, fox_ref, flog_ref)


def _inproj_col_scale():
    cs = np.ones((1, N_PACK), np.float32)
    cs[0, KR_OFF:KR_OFF + 2 * LANES] = RET_DK ** -0.5
    cs[0, QF_OFF:QF_OFF + FOX_W] = FOX_DH ** -0.5 * LOG2E
    return jnp.asarray(cs)


def _inproj(h, w, layer, tm):
    rows = h.shape[0]
    return pl.pallas_call(
        _inproj_kernel,
        out_shape=(jax.ShapeDtypeStruct((rows, RET_COLS), BF16),
                   jax.ShapeDtypeStruct((rows, FOX_COLS), BF16),
                   jax.ShapeDtypeStruct((rows, LANES), F32)),
        grid=(rows // tm,),
        in_specs=[
            pl.BlockSpec((tm, D_MODEL), lambda i: (i, 0)),
            pl.BlockSpec((None, N_PACK, D_MODEL), lambda i: (layer, 0, 0)),
            pl.BlockSpec((1, N_PACK), lambda i: (0, 0)),
        ],
        out_specs=(pl.BlockSpec((tm, RET_COLS), lambda i: (i, 0)),
                   pl.BlockSpec((tm, FOX_COLS), lambda i: (i, 0)),
                   pl.BlockSpec((tm, LANES), lambda i: (i, 0))),
        compiler_params=pltpu.CompilerParams(dimension_semantics=("parallel",),
                                             vmem_limit_bytes=VMEM_LIMIT),
        name="in_proj",
    )(h, w, _inproj_col_scale())


def _seq_kernel(qk_ref, v_ref, g_ref, u_ref, fl_ref, cos_ref, sin_ref,
                xiq_ref, zk_ref, dm_ref, dec_ref, bm_ref, gng_ref, bf_ref, pw_ref, ps_ref,
                or_ref, op_ref, c_ref,
                state_sc, tail_sc, carry_sc):
    for s in range(qk_ref.shape[0]):
        _seq_one(qk_ref.at[s], v_ref.at[s], g_ref.at[s], u_ref.at[s], fl_ref.at[s], cos_ref, sin_ref,
                 xiq_ref, zk_ref, dm_ref, dec_ref, bm_ref, gng_ref, bf_ref, pw_ref, ps_ref,
                 or_ref.at[s], op_ref.at[s], c_ref.at[s],
                 state_sc.at[s], tail_sc.at[s], carry_sc.at[s])


def _seq_one(qk_ref, v_ref, g_ref, u_ref, fl_ref, cos_ref, sin_ref,
             xiq_ref, zk_ref, dm_ref, dec_ref, bm_ref, gng_ref, bf_ref, pw_ref, ps_ref,
             or_ref, op_ref, c_ref,
             state_sc, tail_sc, carry_sc):
    j = pl.program_id(1)
    ch = SEQ_BLOCK

    @pl.when(j == 0)
    def _():
        state_sc[...] = jnp.zeros_like(state_sc)
        tail_sc[...] = jnp.zeros_like(tail_sc)
        carry_sc[...] = jnp.zeros_like(carry_sc)

    qk = qk_ref[...].astype(F32)
    cs = cos_ref[...]
    sn = sin_ref[...]
    q1, q2 = qk[:, 0:LANES], qk[:, LANES:2 * LANES]
    k1, k2 = qk[:, 2 * LANES:3 * LANES], qk[:, 3 * LANES:4 * LANES]
    qr = jnp.concatenate([q1 * cs - q2 * sn, q1 * sn + q2 * cs], axis=-1)
    kr = jnp.concatenate([k1 * cs - k2 * sn, k1 * sn + k2 * cs], axis=-1)
    qb = qr.astype(BF16)
    qx = (qr * xiq_ref[...]).astype(BF16)
    kb = kr.astype(BF16)
    kz = (kr * zk_ref[...]).astype(BF16)
    v = v_ref[...]
    st = state_sc[...]
    cross = jnp.dot(qx, st.astype(BF16), preferred_element_type=F32)
    qlane = lax.broadcasted_iota(jnp.int32, (1, 2 * LANES), 1)
    qhead = jnp.where(qlane % LANES < RET_HEADS * RET_HALF, (qlane % LANES) // RET_HALF, RET_HEADS)
    inner = []
    for h in range(RET_HEADS):
        qh = jnp.where(qhead == h, qb, jnp.zeros_like(qb))
        s = lax.dot_general(qh, kb, (((1,), (1,)), ((), ())), preferred_element_type=F32)
        p = (s * dm_ref[h]).astype(BF16)
        inner.append(jnp.dot(p, v[:, h * LANES:(h + 1) * LANES], preferred_element_type=F32))
    o = jnp.concatenate(inner, axis=-1) + cross
    kv = lax.dot_general(kz, v, (((0,), (0,)), ((), ())), preferred_element_type=F32)
    state_sc[...] = st * dec_ref[...] + kv * bm_ref[...]

    vlane = lax.broadcasted_iota(jnp.int32, (1, LANES), 1) < RET_DV
    normed = []
    for h in range(RET_HEADS):
        xh = o[:, h * LANES:(h + 1) * LANES]
        mu = jnp.sum(xh, axis=-1, keepdims=True) * (1.0 / RET_DV)
        d = jnp.where(vlane, xh - mu, 0.0)
        var = jnp.sum(d * d, axis=-1, keepdims=True) * (1.0 / RET_DV)
        normed.append(d * lax.rsqrt(var + LN_EPS))
    y = jnp.concatenate(normed, axis=-1) * gng_ref[...]
    gate = g_ref[...].astype(F32)
    or_ref[...] = (gate * _sigmoid(gate) * y).astype(BF16)

    u = u_ref[...].astype(F32)
    tail_rows = tail_sc.shape[0]
    ext = jnp.concatenate([tail_sc[...], u], axis=0)
    tail_sc[...] = u[ch - tail_rows:, :]
    e2 = ext + pltpu.roll(ext, 1, 0)
    e4 = e2 + pltpu.roll(e2, 2, 0)
    e8 = e4 + pltpu.roll(e4, 4, 0)
    e16 = e8 + pltpu.roll(e8, 8, 0)
    glane = lax.broadcasted_iota(jnp.int32, (1, POOL_W), 1) // POOL_GROUP
    win = jnp.where(glane == 0, e2, jnp.where(glane == 1, e4, jnp.where(glane == 2, e8, e16)))
    win = win[tail_rows:, :]
    wlen = jnp.where(glane == 0, 2, jnp.where(glane == 1, 4, jnp.where(glane == 2, 8, 16)))
    pos = j * ch + lax.broadcasted_iota(jnp.int32, (ch, POOL_W), 0)
    cnt = jnp.minimum(pos + 1, wlen).astype(F32)
    pooled = (win / cnt - u).astype(BF16)
    yp = jnp.dot(pooled, pw_ref[...], preferred_element_type=F32) * ps_ref[...]
    op_ref[...] = yp.astype(BF16)

    z = (fl_ref[...] + bf_ref[...]).T[0:8, :]
    logf = jnp.minimum(z, 0.0) - jnp.log1p(jnp.exp(-jnp.abs(z)))
    pos_in = lax.broadcasted_iota(jnp.int32, (8, ch), 1)
    sh = 1
    while sh < ch:
        logf = logf + jnp.where(pos_in >= sh, pltpu.roll(logf, sh, 1), 0.0)
        sh *= 2
    c_t = logf + jnp.concatenate([carry_sc[...]] * (ch // LANES), axis=1)
    carry_sc[...] = jnp.broadcast_to(c_t[:, ch - 1:ch], carry_sc.shape)
    c = jnp.concatenate([c_t, jnp.zeros((LANES - 8, ch), F32)], axis=0).T
    c_ref[...] = c * LOG2E


def _seq_mix(proj, flog, consts, params, layer, bsz, lp):
    ch = SEQ_BLOCK
    nc = lp // ch
    rows = bsz * lp

    par = SEQ_PAR if bsz % SEQ_PAR == 0 else 1

    def rowblk(width, colblk):
        return pl.BlockSpec((par, ch, width), lambda g, j: (g, j, colblk))

    def const(shape):
        nd = len(shape)
        return pl.BlockSpec(shape, lambda bb, j: (0,) * nd)

    def layer_param(shape):
        return pl.BlockSpec((None,) + shape, lambda bb, j: (layer,) + (0,) * len(shape))

    in_specs = [
        rowblk(4 * LANES, QR_OFF // (4 * LANES)),
        rowblk(RETV_W, VR_OFF // RETV_W),
        rowblk(RETV_W, GR_OFF // RETV_W),
        rowblk(POOL_W, UP_OFF // POOL_W),
        rowblk(LANES, 0),
        pl.BlockSpec((ch, LANES), lambda bb, j: (j, 0)),
        pl.BlockSpec((ch, LANES), lambda bb, j: (j, 0)),
        const((ch, 2 * LANES)), const((ch, 2 * LANES)), const((RET_HEADS, ch, ch)),
        const((1, RETV_W)), const((2 * LANES, RETV_W)),
        layer_param((1, RETV_W)), layer_param((1, LANES)), layer_param((POOL_W, POOL_W)),
        layer_param((1, POOL_W)),
    ]
    out_shape = (jax.ShapeDtypeStruct((bsz, lp, RETV_W), BF16),
                 jax.ShapeDtypeStruct((bsz, lp, POOL_W), BF16),
                 jax.ShapeDtypeStruct((bsz, lp, LANES), F32))
    out_specs = (rowblk(RETV_W, 0), rowblk(POOL_W, 0), rowblk(LANES, 0))
    proj = proj.reshape(bsz, lp, proj.shape[-1])
    flog = flog.reshape(bsz, lp, LANES)
    o_r, o_p, cum = pl.pallas_call(
        _seq_kernel,
        out_shape=out_shape,
        grid=(bsz // par, nc),
        in_specs=in_specs,
        out_specs=out_specs,
        scratch_shapes=[pltpu.VMEM((par, 2 * LANES, RETV_W), F32),
                        pltpu.VMEM((par, 16, POOL_W), F32),
                        pltpu.VMEM((par, 8, LANES), F32)],
        compiler_params=pltpu.CompilerParams(dimension_semantics=("parallel", "arbitrary"),
                                             vmem_limit_bytes=VMEM_LIMIT),
        name="seq_mixers",
    )(proj, proj, proj, proj, flog, consts["cos"], consts["sin"],
      consts["xiq"], consts["zk"], consts["dm"], consts["dec"], consts["bm"],
      params["gn_g"], params["b_f"], params["pool_w"], params["pool_scale"])
    return o_r.reshape(rows, RETV_W), o_p.reshape(rows, POOL_W), cum.reshape(rows, LANES)


FOX_AUG = 3
FOX_VROWS = FOX_DH + 16
FOX_LATE_HEADS = 2
FOX_UNROLL = 8


def _fox_select_matrix():
    sel = np.zeros((LANES, FOX_HEADS * LANES), np.float32)
    for a in range(FOX_AUG):
        for h in range(FOX_HEADS):
            sel[8 * a + h, h * LANES + FOX_DH + a] = -1.0
    return jnp.asarray(sel, BF16)


def _fox_kernel(q_ref, k_ref, v_ref, c_ref, sel_ref, o_ref,
                ka_sc, vt_sc, qa_sc, st_sc, m_sc, acc_sc):
    i = pl.program_id(1)
    tq = SEQ_BLOCK
    tk = SEQ_BLOCK
    npair = FOX_HEADS // 2
    lane = lax.broadcasted_iota(jnp.int32, (1, LANES), 1)

    @pl.when(i == 0)
    def _():
        def chunk(t, carry):
            r0 = pl.multiple_of(t * tk, tk)
            cc = c_ref[pl.ds(r0, tk), :]
            pieces, rem = [], cc
            for a in range(FOX_AUG):
                piece = rem.astype(BF16).astype(F32)
                pieces.append(piece if a == 0 else pltpu.roll(piece, 8 * a, 1))
                rem = rem - piece
            packed = jnp.where(lane < 8, pieces[0], jnp.where(lane < 16, pieces[1], pieces[2]))
            c_aug = jnp.dot(packed.astype(BF16), sel_ref[...], preferred_element_type=F32)
            for pr in range(npair):
                kk = k_ref[pl.ds(r0, tk), pr * LANES:(pr + 1) * LANES].astype(F32)
                vv = v_ref[pl.ds(r0, tk), pr * LANES:(pr + 1) * LANES].astype(F32)
                vtt = vv.T
                ones = jnp.ones((FOX_VROWS - FOX_DH, tk), F32)
                k_heads = (kk, pltpu.roll(kk, FOX_DH, 1))
                v_heads = (vtt[:FOX_DH], vtt[FOX_DH:])
                for hh in range(2):
                    h = 2 * pr + hh
                    aug = jnp.where(lane < FOX_DH, k_heads[hh], c_aug[:, h * LANES:(h + 1) * LANES])
                    ka_sc[h, pl.ds(r0, tk), :] = aug.astype(BF16)
                    vt_sc[h, :, pl.ds(r0, tk)] = jnp.concatenate([v_heads[hh], ones], axis=0).astype(BF16)
            return carry

        lax.fori_loop(0, k_ref.shape[0] // tk, chunk, 0)

    for pr in range(npair):
        qq = q_ref[:, pr * LANES:(pr + 1) * LANES].astype(F32)
        tail = jnp.where(lane < FOX_DH + FOX_AUG, 1.0, 0.0)
        qa_sc[2 * pr] = jnp.where(lane < FOX_DH, qq, tail).astype(BF16)
        qa_sc[2 * pr + 1] = jnp.where(lane < FOX_DH, pltpu.roll(qq, FOX_DH, 1), tail).astype(BF16)
    m_sc[...] = jnp.full(m_sc.shape, NEG_INF, F32)
    acc_sc[...] = jnp.zeros_like(acc_sc)

    def score(t, h):
        ks = t * tk if isinstance(t, int) else pl.multiple_of(t * tk, tk)
        st_sc[h] = lax.dot_general(ka_sc[h, pl.ds(ks, tk), :], qa_sc[h], (((1,), (1,)), ((), ())),
                                   preferred_element_type=F32)

    def attend(t, masked, next_t):
        ks = t * tk if isinstance(t, int) else pl.multiple_of(t * tk, tk)

        def load(h, c0):
            s = st_sc[h, :, c0:c0 + LANES]
            if masked:
                k_id = lax.broadcasted_iota(jnp.int32, (tk, LANES), 0)
                q_id = c0 + lax.broadcasted_iota(jnp.int32, (tk, LANES), 1)
                s = jnp.where(k_id <= q_id, s, NEG_INF)
            return s

        for h in range(early, FOX_HEADS):
            score(t, h)
        for h in range(FOX_HEADS):
            p_halves, a_halves = [], []
            for c0 in range(0, tq, LANES):
                m_prev = m_sc[h, :, c0:c0 + LANES]
                m_new = jnp.maximum(m_prev, jnp.max(load(h, c0), axis=0, keepdims=True))
                m_sc[h, :, c0:c0 + LANES] = m_new
                a_halves.append(jnp.exp2(m_prev - m_new))
                p_halves.append(jnp.exp2(load(h, c0) - m_new).astype(BF16))
            p_t = jnp.concatenate(p_halves, axis=1)
            pv = jnp.dot(vt_sc[h, :, pl.ds(ks, tk)], p_t, preferred_element_type=F32)
            acc_sc[h] = jnp.concatenate(a_halves, axis=1) * acc_sc[h] + pv
            if next_t is not None and h < early:
                score(next_t, h)

    early = FOX_HEADS - FOX_LATE_HEADS
    for h in range(early):
        score(0, h)

    def run(t0, count):
        for d in range(count):
            attend(t0 + d, False, t0 + d + 1)

    def body(u, carry):
        run(FOX_UNROLL * u, FOX_UNROLL)
        return carry

    lax.fori_loop(0, i // FOX_UNROLL, body, 0)
    done = (i // FOX_UNROLL) * FOX_UNROLL
    span = FOX_UNROLL // 2
    while span >= 1:
        take = (i - done) >= span

        @pl.when(take)
        def _(done=done, span=span):
            run(done, span)

        done = done + jnp.where(take, span, 0)
        span //= 2

    attend(i, True, None)

    outs = []
    for pr in range(npair):
        acc_a = acc_sc[2 * pr]
        acc_b = acc_sc[2 * pr + 1]
        o_t = jnp.concatenate([acc_a[:FOX_DH] / acc_a[FOX_DH:FOX_DH + 1],
                               acc_b[:FOX_DH] / acc_b[FOX_DH:FOX_DH + 1]], axis=0)
        outs.append(o_t.T)
    o_ref[...] = jnp.concatenate(outs, axis=-1).astype(BF16)


def _fox(qkv, cum, bsz, lp):
    tq = SEQ_BLOCK
    nq = lp // tq
    rows = bsz * lp
    return pl.pallas_call(
        _fox_kernel,
        out_shape=jax.ShapeDtypeStruct((rows, FOX_W), BF16),
        grid=(bsz, nq),
        in_specs=[
            pl.BlockSpec((tq, FOX_W), lambda bb, i: (bb * nq + i, 0)),
            pl.BlockSpec((lp, FOX_W), lambda bb, i: (bb, 1)),
            pl.BlockSpec((lp, FOX_W), lambda bb, i: (bb, 2)),
            pl.BlockSpec((lp, LANES), lambda bb, i: (bb, 0)),
            pl.BlockSpec((LANES, FOX_HEADS * LANES), lambda bb, i: (0, 0)),
        ],
        out_specs=pl.BlockSpec((tq, FOX_W), lambda bb, i: (bb * nq + i, 0)),
        scratch_shapes=[pltpu.VMEM((FOX_HEADS, lp, LANES), BF16),
                        pltpu.VMEM((FOX_HEADS, FOX_VROWS, lp), BF16),
                        pltpu.VMEM((FOX_HEADS, tq, LANES), BF16),
                        pltpu.VMEM((FOX_HEADS, SEQ_BLOCK, tq), F32),
                        pltpu.VMEM((FOX_HEADS, 1, tq), F32),
                        pltpu.VMEM((FOX_HEADS, FOX_VROWS, tq), F32)],
        compiler_params=pltpu.CompilerParams(
            dimension_semantics=("parallel", "arbitrary"),
            vmem_limit_bytes=VMEM_LIMIT),
        name="fox_attention",
    )(qkv, qkv, qkv, cum, _fox_select_matrix())


def _mix_ffn_kernel(or_ref, op_ref, of_ref, h_ref, wo_ref, g1_ref, b1_ref,
                    w1_ref, w3_ref, w2_ref, g2_ref, b2_ref, o_ref):
    tm = h_ref.shape[0]
    sub = tm // MIX_FFN_SPLIT
    parts = [slice(s * sub, (s + 1) * sub) for s in range(MIX_FFN_SPLIT)]
    ys = []
    for rows in parts:
        mix = jnp.concatenate([or_ref[rows, :], op_ref[rows, :], of_ref[rows, :]], axis=-1)
        ys.append(ALPHA * h_ref[rows, :] + jnp.dot(mix, wo_ref[...], preferred_element_type=F32))
    h1s = [_layer_norm(y, g1_ref[...], b1_ref[...]) for y in ys]
    accs = []
    for h1 in h1s:
        xb = h1.astype(BF16)
        acc = None
        for c0 in range(0, D_FF, FF_CHUNK):
            a = jnp.dot(xb, w1_ref[:, c0:c0 + FF_CHUNK], preferred_element_type=F32)
            b = jnp.dot(xb, w3_ref[:, c0:c0 + FF_CHUNK], preferred_element_type=F32)
            t = (a * _sigmoid(a) * b).astype(BF16)
            part = jnp.dot(t, w2_ref[c0:c0 + FF_CHUNK, :], preferred_element_type=F32)
            acc = part if acc is None else acc + part
        accs.append(acc)
    for rows, h1, acc in zip(parts, h1s, accs):
        o_ref[rows, :] = _layer_norm(ALPHA * h1 + acc, g2_ref[...], b2_ref[...])


def _mix_ffn(o_r, o_p, o_f, h, params, layer, tm):
    rows = h.shape[0]

    def row(width):
        return pl.BlockSpec((tm, width), lambda i: (i, 0))

    def resident(shape):
        return pl.BlockSpec((None,) + shape, lambda i: (layer,) + (0,) * len(shape),
                            pipeline_mode=pl.Buffered(1))

    vec = resident((1, D_MODEL))
    return pl.pallas_call(
        _mix_ffn_kernel,
        out_shape=jax.ShapeDtypeStruct((rows, D_MODEL), F32),
        grid=(rows // tm,),
        in_specs=[row(RETV_W), row(POOL_W), row(FOX_W), row(D_MODEL),
                  resident((MIX_W, D_MODEL)), vec, vec,
                  resident((D_MODEL, D_FF)), resident((D_MODEL, D_FF)), resident((D_FF, D_MODEL)), vec, vec],
        out_specs=row(D_MODEL),
        compiler_params=pltpu.CompilerParams(dimension_semantics=("parallel",),
                                             vmem_limit_bytes=VMEM_LIMIT),
        name="mix_ffn_ln",
    )(o_r, o_p, o_f, h, params["w_out"], params["ln1_g"], params["ln1_b"],
      params["w1"], params["w3"], params["w2"], params["ln2_g"], params["ln2_b"])


def _pack_w_in_kernel(w_ref, o_ref):
    w = w_ref[...]
    cols = w.shape[1]
    pad = jnp.zeros((LANES - RET_HEADS * RET_HALF, cols), w.dtype)
    pieces = []
    for base in (0, RET_QK):
        for half in range(2):
            for h in range(RET_HEADS):
                lo = base + h * RET_DK + half * RET_HALF
                pieces.append(w[lo:lo + RET_HALF, :])
            pieces.append(pad)
    for base in (2 * RET_QK, 2 * RET_QK + RET_W):
        for h in range(RET_HEADS):
            pieces.append(w[base + h * RET_DV:base + (h + 1) * RET_DV, :])
            pieces.append(pad)
    aligned = 2 * RET_QK + 2 * RET_W
    pieces.append(w[aligned:aligned + POOL_W + 3 * FOX_W, :])
    logits = jnp.concatenate([w[aligned + POOL_W + 3 * FOX_W:, :], jnp.zeros((2, cols), w.dtype)], axis=0)
    pieces.append(logits)
    pieces.append(jnp.zeros((LANES - 8, cols), w.dtype))
    o_ref[...] = jnp.concatenate(pieces, axis=0).astype(BF16)


def _pack_w_in_call(w):
    depth, d, d_in = w.shape
    slab = SEQ_BLOCK
    per_layer = d // slab
    w_t = jnp.transpose(w, (2, 0, 1)).reshape(d_in, depth * d)
    return pl.pallas_call(
        _pack_w_in_kernel,
        out_shape=jax.ShapeDtypeStruct((depth, N_PACK, d), BF16),
        grid=(depth, per_layer),
        in_specs=[pl.BlockSpec((d_in, slab), lambda l, i: (0, l * per_layer + i))],
        out_specs=pl.BlockSpec((None, N_PACK, slab), lambda l, i: (l, 0, i)),
        compiler_params=pltpu.CompilerParams(dimension_semantics=("parallel", "parallel")),
        name="pack_w_in",
    )(w_t)


def _pack_w_out(w):
    depth, _, d = w.shape
    w_r = jnp.pad(w[:, :RET_W].reshape(depth, RET_HEADS, RET_DV, d),
                  ((0, 0), (0, 0), (0, RET_HEAD_PAD - RET_DV), (0, 0))).reshape(depth, RETV_W, d)
    return jnp.concatenate([w_r, w[:, RET_W:]], axis=1).astype(BF16)


def _pad_heads(vec):
    depth = vec.shape[0]
    padded = jnp.pad(vec.reshape(depth, RET_HEADS, RET_DV), ((0, 0), (0, 0), (0, RET_HEAD_PAD - RET_DV)))
    return padded.reshape(depth, 1, RETV_W)


def _block_diag(pw):
    g = len(POOL_WINDOWS)
    eye = jnp.eye(g, dtype=pw.dtype)
    return jnp.einsum("lgij,gh->lgihj", pw, eye).reshape(pw.shape[0], POOL_W, POOL_W).astype(BF16)


def _cast_kernel(x_ref, o_ref):
    o_ref[...] = x_ref[...].astype(o_ref.dtype)


def _to_bf16(w):
    depth, k, n = w.shape
    slab = max(s for s in range(16, k + 1, 16) if k % s == 0 and s * n <= CAST_SLAB_ELEMS)
    spec = pl.BlockSpec((None, slab, n), lambda l, i: (l, i, 0))
    return pl.pallas_call(
        _cast_kernel,
        out_shape=jax.ShapeDtypeStruct(w.shape, BF16),
        grid=(depth, k // slab),
        in_specs=[spec],
        out_specs=spec,
        compiler_params=pltpu.CompilerParams(dimension_semantics=("parallel", "parallel")),
        name="weights_to_bf16",
    )(w)


def _prepare_params(w_in, b_f, ret_gn_g, pool_w, pool_scale, w_out, ln1_g, ln1_b,
                    w_ffn1, w_ffn3, w_ffn2, ln2_g, ln2_b):
    depth = w_in.shape[0]
    vec = lambda a: a.reshape(depth, 1, a.shape[-1])
    return dict(
        w_in=_pack_w_in_call(w_in),
        gn_g=_pad_heads(ret_gn_g),
        b_f=vec(jnp.pad(b_f, ((0, 0), (0, LANES - FOX_HEADS)))),
        pool_w=_block_diag(pool_w),
        pool_scale=vec(pool_scale),
        w_out=_pack_w_out(w_out),
        ln1_g=vec(ln1_g), ln1_b=vec(ln1_b), ln2_g=vec(ln2_g), ln2_b=vec(ln2_b),
        w1=_to_bf16(w_ffn1), w3=_to_bf16(w_ffn3), w2=_to_bf16(w_ffn2))


def _retention_tables():
    ch = SEQ_BLOCK
    gamma = (1.0 - 2.0 ** (-5.0 - np.arange(RET_HEADS, dtype=np.float32))).astype(np.float32)
    lg = np.log(gamma).astype(np.float32)
    i = np.arange(ch, dtype=np.float32)
    diff = i[:, None] - i[None, :]
    dm = np.where(diff >= 0, np.exp(lg[:, None, None] * np.maximum(diff, 0.0)), 0.0).astype(np.float32)
    xi = np.exp(lg[:, None] * (i + 1.0)).astype(np.float32)
    zeta = np.exp(lg[:, None] * (ch - 1.0 - i)).astype(np.float32)
    lane = np.arange(2 * LANES)
    within = lane % LANES
    lane_head = np.where(within < RET_HEADS * RET_HALF, within // RET_HALF, -1)
    xiq = np.zeros((ch, 2 * LANES), np.float32)
    zk = np.zeros((ch, 2 * LANES), np.float32)
    bm = np.zeros((2 * LANES, RETV_W), np.float32)
    for h in range(RET_HEADS):
        sel = lane_head == h
        xiq[:, sel] = xi[h][:, None]
        zk[:, sel] = zeta[h][:, None]
        bm[sel, h * LANES:(h + 1) * LANES] = 1.0
    dec = np.repeat(np.exp(lg * ch).astype(np.float32), LANES)[None, :]
    return dict(xiq=jnp.asarray(xiq), zk=jnp.asarray(zk), dm=jnp.asarray(dm),
                dec=jnp.asarray(dec), bm=jnp.asarray(bm))


def _rotary_tables(lp):
    pos = jnp.arange(lp, dtype=F32)
    inv_freq = ROPE_BASE ** (-jnp.arange(RET_HALF, dtype=F32) / RET_HALF)
    ang = pos[:, None] * inv_freq[None, :]
    pad = LANES - RET_HEADS * RET_HALF
    cos = jnp.pad(jnp.tile(jnp.cos(ang), (1, RET_HEADS)), ((0, 0), (0, pad)))
    sin = jnp.pad(jnp.tile(jnp.sin(ang), (1, RET_HEADS)), ((0, 0), (0, pad)))
    return cos, sin


def kernel(x, meta, ln_emb_g, ln_emb_b, w_in, b_f, ret_gn_g, pool_w, pool_scale, w_out, ln1_g, ln1_b,
           w_ffn1, w_ffn3, w_ffn2, ln2_g, ln2_b):
    bsz, seq, d = x.shape
    assert d == D_MODEL and seq % SEQ_BLOCK == 0
    depth = w_in.shape[0]
    assert depth == DEPTH
    lp = seq + SEQ_BLOCK
    rows = bsz * lp
    tm = ROW_TILE if rows % ROW_TILE == 0 else SEQ_BLOCK

    consts = _retention_tables()
    consts["cos"], consts["sin"] = _rotary_tables(lp)
    params = _prepare_params(w_in, b_f, ret_gn_g, pool_w, pool_scale, w_out, ln1_g, ln1_b,
                             w_ffn1, w_ffn3, w_ffn2, ln2_g, ln2_b)

    h, ret, qkv, flog = _embed_inproj(x, meta, ln_emb_g, ln_emb_b, params["w_in"], lp)
    for l in range(depth):
        if l > 0:
            ret, qkv, flog = _inproj(h, params["w_in"], l, tm)
        o_r, o_p, cum = _seq_mix(ret, flog, consts, params, l, bsz, lp)
        o_f = _fox(qkv, cum, bsz, lp)
        h = _mix_ffn(o_r, o_p, o_f, h, params, l, tm)
    return h.reshape(bsz, lp, d)[:, N_META:N_META + seq]
```

```python
import functools

import numpy as np
import jax
import jax.numpy as jnp
from jax import lax
from jax.experimental import pallas as pl
from jax.experimental.pallas import tpu as pltpu

F32 = jnp.float32
BF16 = jnp.bfloat16

D_MODEL = 1024
N_META = 16
RET_HEADS = 4
RET_DK = 48
RET_HALF = RET_DK // 2
RET_DV = 96
RET_QK = RET_HEADS * RET_DK
RET_W = RET_HEADS * RET_DV
POOL_WINDOWS = (2, 4, 8, 16)
POOL_GROUP = 64
POOL_W = len(POOL_WINDOWS) * POOL_GROUP
FOX_HEADS = 6
FOX_DH = 64
FOX_W = FOX_HEADS * FOX_DH
D_FF = 2816
ROPE_BASE = 10000.0
LN_EPS = 1e-5
NEG_INF = -1e30
DEPTH = 2
ALPHA = (2.0 * DEPTH) ** 0.25
LOG2E = 1.4426950408889634

LANES = 128
MXU_DIM = 256
SEQ_BLOCK = 256
ROW_TILE = 1024
VMEM_LIMIT = 56 * 1024 * 1024

RET_HEAD_PAD = LANES
QR_OFF = 0
KR_OFF = 2 * LANES
VR_OFF = 4 * LANES
GR_OFF = VR_OFF + RET_HEADS * RET_HEAD_PAD
UP_OFF = GR_OFF + RET_HEADS * RET_HEAD_PAD
QF_OFF = UP_OFF + POOL_W
KF_OFF = QF_OFF + FOX_W
VF_OFF = KF_OFF + FOX_W
FL_OFF = VF_OFF + FOX_W
RET_COLS = QF_OFF
FOX_COLS = 3 * FOX_W
N_PACK = FL_OFF + LANES
RETV_W = RET_HEADS * RET_HEAD_PAD
MIX_W = RETV_W + POOL_W + FOX_W
PROJ_CHUNK = 512
FF_CHUNK = 256
CAST_SLAB_ELEMS = 768 * 1024
SEQ_PAR = 4
MIX_FFN_SPLIT = 2


def _layer_norm(x, g, b):
    mu = jnp.mean(x, axis=-1, keepdims=True)
    d = x - mu
    var = jnp.mean(d * d, axis=-1, keepdims=True)
    return d * lax.rsqrt(var + LN_EPS) * g + b


def _sigmoid(x):
    return 1.0 / (1.0 + jnp.exp(-x))


def _project(xb, w_ref, cs_ref, ret_ref, fox_ref, flog_ref):
    def cols(c0, width):
        r = lax.dot_general(xb, w_ref[c0:c0 + width, :], (((1,), (1,)), ((), ())),
                            preferred_element_type=F32)
        return r * cs_ref[:, c0:c0 + width]

    for c0 in range(0, RET_COLS, PROJ_CHUNK):
        width = min(PROJ_CHUNK, RET_COLS - c0)
        ret_ref[:, c0:c0 + width] = cols(c0, width).astype(BF16)
    for c0 in range(0, FOX_COLS - LANES, PROJ_CHUNK):
        fox_ref[:, c0:c0 + PROJ_CHUNK] = cols(QF_OFF + c0, PROJ_CHUNK).astype(BF16)
    tail = cols(FL_OFF - LANES, 2 * LANES)
    fox_ref[:, FOX_COLS - LANES:] = tail[:, :LANES].astype(BF16)
    flog_ref[...] = tail[:, LANES:]


def _embed_inproj_kernel(meta_ref, xm_ref, xe_ref, g_ref, b_ref, w_ref, cs_ref,
                         h_ref, ret_ref, fox_ref, flog_ref):
    j = pl.program_id(1)
    last = pl.num_programs(1) - 1
    top = jnp.where(j == 0, meta_ref[...], xe_ref[...])
    body = jnp.where(j == last, 0.0, xm_ref[0:SEQ_BLOCK - N_META, :])
    rows = jnp.concatenate([top, body], axis=0)
    h = _layer_norm(rows, g_ref[...], b_ref[...])
    h_ref[...] = h
    _project(h.astype(BF16), w_ref, cs_ref, ret_ref, fox_ref, flog_ref)


def _embed_inproj(x, meta, g, b, w, lp):
    bsz, seq, d = x.shape
    nblk = lp // SEQ_BLOCK
    n_xblk = seq // SEQ_BLOCK
    per = SEQ_BLOCK // N_META
    rows = bsz * lp
    out_row = lambda width: pl.BlockSpec((SEQ_BLOCK, width), lambda bb, j: (bb * nblk + j, 0))
    return pl.pallas_call(
        _embed_inproj_kernel,
        out_shape=(jax.ShapeDtypeStruct((rows, d), F32),
                   jax.ShapeDtypeStruct((rows, RET_COLS), BF16),
                   jax.ShapeDtypeStruct((rows, FOX_COLS), BF16),
                   jax.ShapeDtypeStruct((rows, LANES), F32)),
        grid=(bsz, nblk),
        in_specs=[
            pl.BlockSpec((N_META, d), lambda bb, j: (0, 0)),
            pl.BlockSpec((None, SEQ_BLOCK, d), lambda bb, j: (bb, jnp.minimum(j, n_xblk - 1), 0)),
            pl.BlockSpec((None, N_META, d), lambda bb, j: (bb, jnp.maximum(per * j - 1, 0), 0)),
            pl.BlockSpec((1, d), lambda bb, j: (0, 0)),
            pl.BlockSpec((1, d), lambda bb, j: (0, 0)),
            pl.BlockSpec((None, N_PACK, D_MODEL), lambda bb, j: (0, 0, 0)),
            pl.BlockSpec((1, N_PACK), lambda bb, j: (0, 0)),
        ],
        out_specs=(out_row(d), out_row(RET_COLS), out_row(FOX_COLS), out_row(LANES)),
        compiler_params=pltpu.CompilerParams(dimension_semantics=("parallel", "arbitrary"),
                                             vmem_limit_bytes=VMEM_LIMIT),
        name="embed_in_proj",
    )(meta, x, x, g.reshape(1, d), b.reshape(1, d), w, _inproj_col_scale())


def _inproj_kernel(h_ref, w_ref, cs_ref, ret_ref, fox_ref, flog_ref):
    _project(h_ref[...].astype(BF16), w_ref, cs_ref, ret_ref, fox_ref, flog_ref)


def _inproj_col_scale():
    cs = np.ones((1, N_PACK), np.float32)
    cs[0, KR_OFF:KR_OFF + 2 * LANES] = RET_DK ** -0.5
    cs[0, QF_OFF:QF_OFF + FOX_W] = FOX_DH ** -0.5 * LOG2E
    return jnp.asarray(cs)


def _inproj(h, w, layer, tm):
    rows = h.shape[0]
    return pl.pallas_call(
        _inproj_kernel,
        out_shape=(jax.ShapeDtypeStruct((rows, RET_COLS), BF16),
                   jax.ShapeDtypeStruct((rows, FOX_COLS), BF16),
                   jax.ShapeDtypeStruct((rows, LANES), F32)),
        grid=(rows // tm,),
        in_specs=[
            pl.BlockSpec((tm, D_MODEL), lambda i: (i, 0)),
            pl.BlockSpec((None, N_PACK, D_MODEL), lambda i: (layer, 0, 0)),
            pl.BlockSpec((1, N_PACK), lambda i: (0, 0)),
        ],
        out_specs=(pl.BlockSpec((tm, RET_COLS), lambda i: (i, 0)),
                   pl.BlockSpec((tm, FOX_COLS), lambda i: (i, 0)),
                   pl.BlockSpec((tm, LANES), lambda i: (i, 0))),
        compiler_params=pltpu.CompilerParams(dimension_semantics=("parallel",),
                                             vmem_limit_bytes=VMEM_LIMIT),
        name="in_proj",
    )(h, w, _inproj_col_scale())


def _seq_kernel(qk_ref, v_ref, g_ref, u_ref, fl_ref, cos_ref, sin_ref,
                xiq_ref, zk_ref, dm_ref, dec_ref, bm_ref, gng_ref, bf_ref, pw_ref, ps_ref,
                or_ref, op_ref, c_ref,
                state_sc, tail_sc, carry_sc):
    for s in range(qk_ref.shape[0]):
        _seq_one(qk_ref.at[s], v_ref.at[s], g_ref.at[s], u_ref.at[s], fl_ref.at[s], cos_ref, sin_ref,
                 xiq_ref, zk_ref, dm_ref, dec_ref, bm_ref, gng_ref, bf_ref, pw_ref, ps_ref,
                 or_ref.at[s], op_ref.at[s], c_ref.at[s],
                 state_sc.at[s], tail_sc.at[s], carry_sc.at[s])


def _seq_one(qk_ref, v_ref, g_ref, u_ref, fl_ref, cos_ref, sin_ref,
             xiq_ref, zk_ref, dm_ref, dec_ref, bm_ref, gng_ref, bf_ref, pw_ref, ps_ref,
             or_ref, op_ref, c_ref,
             state_sc, tail_sc, carry_sc):
    j = pl.program_id(1)
    ch = SEQ_BLOCK

    @pl.when(j == 0)
    def _():
        state_sc[...] = jnp.zeros_like(state_sc)
        tail_sc[...] = jnp.zeros_like(tail_sc)
        carry_sc[...] = jnp.zeros_like(carry_sc)

    qk = qk_ref[...].astype(F32)
    cs = cos_ref[...]
    sn = sin_ref[...]
    q1, q2 = qk[:, 0:LANES], qk[:, LANES:2 * LANES]
    k1, k2 = qk[:, 2 * LANES:3 * LANES], qk[:, 3 * LANES:4 * LANES]
    qr = jnp.concatenate([q1 * cs - q2 * sn, q1 * sn + q2 * cs], axis=-1)
    kr = jnp.concatenate([k1 * cs - k2 * sn, k1 * sn + k2 * cs], axis=-1)
    qb = qr.astype(BF16)
    qx = (qr * xiq_ref[...]).astype(BF16)
    kb = kr.astype(BF16)
    kz = (kr * zk_ref[...]).astype(BF16)
    v = v_ref[...]
    st = state_sc[...]
    cross = jnp.dot(qx, st.astype(BF16), preferred_element_type=F32)
    qlane = lax.broadcasted_iota(jnp.int32, (1, 2 * LANES), 1)
    qhead = jnp.where(qlane % LANES < RET_HEADS * RET_HALF, (qlane % LANES) // RET_HALF, RET_HEADS)
    inner = []
    for h in range(RET_HEADS):
        qh = jnp.where(qhead == h, qb, jnp.zeros_like(qb))
        s = lax.dot_general(qh, kb, (((1,), (1,)), ((), ())), preferred_element_type=F32)
        p = (s * dm_ref[h]).astype(BF16)
        inner.append(jnp.dot(p, v[:, h * LANES:(h + 1) * LANES], preferred_element_type=F32))
    o = jnp.concatenate(inner, axis=-1) + cross
    kv = lax.dot_general(kz, v, (((0,), (0,)), ((), ())), preferred_element_type=F32)
    state_sc[...] = st * dec_ref[...] + kv * bm_ref[...]

    vlane = lax.broadcasted_iota(jnp.int32, (1, LANES), 1) < RET_DV
    normed = []
    for h in range(RET_HEADS):
        xh = o[:, h * LANES:(h + 1) * LANES]
        mu = jnp.sum(xh, axis=-1, keepdims=True) * (1.0 / RET_DV)
        d = jnp.where(vlane, xh - mu, 0.0)
        var = jnp.sum(d * d, axis=-1, keepdims=True) * (1.0 / RET_DV)
        normed.append(d * lax.rsqrt(var + LN_EPS))
    y = jnp.concatenate(normed, axis=-1) * gng_ref[...]
    gate = g_ref[...].astype(F32)
    or_ref[...] = (gate * _sigmoid(gate) * y).astype(BF16)

    u = u_ref[...].astype(F32)
    tail_rows = tail_sc.shape[0]
    ext = jnp.concatenate([tail_sc[...], u], axis=0)
    tail_sc[...] = u[ch - tail_rows:, :]
    e2 = ext + pltpu.roll(ext, 1, 0)
    e4 = e2 + pltpu.roll(e2, 2, 0)
    e8 = e4 + pltpu.roll(e4, 4, 0)
    e16 = e8 + pltpu.roll(e8, 8, 0)
    glane = lax.broadcasted_iota(jnp.int32, (1, POOL_W), 1) // POOL_GROUP
    win = jnp.where(glane == 0, e2, jnp.where(glane == 1, e4, jnp.where(glane == 2, e8, e16)))
    win = win[tail_rows:, :]
    wlen = jnp.where(glane == 0, 2, jnp.where(glane == 1, 4, jnp.where(glane == 2, 8, 16)))
    pos = j * ch + lax.broadcasted_iota(jnp.int32, (ch, POOL_W), 0)
    cnt = jnp.minimum(pos + 1, wlen).astype(F32)
    pooled = (win / cnt - u).astype(BF16)
    yp = jnp.dot(pooled, pw_ref[...], preferred_element_type=F32) * ps_ref[...]
    op_ref[...] = yp.astype(BF16)

    z = (fl_ref[...] + bf_ref[...]).T[0:8, :]
    logf = jnp.minimum(z, 0.0) - jnp.log1p(jnp.exp(-jnp.abs(z)))
    pos_in = lax.broadcasted_iota(jnp.int32, (8, ch), 1)
    sh = 1
    while sh < ch:
        logf = logf + jnp.where(pos_in >= sh, pltpu.roll(logf, sh, 1), 0.0)
        sh *= 2
    c_t = logf + jnp.concatenate([carry_sc[...]] * (ch // LANES), axis=1)
    carry_sc[...] = jnp.broadcast_to(c_t[:, ch - 1:ch], carry_sc.shape)
    c = jnp.concatenate([c_t, jnp.zeros((LANES - 8, ch), F32)], axis=0).T
    c_ref[...] = c * LOG2E


def _seq_mix(proj, flog, consts, params, layer, bsz, lp):
    ch = SEQ_BLOCK
    nc = lp // ch
    rows = bsz * lp

    par = SEQ_PAR if bsz % SEQ_PAR == 0 else 1

    def rowblk(width, colblk):
        return pl.BlockSpec((par, ch, width), lambda g, j: (g, j, colblk))

    def const(shape):
        nd = len(shape)
        return pl.BlockSpec(shape, lambda bb, j: (0,) * nd)

    def layer_param(shape):
        return pl.BlockSpec((None,) + shape, lambda bb, j: (layer,) + (0,) * len(shape))

    in_specs = [
        rowblk(4 * LANES, QR_OFF // (4 * LANES)),
        rowblk(RETV_W, VR_OFF // RETV_W),
        rowblk(RETV_W, GR_OFF // RETV_W),
        rowblk(POOL_W, UP_OFF // POOL_W),
        rowblk(LANES, 0),
        pl.BlockSpec((ch, LANES), lambda bb, j: (j, 0)),
        pl.BlockSpec((ch, LANES), lambda bb, j: (j, 0)),
        const((ch, 2 * LANES)), const((ch, 2 * LANES)), const((RET_HEADS, ch, ch)),
        const((1, RETV_W)), const((2 * LANES, RETV_W)),
        layer_param((1, RETV_W)), layer_param((1, LANES)), layer_param((POOL_W, POOL_W)),
        layer_param((1, POOL_W)),
    ]
    out_shape = (jax.ShapeDtypeStruct((bsz, lp, RETV_W), BF16),
                 jax.ShapeDtypeStruct((bsz, lp, POOL_W), BF16),
                 jax.ShapeDtypeStruct((bsz, lp, LANES), F32))
    out_specs = (rowblk(RETV_W, 0), rowblk(POOL_W, 0), rowblk(LANES, 0))
    proj = proj.reshape(bsz, lp, proj.shape[-1])
    flog = flog.reshape(bsz, lp, LANES)
    o_r, o_p, cum = pl.pallas_call(
        _seq_kernel,
        out_shape=out_shape,
        grid=(bsz // par, nc),
        in_specs=in_specs,
        out_specs=out_specs,
        scratch_shapes=[pltpu.VMEM((par, 2 * LANES, RETV_W), F32),
                        pltpu.VMEM((par, 16, POOL_W), F32),
                        pltpu.VMEM((par, 8, LANES), F32)],
        compiler_params=pltpu.CompilerParams(dimension_semantics=("parallel", "arbitrary"),
                                             vmem_limit_bytes=VMEM_LIMIT),
        name="seq_mixers",
    )(proj, proj, proj, proj, flog, consts["cos"], consts["sin"],
      consts["xiq"], consts["zk"], consts["dm"], consts["dec"], consts["bm"],
      params["gn_g"], params["b_f"], params["pool_w"], params["pool_scale"])
    return o_r.reshape(rows, RETV_W), o_p.reshape(rows, POOL_W), cum.reshape(rows, LANES)


FOX_AUG = 3
FOX_VROWS = FOX_DH + 16
FOX_LATE_HEADS = 2
FOX_UNROLL = 8


def _fox_select_matrix():
    sel = np.zeros((LANES, FOX_HEADS * LANES), np.float32)
    for a in range(FOX_AUG):
        for h in range(FOX_HEADS):
            sel[8 * a + h, h * LANES + FOX_DH + a] = -1.0
    return jnp.asarray(sel, BF16)


def _fox_kernel(q_ref, k_ref, v_ref, c_ref, sel_ref, o_ref,
                ka_sc, vt_sc, qa_sc, st_sc, m_sc, acc_sc):
    i = pl.program_id(1)
    tq = SEQ_BLOCK
    tk = SEQ_BLOCK
    npair = FOX_HEADS // 2
    lane = lax.broadcasted_iota(jnp.int32, (1, LANES), 1)

    @pl.when(i == 0)
    def _():
        def chunk(t, carry):
            r0 = pl.multiple_of(t * tk, tk)
            cc = c_ref[pl.ds(r0, tk), :]
            pieces, rem = [], cc
            for a in range(FOX_AUG):
                piece = rem.astype(BF16).astype(F32)
                pieces.append(piece if a == 0 else pltpu.roll(piece, 8 * a, 1))
                rem = rem - piece
            packed = jnp.where(lane < 8, pieces[0], jnp.where(lane < 16, pieces[1], pieces[2]))
            c_aug = jnp.dot(packed.astype(BF16), sel_ref[...], preferred_element_type=F32)
            for pr in range(npair):
                kk = k_ref[pl.ds(r0, tk), pr * LANES:(pr + 1) * LANES].astype(F32)
                vv = v_ref[pl.ds(r0, tk), pr * LANES:(pr + 1) * LANES].astype(F32)
                vtt = vv.T
                ones = jnp.ones((FOX_VROWS - FOX_DH, tk), F32)
                k_heads = (kk, pltpu.roll(kk, FOX_DH, 1))
                v_heads = (vtt[:FOX_DH], vtt[FOX_DH:])
                for hh in range(2):
                    h = 2 * pr + hh
                    aug = jnp.where(lane < FOX_DH, k_heads[hh], c_aug[:, h * LANES:(h + 1) * LANES])
                    ka_sc[h, pl.ds(r0, tk), :] = aug.astype(BF16)
                    vt_sc[h, :, pl.ds(r0, tk)] = jnp.concatenate([v_heads[hh], ones], axis=0).astype(BF16)
            return carry

        lax.fori_loop(0, k_ref.shape[0] // tk, chunk, 0)

    for pr in range(npair):
        qq = q_ref[:, pr * LANES:(pr + 1) * LANES].astype(F32)
        tail = jnp.where(lane < FOX_DH + FOX_AUG, 1.0, 0.0)
        qa_sc[2 * pr] = jnp.where(lane < FOX_DH, qq, tail).astype(BF16)
        qa_sc[2 * pr + 1] = jnp.where(lane < FOX_DH, pltpu.roll(qq, FOX_DH, 1), tail).astype(BF16)
    m_sc[...] = jnp.full(m_sc.shape, NEG_INF, F32)
    acc_sc[...] = jnp.zeros_like(acc_sc)

    def score(t, h):
        ks = t * tk if isinstance(t, int) else pl.multiple_of(t * tk, tk)
        st_sc[h] = lax.dot_general(ka_sc[h, pl.ds(ks, tk), :], qa_sc[h], (((1,), (1,)), ((), ())),
                                   preferred_element_type=F32)

    def attend(t, masked, next_t):
        ks = t * tk if isinstance(t, int) else pl.multiple_of(t * tk, tk)

        def load(h, c0):
            s = st_sc[h, :, c0:c0 + LANES]
            if masked:
                k_id = lax.broadcasted_iota(jnp.int32, (tk, LANES), 0)
                q_id = c0 + lax.broadcasted_iota(jnp.int32, (tk, LANES), 1)
                s = jnp.where(k_id <= q_id, s, NEG_INF)
            return s

        for h in range(early, FOX_HEADS):
            score(t, h)
        for h in range(FOX_HEADS):
            p_halves, a_halves = [], []
            for c0 in range(0, tq, LANES):
                m_prev = m_sc[h, :, c0:c0 + LANES]
                m_new = jnp.maximum(m_prev, jnp.max(load(h, c0), axis=0, keepdims=True))
                m_sc[h, :, c0:c0 + LANES] = m_new
                a_halves.append(jnp.exp2(m_prev - m_new))
                p_halves.append(jnp.exp2(load(h, c0) - m_new).astype(BF16))
            p_t = jnp.concatenate(p_halves, axis=1)
            pv = jnp.dot(vt_sc[h, :, pl.ds(ks, tk)], p_t, preferred_element_type=F32)
            acc_sc[h] = jnp.concatenate(a_halves, axis=1) * acc_sc[h] + pv
            if next_t is not None and h < early:
                score(next_t, h)

    early = FOX_HEADS - FOX_LATE_HEADS
    for h in range(early):
        score(0, h)

    def run(t0, count):
        for d in range(count):
            attend(t0 + d, False, t0 + d + 1)

    def body(u, carry):
        run(FOX_UNROLL * u, FOX_UNROLL)
        return carry

    lax.fori_loop(0, i // FOX_UNROLL, body, 0)
    done = (i // FOX_UNROLL) * FOX_UNROLL
    span = FOX_UNROLL // 2
    while span >= 1:
        take = (i - done) >= span

        @pl.when(take)
        def _(done=done, span=span):
            run(done, span)

        done = done + jnp.where(take, span, 0)
        span //= 2

    attend(i, True, None)

    outs = []
    for pr in range(npair):
        acc_a = acc_sc[2 * pr]
        acc_b = acc_sc[2 * pr + 1]
        o_t = jnp.concatenate([acc_a[:FOX_DH] / acc_a[FOX_DH:FOX_DH + 1],
                               acc_b[:FOX_DH] / acc_b[FOX_DH:FOX_DH + 1]], axis=0)
        outs.append(o_t.T)
    o_ref[...] = jnp.concatenate(outs, axis=-1).astype(BF16)


def _fox(qkv, cum, bsz, lp):
    tq = SEQ_BLOCK
    nq = lp // tq
    rows = bsz * lp
    return pl.pallas_call(
        _fox_kernel,
        out_shape=jax.ShapeDtypeStruct((rows, FOX_W), BF16),
        grid=(bsz, nq),
        in_specs=[
            pl.BlockSpec((tq, FOX_W), lambda bb, i: (bb * nq + i, 0)),
            pl.BlockSpec((lp, FOX_W), lambda bb, i: (bb, 1)),
            pl.BlockSpec((lp, FOX_W), lambda bb, i: (bb, 2)),
            pl.BlockSpec((lp, LANES), lambda bb, i: (bb, 0)),
            pl.BlockSpec((LANES, FOX_HEADS * LANES), lambda bb, i: (0, 0)),
        ],
        out_specs=pl.BlockSpec((tq, FOX_W), lambda bb, i: (bb * nq + i, 0)),
        scratch_shapes=[pltpu.VMEM((FOX_HEADS, lp, LANES), BF16),
                        pltpu.VMEM((FOX_HEADS, FOX_VROWS, lp), BF16),
                        pltpu.VMEM((FOX_HEADS, tq, LANES), BF16),
                        pltpu.VMEM((FOX_HEADS, SEQ_BLOCK, tq), F32),
                        pltpu.VMEM((FOX_HEADS, 1, tq), F32),
                        pltpu.VMEM((FOX_HEADS, FOX_VROWS, tq), F32)],
        compiler_params=pltpu.CompilerParams(
            dimension_semantics=("parallel", "arbitrary"),
            vmem_limit_bytes=VMEM_LIMIT),
        name="fox_attention",
    )(qkv, qkv, qkv, cum, _fox_select_matrix())


def _mix_ffn_kernel(or_ref, op_ref, of_ref, h_ref, wo_ref, g1_ref, b1_ref,
                    w1_ref, w3_ref, w2_ref, g2_ref, b2_ref, o_ref):
    tm = h_ref.shape[0]
    sub = tm // MIX_FFN_SPLIT
    parts = [slice(s * sub, (s + 1) * sub) for s in range(MIX_FFN_SPLIT)]
    ys = []
    for rows in parts:
        mix = jnp.concatenate([or_ref[rows, :], op_ref[rows, :], of_ref[rows, :]], axis=-1)
        ys.append(ALPHA * h_ref[rows, :] + jnp.dot(mix, wo_ref[...], preferred_element_type=F32))
    h1s = [_layer_norm(y, g1_ref[...], b1_ref[...]) for y in ys]
    accs = []
    for h1 in h1s:
        xb = h1.astype(BF16)
        acc = None
        for c0 in range(0, D_FF, FF_CHUNK):
            a = jnp.dot(xb, w1_ref[:, c0:c0 + FF_CHUNK], preferred_element_type=F32)
            b = jnp.dot(xb, w3_ref[:, c0:c0 + FF_CHUNK], preferred_element_type=F32)
            t = (a * _sigmoid(a) * b).astype(BF16)
            part = jnp.dot(t, w2_ref[c0:c0 + FF_CHUNK, :], preferred_element_type=F32)
            acc = part if acc is None else acc + part
        accs.append(acc)
    for rows, h1, acc in zip(parts, h1s, accs):
        o_ref[rows, :] = _layer_norm(ALPHA * h1 + acc, g2_ref[...], b2_ref[...])


def _mix_ffn(o_r, o_p, o_f, h, params, layer, tm):
    rows = h.shape[0]

    def row(width):
        return pl.BlockSpec((tm, width), lambda i: (i, 0))

    def resident(shape):
        return pl.BlockSpec((None,) + shape, lambda i: (layer,) + (0,) * len(shape),
                            pipeline_mode=pl.Buffered(1))

    vec = resident((1, D_MODEL))
    return pl.pallas_call(
        _mix_ffn_kernel,
        out_shape=jax.ShapeDtypeStruct((rows, D_MODEL), F32),
        grid=(rows // tm,),
        in_specs=[row(RETV_W), row(POOL_W), row(FOX_W), row(D_MODEL),
                  resident((MIX_W, D_MODEL)), vec, vec,
                  resident((D_MODEL, D_FF)), resident((D_MODEL, D_FF)), resident((D_FF, D_MODEL)), vec, vec],
        out_specs=row(D_MODEL),
        compiler_params=pltpu.CompilerParams(dimension_semantics=("parallel",),
                                             vmem_limit_bytes=VMEM_LIMIT),
        name="mix_ffn_ln",
    )(o_r, o_p, o_f, h, params["w_out"], params["ln1_g"], params["ln1_b"],
      params["w1"], params["w3"], params["w2"], params["ln2_g"], params["ln2_b"])


def _pack_w_in_kernel(w_ref, o_ref):
    w = w_ref[...]
    cols = w.shape[1]
    pad = jnp.zeros((LANES - RET_HEADS * RET_HALF, cols), w.dtype)
    pieces = []
    for base in (0, RET_QK):
        for half in range(2):
            for h in range(RET_HEADS):
                lo = base + h * RET_DK + half * RET_HALF
                pieces.append(w[lo:lo + RET_HALF, :])
            pieces.append(pad)
    for base in (2 * RET_QK, 2 * RET_QK + RET_W):
        for h in range(RET_HEADS):
            pieces.append(w[base + h * RET_DV:base + (h + 1) * RET_DV, :])
            pieces.append(pad)
    aligned = 2 * RET_QK + 2 * RET_W
    pieces.append(w[aligned:aligned + POOL_W + 3 * FOX_W, :])
    logits = jnp.concatenate([w[aligned + POOL_W + 3 * FOX_W:, :], jnp.zeros((2, cols), w.dtype)], axis=0)
    pieces.append(logits)
    pieces.append(jnp.zeros((LANES - 8, cols), w.dtype))
    o_ref[...] = jnp.concatenate(pieces, axis=0).astype(BF16)


def _pack_w_in_call(w):
    depth, d, d_in = w.shape
    slab = SEQ_BLOCK
    per_layer = d // slab
    w_t = jnp.transpose(w, (2, 0, 1)).reshape(d_in, depth * d)
    return pl.pallas_call(
        _pack_w_in_kernel,
        out_shape=jax.ShapeDtypeStruct((depth, N_PACK, d), BF16),
        grid=(depth, per_layer),
        in_specs=[pl.BlockSpec((d_in, slab), lambda l, i: (0, l * per_layer + i))],
        out_specs=pl.BlockSpec((None, N_PACK, slab), lambda l, i: (l, 0, i)),
        compiler_params=pltpu.CompilerParams(dimension_semantics=("parallel", "parallel")),
        name="pack_w_in",
    )(w_t)


def _pack_w_out(w):
    depth, _, d = w.shape
    w_r = jnp.pad(w[:, :RET_W].reshape(depth, RET_HEADS, RET_DV, d),
                  ((0, 0), (0, 0), (0, RET_HEAD_PAD - RET_DV), (0, 0))).reshape(depth, RETV_W, d)
    return jnp.concatenate([w_r, w[:, RET_W:]], axis=1).astype(BF16)


def _pad_heads(vec):
    depth = vec.shape[0]
    padded = jnp.pad(vec.reshape(depth, RET_HEADS, RET_DV), ((0, 0), (0, 0), (0, RET_HEAD_PAD - RET_DV)))
    return padded.reshape(depth, 1, RETV_W)


def _block_diag(pw):
    g = len(POOL_WINDOWS)
    eye = jnp.eye(g, dtype=pw.dtype)
    return jnp.einsum("lgij,gh->lgihj", pw, eye).reshape(pw.shape[0], POOL_W, POOL_W).astype(BF16)


def _cast_kernel(x_ref, o_ref):
    o_ref[...] = x_ref[...].astype(o_ref.dtype)


def _to_bf16(w):
    depth, k, n = w.shape
    slab = max(s for s in range(16, k + 1, 16) if k % s == 0 and s * n <= CAST_SLAB_ELEMS)
    spec = pl.BlockSpec((None, slab, n), lambda l, i: (l, i, 0))
    return pl.pallas_call(
        _cast_kernel,
        out_shape=jax.ShapeDtypeStruct(w.shape, BF16),
        grid=(depth, k // slab),
        in_specs=[spec],
        out_specs=spec,
        compiler_params=pltpu.CompilerParams(dimension_semantics=("parallel", "parallel")),
        name="weights_to_bf16",
    )(w)


def _prepare_params(w_in, b_f, ret_gn_g, pool_w, pool_scale, w_out, ln1_g, ln1_b,
                    w_ffn1, w_ffn3, w_ffn2, ln2_g, ln2_b):
    depth = w_in.shape[0]
    vec = lambda a: a.reshape(depth, 1, a.shape[-1])
    return dict(
        w_in=_pack_w_in_call(w_in),
        gn_g=_pad_heads(ret_gn_g),
        b_f=vec(jnp.pad(b_f, ((0, 0), (0, LANES - FOX_HEADS)))),
        pool_w=_block_diag(pool_w),
        pool_scale=vec(pool_scale),
        w_out=_pack_w_out(w_out),
        ln1_g=vec(ln1_g), ln1_b=vec(ln1_b), ln2_g=vec(ln2_g), ln2_b=vec(ln2_b),
        w1=_to_bf16(w_ffn1), w3=_to_bf16(w_ffn3), w2=_to_bf16(w_ffn2))


def _retention_tables():
    ch = SEQ_BLOCK
    gamma = (1.0 - 2.0 ** (-5.0 - np.arange(RET_HEADS, dtype=np.float32))).astype(np.float32)
    lg = np.log(gamma).astype(np.float32)
    i = np.arange(ch, dtype=np.float32)
    diff = i[:, None] - i[None, :]
    dm = np.where(diff >= 0, np.exp(lg[:, None, None] * np.maximum(diff, 0.0)), 0.0).astype(np.float32)
    xi = np.exp(lg[:, None] * (i + 1.0)).astype(np.float32)
    zeta = np.exp(lg[:, None] * (ch - 1.0 - i)).astype(np.float32)
    lane = np.arange(2 * LANES)
    within = lane % LANES
    lane_head = np.where(within < RET_HEADS * RET_HALF, within // RET_HALF, -1)
    xiq = np.zeros((ch, 2 * LANES), np.float32)
    zk = np.zeros((ch, 2 * LANES), np.float32)
    bm = np.zeros((2 * LANES, RETV_W), np.float32)
    for h in range(RET_HEADS):
        sel = lane_head == h
        xiq[:, sel] = xi[h][:, None]
        zk[:, sel] = zeta[h][:, None]
        bm[sel, h * LANES:(h + 1) * LANES] = 1.0
    dec = np.repeat(np.exp(lg * ch).astype(np.float32), LANES)[None, :]
    return dict(xiq=jnp.asarray(xiq), zk=jnp.asarray(zk), dm=jnp.asarray(dm),
                dec=jnp.asarray(dec), bm=jnp.asarray(bm))


def _rotary_tables(lp):
    pos = jnp.arange(lp, dtype=F32)
    inv_freq = ROPE_BASE ** (-jnp.arange(RET_HALF, dtype=F32) / RET_HALF)
    ang = pos[:, None] * inv_freq[None, :]
    pad = LANES - RET_HEADS * RET_HALF
    cos = jnp.pad(jnp.tile(jnp.cos(ang), (1, RET_HEADS)), ((0, 0), (0, pad)))
    sin = jnp.pad(jnp.tile(jnp.sin(ang), (1, RET_HEADS)), ((0, 0), (0, pad)))
    return cos, sin


def kernel(x, meta, ln_emb_g, ln_emb_b, w_in, b_f, ret_gn_g, pool_w, pool_scale, w_out, ln1_g, ln1_b,
           w_ffn1, w_ffn3, w_ffn2, ln2_g, ln2_b):
    bsz, seq, d = x.shape
    assert d == D_MODEL and seq % SEQ_BLOCK == 0
    depth = w_in.shape[0]
    assert depth == DEPTH
    lp = seq + SEQ_BLOCK
    rows = bsz * lp
    tm = ROW_TILE if rows % ROW_TILE == 0 else SEQ_BLOCK

    consts = _retention_tables()
    consts["cos"], consts["sin"] = _rotary_tables(lp)
    params = _prepare_params(w_in, b_f, ret_gn_g, pool_w, pool_scale, w_out, ln1_g, ln1_b,
                             w_ffn1, w_ffn3, w_ffn2, ln2_g, ln2_b)

    h, ret, qkv, flog = _embed_inproj(x, meta, ln_emb_g, ln_emb_b, params["w_in"], lp)
    for l in range(depth):
        if l > 0:
            ret, qkv, flog = _inproj(h, params["w_in"], l, tm)
        o_r, o_p, cum = _seq_mix(ret, flog, consts, params, l, bsz, lp)
        o_f = _fox(qkv, cum, bsz, lp)
        h = _mix_ffn(o_r, o_p, o_f, h, params, l, tm)
    return h.reshape(bsz, lp, d)[:, N_META:N_META + seq]
```

```python
import numpy as np
import jax
import jax.numpy as jnp
from jax import lax
from jax.experimental import pallas as pl
from jax.experimental.pallas import tpu as pltpu

F32 = jnp.float32
BF16 = jnp.bfloat16

D_MODEL = 1024
N_META = 16
RET_HEADS = 4
RET_DK = 48
RET_HALF = RET_DK // 2
RET_DV = 96
RET_QK = RET_HEADS * RET_DK
RET_W = RET_HEADS * RET_DV
POOL_WINDOWS = (2, 4, 8, 16)
POOL_GROUP = 64
POOL_W = len(POOL_WINDOWS) * POOL_GROUP
FOX_HEADS = 6
FOX_DH = 64
FOX_W = FOX_HEADS * FOX_DH
D_FF = 2816
ROPE_BASE = 10000.0
LN_EPS = 1e-5
NEG_INF = -1e30
DEPTH = 2
ALPHA = (2.0 * DEPTH) ** 0.25
LOG2E = 1.4426950408889634

LANES = 128
SEQ_BLOCK = 256
ROW_TILE = 1024
VMEM_LIMIT = 56 * 1024 * 1024

RET_HEAD_PAD = LANES
QR_OFF = 0
KR_OFF = 2 * LANES
VR_OFF = 4 * LANES
GR_OFF = VR_OFF + RET_HEADS * RET_HEAD_PAD
UP_OFF = GR_OFF + RET_HEADS * RET_HEAD_PAD
QF_OFF = UP_OFF + POOL_W
KF_OFF = QF_OFF + FOX_W
VF_OFF = KF_OFF + FOX_W
FL_OFF = VF_OFF + FOX_W
RET_COLS = QF_OFF
FOX_COLS = 3 * FOX_W
N_PACK = FL_OFF + LANES
RETV_W = RET_HEADS * RET_HEAD_PAD
MIX_W = RETV_W + POOL_W + FOX_W
PROJ_CHUNK = 512
FF_CHUNK = 256
CAST_SLAB_ELEMS = 768 * 1024
SEQ_PAR = 4
MIX_FFN_SPLIT = 2


def _layer_norm(x, g, b):
    mu = jnp.mean(x, axis=-1, keepdims=True)
    d = x - mu
    var = jnp.mean(d * d, axis=-1, keepdims=True)
    return d * lax.rsqrt(var + LN_EPS) * g + b


def _sigmoid(x):
    return 1.0 / (1.0 + jnp.exp(-x))


def _project(xb, w_ref, cs_ref, ret_ref, fox_ref, flog_ref):
    def cols(c0, width):
        r = jnp.dot(xb, w_ref[:, c0:c0 + width], preferred_element_type=F32)
        return r * cs_ref[:, c0:c0 + width]

    for c0 in range(0, RET_COLS, PROJ_CHUNK):
        width = min(PROJ_CHUNK, RET_COLS - c0)
        ret_ref[:, c0:c0 + width] = cols(c0, width).astype(BF16)
    for c0 in range(0, FOX_COLS - LANES, PROJ_CHUNK):
        fox_ref[:, c0:c0 + PROJ_CHUNK] = cols(QF_OFF + c0, PROJ_CHUNK).astype(BF16)
    tail = cols(FL_OFF - LANES, 2 * LANES)
    fox_ref[:, FOX_COLS - LANES:] = tail[:, :LANES].astype(BF16)
    flog_ref[...] = tail[:, LANES:]


def _embed_inproj_kernel(meta_ref, xm_ref, xe_ref, g_ref, b_ref, w_ref, cs_ref,
                         h_ref, ret_ref, fox_ref, flog_ref):
    j = pl.program_id(1)
    last = pl.num_programs(1) - 1
    top = jnp.where(j == 0, meta_ref[...], xe_ref[...])
    body = jnp.where(j == last, 0.0, xm_ref[0:SEQ_BLOCK - N_META, :])
    rows = jnp.concatenate([top, body], axis=0)
    h = _layer_norm(rows, g_ref[...], b_ref[...])
    h_ref[...] = h
    _project(h.astype(BF16), w_ref, cs_ref, ret_ref, fox_ref, flog_ref)


def _embed_inproj(x, meta, g, b, w, lp):
    bsz, seq, d = x.shape
    nblk = lp // SEQ_BLOCK
    n_xblk = seq // SEQ_BLOCK
    per = SEQ_BLOCK // N_META
    rows = bsz * lp
    out_row = lambda width: pl.BlockSpec((SEQ_BLOCK, width), lambda bb, j: (bb * nblk + j, 0))
    return pl.pallas_call(
        _embed_inproj_kernel,
        out_shape=(jax.ShapeDtypeStruct((rows, d), F32),
                   jax.ShapeDtypeStruct((rows, RET_COLS), BF16),
                   jax.ShapeDtypeStruct((rows, FOX_COLS), BF16),
                   jax.ShapeDtypeStruct((rows, LANES), F32)),
        grid=(bsz, nblk),
        in_specs=[
            pl.BlockSpec((N_META, d), lambda bb, j: (0, 0)),
            pl.BlockSpec((None, SEQ_BLOCK, d), lambda bb, j: (bb, jnp.minimum(j, n_xblk - 1), 0)),
            pl.BlockSpec((None, N_META, d), lambda bb, j: (bb, jnp.maximum(per * j - 1, 0), 0)),
            pl.BlockSpec((1, d), lambda bb, j: (0, 0)),
            pl.BlockSpec((1, d), lambda bb, j: (0, 0)),
            pl.BlockSpec((None, D_MODEL, N_PACK), lambda bb, j: (0, 0, 0)),
            pl.BlockSpec((1, N_PACK), lambda bb, j: (0, 0)),
        ],
        out_specs=(out_row(d), out_row(RET_COLS), out_row(FOX_COLS), out_row(LANES)),
        compiler_params=pltpu.CompilerParams(dimension_semantics=("parallel", "arbitrary"),
                                             vmem_limit_bytes=VMEM_LIMIT),
        name="embed_in_proj",
    )(meta, x, x, g.reshape(1, d), b.reshape(1, d), w, _inproj_col_scale())


def _inproj_kernel(h_ref, w_ref, cs_ref, ret_ref, fox_ref, flog_ref):
    _project(h_ref[...].astype(BF16), w_ref, cs_ref, ret_ref, fox_ref, flog_ref)


def _inproj_col_scale():
    cs = np.ones((1, N_PACK), np.float32)
    cs[0, KR_OFF:KR_OFF + 2 * LANES] = RET_DK ** -0.5
    cs[0, QF_OFF:QF_OFF + FOX_W] = FOX_DH ** -0.5 * LOG2E
    return jnp.asarray(cs)


def _inproj(h, w, layer, tm):
    rows = h.shape[0]
    return pl.pallas_call(
        _inproj_kernel,
        out_shape=(jax.ShapeDtypeStruct((rows, RET_COLS), BF16),
                   jax.ShapeDtypeStruct((rows, FOX_COLS), BF16),
                   jax.ShapeDtypeStruct((rows, LANES), F32)),
        grid=(rows // tm,),
        in_specs=[
            pl.BlockSpec((tm, D_MODEL), lambda i: (i, 0)),
            pl.BlockSpec((None, D_MODEL, N_PACK), lambda i: (layer, 0, 0)),
            pl.BlockSpec((1, N_PACK), lambda i: (0, 0)),
        ],
        out_specs=(pl.BlockSpec((tm, RET_COLS), lambda i: (i, 0)),
                   pl.BlockSpec((tm, FOX_COLS), lambda i: (i, 0)),
                   pl.BlockSpec((tm, LANES), lambda i: (i, 0))),
        compiler_params=pltpu.CompilerParams(dimension_semantics=("parallel",),
                                             vmem_limit_bytes=VMEM_LIMIT),
        name="in_proj",
    )(h, w, _inproj_col_scale())


def _seq_kernel(qk_ref, v_ref, g_ref, u_ref, fl_ref, cos_ref, sin_ref,
                xiq_ref, zk_ref, dm_ref, dec_ref, bm_ref, gng_ref, bf_ref, pw_ref, ps_ref,
                or_ref, op_ref, c_ref,
                state_sc, tail_sc, carry_sc):
    for s in range(qk_ref.shape[0]):
        _seq_one(qk_ref.at[s], v_ref.at[s], g_ref.at[s], u_ref.at[s], fl_ref.at[s], cos_ref, sin_ref,
                 xiq_ref, zk_ref, dm_ref, dec_ref, bm_ref, gng_ref, bf_ref, pw_ref, ps_ref,
                 or_ref.at[s], op_ref.at[s], c_ref.at[s],
                 state_sc.at[s], tail_sc.at[s], carry_sc.at[s])


def _seq_one(qk_ref, v_ref, g_ref, u_ref, fl_ref, cos_ref, sin_ref,
             xiq_ref, zk_ref, dm_ref, dec_ref, bm_ref, gng_ref, bf_ref, pw_ref, ps_ref,
             or_ref, op_ref, c_ref,
             state_sc, tail_sc, carry_sc):
    j = pl.program_id(1)
    ch = SEQ_BLOCK

    @pl.when(j == 0)
    def _():
        state_sc[...] = jnp.zeros_like(state_sc)
        tail_sc[...] = jnp.zeros_like(tail_sc)
        carry_sc[...] = jnp.zeros_like(carry_sc)

    qk = qk_ref[...].astype(F32)
    cs = cos_ref[...]
    sn = sin_ref[...]
    q1, q2 = qk[:, 0:LANES], qk[:, LANES:2 * LANES]
    k1, k2 = qk[:, 2 * LANES:3 * LANES], qk[:, 3 * LANES:4 * LANES]
    qr = jnp.concatenate([q1 * cs - q2 * sn, q1 * sn + q2 * cs], axis=-1)
    kr = jnp.concatenate([k1 * cs - k2 * sn, k1 * sn + k2 * cs], axis=-1)
    qb = qr.astype(BF16)
    qx = (qr * xiq_ref[...]).astype(BF16)
    kb = kr.astype(BF16)
    kz = (kr * zk_ref[...]).astype(BF16)
    v = v_ref[...]
    st = state_sc[...]
    cross = jnp.dot(qx, st.astype(BF16), preferred_element_type=F32)
    qlane = lax.broadcasted_iota(jnp.int32, (1, 2 * LANES), 1)
    qhead = jnp.where(qlane % LANES < RET_HEADS * RET_HALF, (qlane % LANES) // RET_HALF, RET_HEADS)
    inner = []
    for h in range(RET_HEADS):
        qh = jnp.where(qhead == h, qb, jnp.zeros_like(qb))
        s = lax.dot_general(qh, kb, (((1,), (1,)), ((), ())), preferred_element_type=F32)
        p = (s * dm_ref[h]).astype(BF16)
        inner.append(jnp.dot(p, v[:, h * LANES:(h + 1) * LANES], preferred_element_type=F32))
    o = jnp.concatenate(inner, axis=-1) + cross
    kv = lax.dot_general(kz, v, (((0,), (0,)), ((), ())), preferred_element_type=F32)
    state_sc[...] = st * dec_ref[...] + kv * bm_ref[...]

    vlane = lax.broadcasted_iota(jnp.int32, (1, LANES), 1) < RET_DV
    normed = []
    for h in range(RET_HEADS):
        xh = o[:, h * LANES:(h + 1) * LANES]
        mu = jnp.sum(xh, axis=-1, keepdims=True) * (1.0 / RET_DV)
        d = jnp.where(vlane, xh - mu, 0.0)
        var = jnp.sum(d * d, axis=-1, keepdims=True) * (1.0 / RET_DV)
        normed.append(d * lax.rsqrt(var + LN_EPS))
    y = jnp.concatenate(normed, axis=-1) * gng_ref[...]
    gate = g_ref[...].astype(F32)
    or_ref[...] = (gate * _sigmoid(gate) * y).astype(BF16)

    u = u_ref[...].astype(F32)
    tail_rows = tail_sc.shape[0]
    ext = jnp.concatenate([tail_sc[...], u], axis=0)
    tail_sc[...] = u[ch - tail_rows:, :]
    e2 = ext + pltpu.roll(ext, 1, 0)
    e4 = e2 + pltpu.roll(e2, 2, 0)
    e8 = e4 + pltpu.roll(e4, 4, 0)
    e16 = e8 + pltpu.roll(e8, 8, 0)
    glane = lax.broadcasted_iota(jnp.int32, (1, POOL_W), 1) // POOL_GROUP
    win = jnp.where(glane == 0, e2, jnp.where(glane == 1, e4, jnp.where(glane == 2, e8, e16)))
    win = win[tail_rows:, :]
    wlen = jnp.where(glane == 0, 2, jnp.where(glane == 1, 4, jnp.where(glane == 2, 8, 16)))
    pos = j * ch + lax.broadcasted_iota(jnp.int32, (ch, POOL_W), 0)
    cnt = jnp.minimum(pos + 1, wlen).astype(F32)
    pooled = (win / cnt - u).astype(BF16)
    yp = jnp.dot(pooled, pw_ref[...], preferred_element_type=F32) * ps_ref[...]
    op_ref[...] = yp.astype(BF16)

    z = (fl_ref[...] + bf_ref[...]).T[0:8, :]
    logf = jnp.minimum(z, 0.0) - jnp.log1p(jnp.exp(-jnp.abs(z)))
    pos_in = lax.broadcasted_iota(jnp.int32, (8, ch), 1)
    sh = 1
    while sh < ch:
        logf = logf + jnp.where(pos_in >= sh, pltpu.roll(logf, sh, 1), 0.0)
        sh *= 2
    c_t = logf + jnp.concatenate([carry_sc[...]] * (ch // LANES), axis=1)
    carry_sc[...] = jnp.broadcast_to(c_t[:, ch - 1:ch], carry_sc.shape)
    c = jnp.concatenate([c_t, jnp.zeros((LANES - 8, ch), F32)], axis=0).T
    c_ref[...] = c * LOG2E


def _seq_mix(proj, flog, consts, params, layer, bsz, lp):
    ch = SEQ_BLOCK
    nc = lp // ch
    rows = bsz * lp

    par = SEQ_PAR if bsz % SEQ_PAR == 0 else 1

    def rowblk(width, colblk):
        return pl.BlockSpec((par, ch, width), lambda g, j: (g, j, colblk))

    def const(shape):
        nd = len(shape)
        return pl.BlockSpec(shape, lambda bb, j: (0,) * nd)

    def layer_param(shape):
        return pl.BlockSpec((None,) + shape, lambda bb, j: (layer,) + (0,) * len(shape))

    in_specs = [
        rowblk(4 * LANES, QR_OFF // (4 * LANES)),
        rowblk(RETV_W, VR_OFF // RETV_W),
        rowblk(RETV_W, GR_OFF // RETV_W),
        rowblk(POOL_W, UP_OFF // POOL_W),
        rowblk(LANES, 0),
        pl.BlockSpec((ch, LANES), lambda bb, j: (j, 0)),
        pl.BlockSpec((ch, LANES), lambda bb, j: (j, 0)),
        const((ch, 2 * LANES)), const((ch, 2 * LANES)), const((RET_HEADS, ch, ch)),
        const((1, RETV_W)), const((2 * LANES, RETV_W)),
        layer_param((1, RETV_W)), layer_param((1, LANES)), layer_param((POOL_W, POOL_W)),
        layer_param((1, POOL_W)),
    ]
    out_shape = (jax.ShapeDtypeStruct((bsz, lp, RETV_W), BF16),
                 jax.ShapeDtypeStruct((bsz, lp, POOL_W), BF16),
                 jax.ShapeDtypeStruct((bsz, lp, LANES), F32))
    out_specs = (rowblk(RETV_W, 0), rowblk(POOL_W, 0), rowblk(LANES, 0))
    proj = proj.reshape(bsz, lp, proj.shape[-1])
    flog = flog.reshape(bsz, lp, LANES)
    o_r, o_p, cum = pl.pallas_call(
        _seq_kernel,
        out_shape=out_shape,
        grid=(bsz // par, nc),
        in_specs=in_specs,
        out_specs=out_specs,
        scratch_shapes=[pltpu.VMEM((par, 2 * LANES, RETV_W), F32),
                        pltpu.VMEM((par, 16, POOL_W), F32),
                        pltpu.VMEM((par, 8, LANES), F32)],
        compiler_params=pltpu.CompilerParams(dimension_semantics=("parallel", "arbitrary"),
                                             vmem_limit_bytes=VMEM_LIMIT),
        name="seq_mixers",
    )(proj, proj, proj, proj, flog, consts["cos"], consts["sin"],
      consts["xiq"], consts["zk"], consts["dm"], consts["dec"], consts["bm"],
      params["gn_g"], params["b_f"], params["pool_w"], params["pool_scale"])
    return o_r.reshape(rows, RETV_W), o_p.reshape(rows, POOL_W), cum.reshape(rows, LANES)


FOX_AUG = 3
FOX_VROWS = FOX_DH + 16
FOX_LATE_HEADS = 2
FOX_UNROLL = 8


def _fox_select_matrix():
    sel = np.zeros((LANES, FOX_HEADS * LANES), np.float32)
    for a in range(FOX_AUG):
        for h in range(FOX_HEADS):
            sel[8 * a + h, h * LANES + FOX_DH + a] = -1.0
    return jnp.asarray(sel, BF16)


def _fox_kernel(q_ref, k_ref, v_ref, c_ref, sel_ref, o_ref,
                ka_sc, vt_sc, qa_sc, st_sc, m_sc, acc_sc):
    i = pl.program_id(1)
    tq = SEQ_BLOCK
    tk = SEQ_BLOCK
    npair = FOX_HEADS // 2
    lane = lax.broadcasted_iota(jnp.int32, (1, LANES), 1)

    @pl.when(i == 0)
    def _():
        def chunk(t, carry):
            r0 = pl.multiple_of(t * tk, tk)
            cc = c_ref[pl.ds(r0, tk), :]
            pieces, rem = [], cc
            for a in range(FOX_AUG):
                piece = rem.astype(BF16).astype(F32)
                pieces.append(piece if a == 0 else pltpu.roll(piece, 8 * a, 1))
                rem = rem - piece
            packed = jnp.where(lane < 8, pieces[0], jnp.where(lane < 16, pieces[1], pieces[2]))
            c_aug = jnp.dot(packed.astype(BF16), sel_ref[...], preferred_element_type=F32)
            for pr in range(npair):
                kk = k_ref[pl.ds(r0, tk), pr * LANES:(pr + 1) * LANES].astype(F32)
                vv = v_ref[pl.ds(r0, tk), pr * LANES:(pr + 1) * LANES].astype(F32)
                vtt = vv.T
                ones = jnp.ones((FOX_VROWS - FOX_DH, tk), F32)
                k_heads = (kk, pltpu.roll(kk, FOX_DH, 1))
                v_heads = (vtt[:FOX_DH], vtt[FOX_DH:])
                for hh in range(2):
                    h = 2 * pr + hh
                    aug = jnp.where(lane < FOX_DH, k_heads[hh], c_aug[:, h * LANES:(h + 1) * LANES])
                    ka_sc[h, pl.ds(r0, tk), :] = aug.astype(BF16)
                    vt_sc[h, :, pl.ds(r0, tk)] = jnp.concatenate([v_heads[hh], ones], axis=0).astype(BF16)
            return carry

        lax.fori_loop(0, k_ref.shape[0] // tk, chunk, 0)

    for pr in range(npair):
        qq = q_ref[:, pr * LANES:(pr + 1) * LANES].astype(F32)
        tail = jnp.where(lane < FOX_DH + FOX_AUG, 1.0, 0.0)
        qa_sc[2 * pr] = jnp.where(lane < FOX_DH, qq, tail).astype(BF16)
        qa_sc[2 * pr + 1] = jnp.where(lane < FOX_DH, pltpu.roll(qq, FOX_DH, 1), tail).astype(BF16)
    m_sc[...] = jnp.full(m_sc.shape, NEG_INF, F32)
    acc_sc[...] = jnp.zeros_like(acc_sc)

    def score(t, h):
        ks = t * tk if isinstance(t, int) else pl.multiple_of(t * tk, tk)
        st_sc[h] = lax.dot_general(ka_sc[h, pl.ds(ks, tk), :], qa_sc[h], (((1,), (1,)), ((), ())),
                                   preferred_element_type=F32)

    def attend(t, masked, next_t):
        ks = t * tk if isinstance(t, int) else pl.multiple_of(t * tk, tk)

        def load(h, c0):
            s = st_sc[h, :, c0:c0 + LANES]
            if masked:
                k_id = lax.broadcasted_iota(jnp.int32, (tk, LANES), 0)
                q_id = c0 + lax.broadcasted_iota(jnp.int32, (tk, LANES), 1)
                s = jnp.where(k_id <= q_id, s, NEG_INF)
            return s

        for h in range(early, FOX_HEADS):
            score(t, h)
        for h in range(FOX_HEADS):
            p_halves, a_halves = [], []
            for c0 in range(0, tq, LANES):
                m_prev = m_sc[h, :, c0:c0 + LANES]
                m_new = jnp.maximum(m_prev, jnp.max(load(h, c0), axis=0, keepdims=True))
                m_sc[h, :, c0:c0 + LANES] = m_new
                a_halves.append(jnp.exp2(m_prev - m_new))
                p_halves.append(jnp.exp2(load(h, c0) - m_new).astype(BF16))
            p_t = jnp.concatenate(p_halves, axis=1)
            pv = jnp.dot(vt_sc[h, :, pl.ds(ks, tk)], p_t, preferred_element_type=F32)
            acc_sc[h] = jnp.concatenate(a_halves, axis=1) * acc_sc[h] + pv
            if next_t is not None and h < early:
                score(next_t, h)

    early = FOX_HEADS - FOX_LATE_HEADS
    for h in range(early):
        score(0, h)

    def run(t0, count):
        for d in range(count):
            attend(t0 + d, False, t0 + d + 1)

    def body(u, carry):
        run(FOX_UNROLL * u, FOX_UNROLL)
        return carry

    lax.fori_loop(0, i // FOX_UNROLL, body, 0)
    done = (i // FOX_UNROLL) * FOX_UNROLL
    span = FOX_UNROLL // 2
    while span >= 1:
        take = (i - done) >= span

        @pl.when(take)
        def _(done=done, span=span):
            run(done, span)

        done = done + jnp.where(take, span, 0)
        span //= 2

    attend(i, True, None)

    outs = []
    for pr in range(npair):
        acc_a = acc_sc[2 * pr]
        acc_b = acc_sc[2 * pr + 1]
        o_t = jnp.concatenate([acc_a[:FOX_DH] / acc_a[FOX_DH:FOX_DH + 1],
                               acc_b[:FOX_DH] / acc_b[FOX_DH:FOX_DH + 1]], axis=0)
        outs.append(o_t.T)
    o_ref[...] = jnp.concatenate(outs, axis=-1).astype(BF16)


def _fox(qkv, cum, bsz, lp):
    tq = SEQ_BLOCK
    nq = lp // tq
    rows = bsz * lp
    return pl.pallas_call(
        _fox_kernel,
        out_shape=jax.ShapeDtypeStruct((rows, FOX_W), BF16),
        grid=(bsz, nq),
        in_specs=[
            pl.BlockSpec((tq, FOX_W), lambda bb, i: (bb * nq + i, 0)),
            pl.BlockSpec((lp, FOX_W), lambda bb, i: (bb, 1)),
            pl.BlockSpec((lp, FOX_W), lambda bb, i: (bb, 2)),
            pl.BlockSpec((lp, LANES), lambda bb, i: (bb, 0)),
            pl.BlockSpec((LANES, FOX_HEADS * LANES), lambda bb, i: (0, 0)),
        ],
        out_specs=pl.BlockSpec((tq, FOX_W), lambda bb, i: (bb * nq + i, 0)),
        scratch_shapes=[pltpu.VMEM((FOX_HEADS, lp, LANES), BF16),
                        pltpu.VMEM((FOX_HEADS, FOX_VROWS, lp), BF16),
                        pltpu.VMEM((FOX_HEADS, tq, LANES), BF16),
                        pltpu.VMEM((FOX_HEADS, SEQ_BLOCK, tq), F32),
                        pltpu.VMEM((FOX_HEADS, 1, tq), F32),
                        pltpu.VMEM((FOX_HEADS, FOX_VROWS, tq), F32)],
        compiler_params=pltpu.CompilerParams(
            dimension_semantics=("parallel", "arbitrary"),
            vmem_limit_bytes=VMEM_LIMIT),
        name="fox_attention",
    )(qkv, qkv, qkv, cum, _fox_select_matrix())


def _mix_ffn_kernel(or_ref, op_ref, of_ref, h_ref, wo_ref, g1_ref, b1_ref,
                    w1_ref, w3_ref, w2_ref, g2_ref, b2_ref, o_ref):
    tm = h_ref.shape[0]
    sub = tm // MIX_FFN_SPLIT
    parts = [slice(s * sub, (s + 1) * sub) for s in range(MIX_FFN_SPLIT)]
    ys = []
    for rows in parts:
        mix = jnp.concatenate([or_ref[rows, :], op_ref[rows, :], of_ref[rows, :]], axis=-1)
        ys.append(ALPHA * h_ref[rows, :] + jnp.dot(mix, wo_ref[...], preferred_element_type=F32))
    h1s = [_layer_norm(y, g1_ref[...], b1_ref[...]) for y in ys]
    accs = []
    for h1 in h1s:
        xb = h1.astype(BF16)
        acc = None
        for c0 in range(0, D_FF, FF_CHUNK):
            a = jnp.dot(xb, w1_ref[:, c0:c0 + FF_CHUNK], preferred_element_type=F32)
            b = jnp.dot(xb, w3_ref[:, c0:c0 + FF_CHUNK], preferred_element_type=F32)
            t = (a * _sigmoid(a) * b).astype(BF16)
            part = jnp.dot(t, w2_ref[c0:c0 + FF_CHUNK, :], preferred_element_type=F32)
            acc = part if acc is None else acc + part
        accs.append(acc)
    for rows, h1, acc in zip(parts, h1s, accs):
        o_ref[rows, :] = _layer_norm(ALPHA * h1 + acc, g2_ref[...], b2_ref[...])


def _mix_ffn(o_r, o_p, o_f, h, params, layer, tm):
    rows = h.shape[0]

    def row(width):
        return pl.BlockSpec((tm, width), lambda i: (i, 0))

    def resident(shape):
        return pl.BlockSpec((None,) + shape, lambda i: (layer,) + (0,) * len(shape),
                            pipeline_mode=pl.Buffered(1))

    vec = resident((1, D_MODEL))
    return pl.pallas_call(
        _mix_ffn_kernel,
        out_shape=jax.ShapeDtypeStruct((rows, D_MODEL), F32),
        grid=(rows // tm,),
        in_specs=[row(RETV_W), row(POOL_W), row(FOX_W), row(D_MODEL),
                  resident((MIX_W, D_MODEL)), vec, vec,
                  resident((D_MODEL, D_FF)), resident((D_MODEL, D_FF)), resident((D_FF, D_MODEL)), vec, vec],
        out_specs=row(D_MODEL),
        compiler_params=pltpu.CompilerParams(dimension_semantics=("parallel",),
                                             vmem_limit_bytes=VMEM_LIMIT),
        name="mix_ffn_ln",
    )(o_r, o_p, o_f, h, params["w_out"], params["ln1_g"], params["ln1_b"],
      params["w1"], params["w3"], params["w2"], params["ln2_g"], params["ln2_b"])


def _pack_w_in_kernel(w_ref, o_ref):
    w = w_ref[...]
    rows = w.shape[0]
    pad = jnp.zeros((rows, LANES - RET_HEADS * RET_HALF), w.dtype)
    pieces = []
    for base in (0, RET_QK):
        for half in range(2):
            for h in range(RET_HEADS):
                lo = base + h * RET_DK + half * RET_HALF
                pieces.append(w[:, lo:lo + RET_HALF])
            pieces.append(pad)
    for base in (2 * RET_QK, 2 * RET_QK + RET_W):
        for h in range(RET_HEADS):
            pieces.append(w[:, base + h * RET_DV:base + (h + 1) * RET_DV])
            pieces.append(pad)
    aligned = 2 * RET_QK + 2 * RET_W
    pieces.append(w[:, aligned:aligned + POOL_W + 3 * FOX_W])
    pieces.append(w[:, aligned + POOL_W + 3 * FOX_W:])
    pieces.append(jnp.zeros((rows, LANES - FOX_HEADS), w.dtype))
    o_ref[...] = jnp.concatenate(pieces, axis=1).astype(BF16)


def _pack_w_in_call(w):
    depth, d, d_in = w.shape
    slab = SEQ_BLOCK
    return pl.pallas_call(
        _pack_w_in_kernel,
        out_shape=jax.ShapeDtypeStruct((depth, d, N_PACK), BF16),
        grid=(depth, d // slab),
        in_specs=[pl.BlockSpec((None, slab, d_in), lambda l, i: (l, i, 0))],
        out_specs=pl.BlockSpec((None, slab, N_PACK), lambda l, i: (l, i, 0)),
        compiler_params=pltpu.CompilerParams(dimension_semantics=("parallel", "parallel")),
        name="pack_w_in",
    )(w)


def _pack_w_out(w):
    depth, _, d = w.shape
    w_r = jnp.pad(w[:, :RET_W].reshape(depth, RET_HEADS, RET_DV, d),
                  ((0, 0), (0, 0), (0, RET_HEAD_PAD - RET_DV), (0, 0))).reshape(depth, RETV_W, d)
    return jnp.concatenate([w_r, w[:, RET_W:]], axis=1).astype(BF16)


def _pad_heads(vec):
    depth = vec.shape[0]
    padded = jnp.pad(vec.reshape(depth, RET_HEADS, RET_DV), ((0, 0), (0, 0), (0, RET_HEAD_PAD - RET_DV)))
    return padded.reshape(depth, 1, RETV_W)


def _block_diag(pw):
    g = len(POOL_WINDOWS)
    eye = jnp.eye(g, dtype=pw.dtype)
    return jnp.einsum("lgij,gh->lgihj", pw, eye).reshape(pw.shape[0], POOL_W, POOL_W).astype(BF16)


def _cast_kernel(x_ref, o_ref):
    o_ref[...] = x_ref[...].astype(o_ref.dtype)


def _to_bf16(w):
    depth, k, n = w.shape
    slab = max(s for s in range(16, k + 1, 16) if k % s == 0 and s * n <= CAST_SLAB_ELEMS)
    spec = pl.BlockSpec((None, slab, n), lambda l, i: (l, i, 0))
    return pl.pallas_call(
        _cast_kernel,
        out_shape=jax.ShapeDtypeStruct(w.shape, BF16),
        grid=(depth, k // slab),
        in_specs=[spec],
        out_specs=spec,
        compiler_params=pltpu.CompilerParams(dimension_semantics=("parallel", "parallel")),
        name="weights_to_bf16",
    )(w)


def _prepare_params(w_in, b_f, ret_gn_g, pool_w, pool_scale, w_out, ln1_g, ln1_b,
                    w_ffn1, w_ffn3, w_ffn2, ln2_g, ln2_b):
    depth = w_in.shape[0]
    vec = lambda a: a.reshape(depth, 1, a.shape[-1])
    return dict(
        w_in=_pack_w_in_call(w_in),
        gn_g=_pad_heads(ret_gn_g),
        b_f=vec(jnp.pad(b_f, ((0, 0), (0, LANES - FOX_HEADS)))),
        pool_w=_block_diag(pool_w),
        pool_scale=vec(pool_scale),
        w_out=_pack_w_out(w_out),
        ln1_g=vec(ln1_g), ln1_b=vec(ln1_b), ln2_g=vec(ln2_g), ln2_b=vec(ln2_b),
        w1=_to_bf16(w_ffn1), w3=_to_bf16(w_ffn3), w2=_to_bf16(w_ffn2))


def _retention_tables():
    ch = SEQ_BLOCK
    gamma = (1.0 - 2.0 ** (-5.0 - np.arange(RET_HEADS, dtype=np.float32))).astype(np.float32)
    lg = np.log(gamma).astype(np.float32)
    i = np.arange(ch, dtype=np.float32)
    diff = i[:, None] - i[None, :]
    dm = np.where(diff >= 0, np.exp(lg[:, None, None] * np.maximum(diff, 0.0)), 0.0).astype(np.float32)
    xi = np.exp(lg[:, None] * (i + 1.0)).astype(np.float32)
    zeta = np.exp(lg[:, None] * (ch - 1.0 - i)).astype(np.float32)
    lane = np.arange(2 * LANES)
    within = lane % LANES
    lane_head = np.where(within < RET_HEADS * RET_HALF, within // RET_HALF, -1)
    xiq = np.zeros((ch, 2 * LANES), np.float32)
    zk = np.zeros((ch, 2 * LANES), np.float32)
    bm = np.zeros((2 * LANES, RETV_W), np.float32)
    for h in range(RET_HEADS):
        sel = lane_head == h
        xiq[:, sel] = xi[h][:, None]
        zk[:, sel] = zeta[h][:, None]
        bm[sel, h * LANES:(h + 1) * LANES] = 1.0
    dec = np.repeat(np.exp(lg * ch).astype(np.float32), LANES)[None, :]
    return dict(xiq=jnp.asarray(xiq), zk=jnp.asarray(zk), dm=jnp.asarray(dm),
                dec=jnp.asarray(dec), bm=jnp.asarray(bm))


def _rotary_tables(lp):
    pos = jnp.arange(lp, dtype=F32)
    inv_freq = ROPE_BASE ** (-jnp.arange(RET_HALF, dtype=F32) / RET_HALF)
    ang = pos[:, None] * inv_freq[None, :]
    pad = LANES - RET_HEADS * RET_HALF
    cos = jnp.pad(jnp.tile(jnp.cos(ang), (1, RET_HEADS)), ((0, 0), (0, pad)))
    sin = jnp.pad(jnp.tile(jnp.sin(ang), (1, RET_HEADS)), ((0, 0), (0, pad)))
    return cos, sin


def kernel(x, meta, ln_emb_g, ln_emb_b, w_in, b_f, ret_gn_g, pool_w, pool_scale, w_out, ln1_g, ln1_b,
           w_ffn1, w_ffn3, w_ffn2, ln2_g, ln2_b):
    bsz, seq, d = x.shape
    assert d == D_MODEL and seq % SEQ_BLOCK == 0
    depth = w_in.shape[0]
    assert depth == DEPTH
    lp = seq + SEQ_BLOCK
    rows = bsz * lp
    tm = ROW_TILE if rows % ROW_TILE == 0 else SEQ_BLOCK

    consts = _retention_tables()
    consts["cos"], consts["sin"] = _rotary_tables(lp)
    params = _prepare_params(w_in, b_f, ret_gn_g, pool_w, pool_scale, w_out, ln1_g, ln1_b,
                             w_ffn1, w_ffn3, w_ffn2, ln2_g, ln2_b)

    h, ret, qkv, flog = _embed_inproj(x, meta, ln_emb_g, ln_emb_b, params["w_in"], lp)
    for l in range(depth):
        if l > 0:
            ret, qkv, flog = _inproj(h, params["w_in"], l, tm)
        o_r, o_p, cum = _seq_mix(ret, flog, consts, params, l, bsz, lp)
        o_f = _fox(qkv, cum, bsz, lp)
        h = _mix_ffn(o_r, o_p, o_f, h, params, l, tm)
    return h.reshape(bsz, lp, d)[:, N_META:N_META + seq]
```

```python
import numpy as np
import jax
import jax.numpy as jnp
from jax import lax
from jax.experimental import pallas as pl
from jax.experimental.pallas import tpu as pltpu

F32 = jnp.float32
BF16 = jnp.bfloat16

D_MODEL = 1024
N_META = 16
RET_HEADS = 4
RET_DK = 48
RET_HALF = RET_DK // 2
RET_DV = 96
RET_QK = RET_HEADS * RET_DK
RET_W = RET_HEADS * RET_DV
POOL_WINDOWS = (2, 4, 8, 16)
POOL_GROUP = 64
POOL_W = len(POOL_WINDOWS) * POOL_GROUP
FOX_HEADS = 6
FOX_DH = 64
FOX_W = FOX_HEADS * FOX_DH
D_FF = 2816
ROPE_BASE = 10000.0
LN_EPS = 1e-5
NEG_INF = -1e30
DEPTH = 2
ALPHA = (2.0 * DEPTH) ** 0.25
LOG2E = 1.4426950408889634

LANES = 128
SEQ_BLOCK = 256
ROW_TILE = 1024
VMEM_LIMIT = 56 * 1024 * 1024

RET_HEAD_PAD = LANES
QR_OFF = 0
KR_OFF = 2 * LANES
VR_OFF = 4 * LANES
GR_OFF = VR_OFF + RET_HEADS * RET_HEAD_PAD
UP_OFF = GR_OFF + RET_HEADS * RET_HEAD_PAD
QF_OFF = UP_OFF + POOL_W
KF_OFF = QF_OFF + FOX_W
VF_OFF = KF_OFF + FOX_W
FL_OFF = VF_OFF + FOX_W
RET_COLS = QF_OFF
FOX_COLS = 3 * FOX_W
N_PACK = FL_OFF + LANES
RETV_W = RET_HEADS * RET_HEAD_PAD
MIX_W = RETV_W + POOL_W + FOX_W
PROJ_CHUNK = 512
FF_CHUNK = 256
CAST_SLAB_ELEMS = 768 * 1024
SEQ_PAR = 4
MIX_FFN_SPLIT = 2


def _layer_norm(x, g, b):
    mu = jnp.mean(x, axis=-1, keepdims=True)
    d = x - mu
    var = jnp.mean(d * d, axis=-1, keepdims=True)
    return d * lax.rsqrt(var + LN_EPS) * g + b


def _sigmoid(x):
    return 1.0 / (1.0 + jnp.exp(-x))


def _project(xb, w_ref, cs_ref, ret_ref, fox_ref, flog_ref):
    def cols(c0, width):
        r = jnp.dot(xb, w_ref[:, c0:c0 + width], preferred_element_type=F32)
        return r * cs_ref[:, c0:c0 + width]

    for c0 in range(0, RET_COLS, PROJ_CHUNK):
        width = min(PROJ_CHUNK, RET_COLS - c0)
        ret_ref[:, c0:c0 + width] = cols(c0, width).astype(BF16)
    for c0 in range(0, FOX_COLS - LANES, PROJ_CHUNK):
        fox_ref[:, c0:c0 + PROJ_CHUNK] = cols(QF_OFF + c0, PROJ_CHUNK).astype(BF16)
    tail = cols(FL_OFF - LANES, 2 * LANES)
    fox_ref[:, FOX_COLS - LANES:] = tail[:, :LANES].astype(BF16)
    flog_ref[...] = tail[:, LANES:]


def _embed_inproj_kernel(meta_ref, xm_ref, xe_ref, g_ref, b_ref, w_ref, cs_ref,
                         h_ref, ret_ref, fox_ref, flog_ref):
    j = pl.program_id(1)
    last = pl.num_programs(1) - 1
    top = jnp.where(j == 0, meta_ref[...], xe_ref[...])
    body = jnp.where(j == last, 0.0, xm_ref[0:SEQ_BLOCK - N_META, :])
    rows = jnp.concatenate([top, body], axis=0)
    h = _layer_norm(rows, g_ref[...], b_ref[...])
    h_ref[...] = h
    _project(h.astype(BF16), w_ref, cs_ref, ret_ref, fox_ref, flog_ref)


def _embed_inproj(x, meta, g, b, w, lp):
    bsz, seq, d = x.shape
    nblk = lp // SEQ_BLOCK
    n_xblk = seq // SEQ_BLOCK
    per = SEQ_BLOCK // N_META
    rows = bsz * lp
    out_row = lambda width: pl.BlockSpec((SEQ_BLOCK, width), lambda bb, j: (bb * nblk + j, 0))
    return pl.pallas_call(
        _embed_inproj_kernel,
        out_shape=(jax.ShapeDtypeStruct((rows, d), F32),
                   jax.ShapeDtypeStruct((rows, RET_COLS), BF16),
                   jax.ShapeDtypeStruct((rows, FOX_COLS), BF16),
                   jax.ShapeDtypeStruct((rows, LANES), F32)),
        grid=(bsz, nblk),
        in_specs=[
            pl.BlockSpec((N_META, d), lambda bb, j: (0, 0)),
            pl.BlockSpec((None, SEQ_BLOCK, d), lambda bb, j: (bb, jnp.minimum(j, n_xblk - 1), 0)),
            pl.BlockSpec((None, N_META, d), lambda bb, j: (bb, jnp.maximum(per * j - 1, 0), 0)),
            pl.BlockSpec((1, d), lambda bb, j: (0, 0)),
            pl.BlockSpec((1, d), lambda bb, j: (0, 0)),
            pl.BlockSpec((None, D_MODEL, N_PACK), lambda bb, j: (0, 0, 0)),
            pl.BlockSpec((1, N_PACK), lambda bb, j: (0, 0)),
        ],
        out_specs=(out_row(d), out_row(RET_COLS), out_row(FOX_COLS), out_row(LANES)),
        compiler_params=pltpu.CompilerParams(dimension_semantics=("parallel", "arbitrary"),
                                             vmem_limit_bytes=VMEM_LIMIT),
        name="embed_in_proj",
    )(meta, x, x, g.reshape(1, d), b.reshape(1, d), w, _inproj_col_scale())


def _inproj_kernel(h_ref, w_ref, cs_ref, ret_ref, fox_ref, flog_ref):
    _project(h_ref[...].astype(BF16), w_ref, cs_ref, ret_ref, fox_ref, flog_ref)


def _inproj_col_scale():
    cs = np.ones((1, N_PACK), np.float32)
    cs[0, KR_OFF:KR_OFF + 2 * LANES] = RET_DK ** -0.5
    cs[0, QF_OFF:QF_OFF + FOX_W] = FOX_DH ** -0.5 * LOG2E
    return jnp.asarray(cs)


def _inproj(h, w, layer, tm):
    rows = h.shape[0]
    return pl.pallas_call(
        _inproj_kernel,
        out_shape=(jax.ShapeDtypeStruct((rows, RET_COLS), BF16),
                   jax.ShapeDtypeStruct((rows, FOX_COLS), BF16),
                   jax.ShapeDtypeStruct((rows, LANES), F32)),
        grid=(rows // tm,),
        in_specs=[
            pl.BlockSpec((tm, D_MODEL), lambda i: (i, 0)),
            pl.BlockSpec((None, D_MODEL, N_PACK), lambda i: (layer, 0, 0)),
            pl.BlockSpec((1, N_PACK), lambda i: (0, 0)),
        ],
        out_specs=(pl.BlockSpec((tm, RET_COLS), lambda i: (i, 0)),
                   pl.BlockSpec((tm, FOX_COLS), lambda i: (i, 0)),
                   pl.BlockSpec((tm, LANES), lambda i: (i, 0))),
        compiler_params=pltpu.CompilerParams(dimension_semantics=("parallel",),
                                             vmem_limit_bytes=VMEM_LIMIT),
        name="in_proj",
    )(h, w, _inproj_col_scale())


def _seq_kernel(qk_ref, v_ref, g_ref, u_ref, fl_ref, cos_ref, sin_ref,
                xiq_ref, zk_ref, dm_ref, dec_ref, bm_ref, tw_ref, gng_ref, bf_ref, pw_ref, ps_ref,
                or_ref, op_ref, c_ref,
                state_sc, tail_sc, carry_sc):
    for s in range(qk_ref.shape[0]):
        _seq_one(qk_ref.at[s], v_ref.at[s], g_ref.at[s], u_ref.at[s], fl_ref.at[s], cos_ref, sin_ref,
                 xiq_ref, zk_ref, dm_ref, dec_ref, bm_ref, tw_ref, gng_ref, bf_ref, pw_ref, ps_ref,
                 or_ref.at[s], op_ref.at[s], c_ref.at[s],
                 state_sc.at[s], tail_sc.at[s], carry_sc.at[s])


def _seq_one(qk_ref, v_ref, g_ref, u_ref, fl_ref, cos_ref, sin_ref,
             xiq_ref, zk_ref, dm_ref, dec_ref, bm_ref, tw_ref, gng_ref, bf_ref, pw_ref, ps_ref,
             or_ref, op_ref, c_ref,
             state_sc, tail_sc, carry_sc):
    j = pl.program_id(1)
    ch = SEQ_BLOCK

    @pl.when(j == 0)
    def _():
        state_sc[...] = jnp.zeros_like(state_sc)
        tail_sc[...] = jnp.zeros_like(tail_sc)
        carry_sc[...] = jnp.zeros_like(carry_sc)

    qk = qk_ref[...].astype(F32)
    cs = cos_ref[...]
    sn = sin_ref[...]
    q1, q2 = qk[:, 0:LANES], qk[:, LANES:2 * LANES]
    k1, k2 = qk[:, 2 * LANES:3 * LANES], qk[:, 3 * LANES:4 * LANES]
    qr = jnp.concatenate([q1 * cs - q2 * sn, q1 * sn + q2 * cs], axis=-1)
    kr = jnp.concatenate([k1 * cs - k2 * sn, k1 * sn + k2 * cs], axis=-1)
    qb = qr.astype(BF16)
    qx = (qr * xiq_ref[...]).astype(BF16)
    kb = kr.astype(BF16)
    kz = (kr * zk_ref[...]).astype(BF16)
    v = v_ref[...]
    st = state_sc[...]
    cross = jnp.dot(qx, st.astype(BF16), preferred_element_type=F32)
    qlane = lax.broadcasted_iota(jnp.int32, (1, 2 * LANES), 1)
    qhead = jnp.where(qlane % LANES < RET_HEADS * RET_HALF, (qlane % LANES) // RET_HALF, RET_HEADS)
    inner = []
    for h in range(RET_HEADS):
        qh = jnp.where(qhead == h, qb, jnp.zeros_like(qb))
        s = lax.dot_general(qh, kb, (((1,), (1,)), ((), ())), preferred_element_type=F32)
        p = (s * dm_ref[h]).astype(BF16)
        inner.append(jnp.dot(p, v[:, h * LANES:(h + 1) * LANES], preferred_element_type=F32))
    o = jnp.concatenate(inner, axis=-1) + cross
    kv = lax.dot_general(kz, v, (((0,), (0,)), ((), ())), preferred_element_type=F32)
    state_sc[...] = st * dec_ref[...] + kv * bm_ref[...]

    vlane = lax.broadcasted_iota(jnp.int32, (1, LANES), 1) < RET_DV
    normed = []
    for h in range(RET_HEADS):
        xh = o[:, h * LANES:(h + 1) * LANES]
        mu = jnp.sum(xh, axis=-1, keepdims=True) * (1.0 / RET_DV)
        d = jnp.where(vlane, xh - mu, 0.0)
        var = jnp.sum(d * d, axis=-1, keepdims=True) * (1.0 / RET_DV)
        normed.append(d * lax.rsqrt(var + LN_EPS))
    y = jnp.concatenate(normed, axis=-1) * gng_ref[...]
    gate = g_ref[...].astype(F32)
    or_ref[...] = (gate * _sigmoid(gate) * y).astype(BF16)

    u_b = u_ref[...]
    u = u_b.astype(F32)
    both = jnp.concatenate([tail_sc[...], u_b], axis=0)
    tail_sc[...] = u_b
    glane = lax.broadcasted_iota(jnp.int32, (1, POOL_W), 1) // POOL_GROUP
    sums = [jnp.dot(tw_ref[g], both, preferred_element_type=F32) for g in range(len(POOL_WINDOWS))]
    win = jnp.where(glane == 0, sums[0], jnp.where(glane == 1, sums[1], jnp.where(glane == 2, sums[2], sums[3])))
    wlen = jnp.where(glane == 0, 2, jnp.where(glane == 1, 4, jnp.where(glane == 2, 8, 16)))
    pos = j * ch + lax.broadcasted_iota(jnp.int32, (ch, POOL_W), 0)
    cnt = jnp.minimum(pos + 1, wlen).astype(F32)
    pooled = (win / cnt - u).astype(BF16)
    yp = jnp.dot(pooled, pw_ref[...], preferred_element_type=F32) * ps_ref[...]
    op_ref[...] = yp.astype(BF16)

    z = (fl_ref[...] + bf_ref[...]).T[0:8, :]
    logf = jnp.minimum(z, 0.0) - jnp.log1p(jnp.exp(-jnp.abs(z)))
    pos_in = lax.broadcasted_iota(jnp.int32, (8, ch), 1)
    sh = 1
    while sh < ch:
        logf = logf + jnp.where(pos_in >= sh, pltpu.roll(logf, sh, 1), 0.0)
        sh *= 2
    c_t = logf + jnp.concatenate([carry_sc[...]] * (ch // LANES), axis=1)
    carry_sc[...] = jnp.broadcast_to(c_t[:, ch - 1:ch], carry_sc.shape)
    c = jnp.concatenate([c_t, jnp.zeros((LANES - 8, ch), F32)], axis=0).T
    c_ref[...] = c * LOG2E


def _seq_mix(proj, flog, consts, params, layer, bsz, lp):
    ch = SEQ_BLOCK
    nc = lp // ch
    rows = bsz * lp

    par = SEQ_PAR if bsz % SEQ_PAR == 0 else 1

    def rowblk(width, colblk):
        return pl.BlockSpec((par, ch, width), lambda g, j: (g, j, colblk))

    def const(shape):
        nd = len(shape)
        return pl.BlockSpec(shape, lambda bb, j: (0,) * nd)

    def layer_param(shape):
        return pl.BlockSpec((None,) + shape, lambda bb, j: (layer,) + (0,) * len(shape))

    in_specs = [
        rowblk(4 * LANES, QR_OFF // (4 * LANES)),
        rowblk(RETV_W, VR_OFF // RETV_W),
        rowblk(RETV_W, GR_OFF // RETV_W),
        rowblk(POOL_W, UP_OFF // POOL_W),
        rowblk(LANES, 0),
        pl.BlockSpec((ch, LANES), lambda bb, j: (j, 0)),
        pl.BlockSpec((ch, LANES), lambda bb, j: (j, 0)),
        const((ch, 2 * LANES)), const((ch, 2 * LANES)), const((RET_HEADS, ch, ch)),
        const((1, RETV_W)), const((2 * LANES, RETV_W)), const((len(POOL_WINDOWS), ch, 2 * ch)),
        layer_param((1, RETV_W)), layer_param((1, LANES)), layer_param((POOL_W, POOL_W)),
        layer_param((1, POOL_W)),
    ]
    out_shape = (jax.ShapeDtypeStruct((bsz, lp, RETV_W), BF16),
                 jax.ShapeDtypeStruct((bsz, lp, POOL_W), BF16),
                 jax.ShapeDtypeStruct((bsz, lp, LANES), F32))
    out_specs = (rowblk(RETV_W, 0), rowblk(POOL_W, 0), rowblk(LANES, 0))
    proj = proj.reshape(bsz, lp, proj.shape[-1])
    flog = flog.reshape(bsz, lp, LANES)
    o_r, o_p, cum = pl.pallas_call(
        _seq_kernel,
        out_shape=out_shape,
        grid=(bsz // par, nc),
        in_specs=in_specs,
        out_specs=out_specs,
        scratch_shapes=[pltpu.VMEM((par, 2 * LANES, RETV_W), F32),
                        pltpu.VMEM((par, ch, POOL_W), BF16),
                        pltpu.VMEM((par, 8, LANES), F32)],
        compiler_params=pltpu.CompilerParams(dimension_semantics=("parallel", "arbitrary"),
                                             vmem_limit_bytes=VMEM_LIMIT),
        name="seq_mixers",
    )(proj, proj, proj, proj, flog, consts["cos"], consts["sin"],
      consts["xiq"], consts["zk"], consts["dm"], consts["dec"], consts["bm"], consts["tw"],
      params["gn_g"], params["b_f"], params["pool_w"], params["pool_scale"])
    return o_r.reshape(rows, RETV_W), o_p.reshape(rows, POOL_W), cum.reshape(rows, LANES)


FOX_AUG = 3
FOX_VROWS = FOX_DH + 16
FOX_LATE_HEADS = 2
FOX_UNROLL = 8


def _fox_select_matrix():
    sel = np.zeros((LANES, FOX_HEADS * LANES), np.float32)
    for a in range(FOX_AUG):
        for h in range(FOX_HEADS):
            sel[8 * a + h, h * LANES + FOX_DH + a] = -1.0
    return jnp.asarray(sel, BF16)


def _fox_kernel(q_ref, k_ref, v_ref, c_ref, sel_ref, o_ref,
                ka_sc, vt_sc, qa_sc, st_sc, m_sc, acc_sc):
    i = pl.program_id(1)
    tq = SEQ_BLOCK
    tk = SEQ_BLOCK
    npair = FOX_HEADS // 2
    lane = lax.broadcasted_iota(jnp.int32, (1, LANES), 1)

    @pl.when(i == 0)
    def _():
        def chunk(t, carry):
            r0 = pl.multiple_of(t * tk, tk)
            cc = c_ref[pl.ds(r0, tk), :]
            pieces, rem = [], cc
            for a in range(FOX_AUG):
                piece = rem.astype(BF16).astype(F32)
                pieces.append(piece if a == 0 else pltpu.roll(piece, 8 * a, 1))
                rem = rem - piece
            packed = jnp.where(lane < 8, pieces[0], jnp.where(lane < 16, pieces[1], pieces[2]))
            c_aug = jnp.dot(packed.astype(BF16), sel_ref[...], preferred_element_type=F32)
            for pr in range(npair):
                kk = k_ref[pl.ds(r0, tk), pr * LANES:(pr + 1) * LANES].astype(F32)
                vv = v_ref[pl.ds(r0, tk), pr * LANES:(pr + 1) * LANES].astype(F32)
                vtt = vv.T
                ones = jnp.ones((FOX_VROWS - FOX_DH, tk), F32)
                k_heads = (kk, pltpu.roll(kk, FOX_DH, 1))
                v_heads = (vtt[:FOX_DH], vtt[FOX_DH:])
                for hh in range(2):
                    h = 2 * pr + hh
                    aug = jnp.where(lane < FOX_DH, k_heads[hh], c_aug[:, h * LANES:(h + 1) * LANES])
                    ka_sc[h, pl.ds(r0, tk), :] = aug.astype(BF16)
                    vt_sc[h, :, pl.ds(r0, tk)] = jnp.concatenate([v_heads[hh], ones], axis=0).astype(BF16)
            return carry

        lax.fori_loop(0, k_ref.shape[0] // tk, chunk, 0)

    for pr in range(npair):
        qq = q_ref[:, pr * LANES:(pr + 1) * LANES].astype(F32)
        tail = jnp.where(lane < FOX_DH + FOX_AUG, 1.0, 0.0)
        qa_sc[2 * pr] = jnp.where(lane < FOX_DH, qq, tail).astype(BF16)
        qa_sc[2 * pr + 1] = jnp.where(lane < FOX_DH, pltpu.roll(qq, FOX_DH, 1), tail).astype(BF16)
    m_sc[...] = jnp.full(m_sc.shape, NEG_INF, F32)
    acc_sc[...] = jnp.zeros_like(acc_sc)

    def score(t, h):
        ks = t * tk if isinstance(t, int) else pl.multiple_of(t * tk, tk)
        st_sc[h] = lax.dot_general(ka_sc[h, pl.ds(ks, tk), :], qa_sc[h], (((1,), (1,)), ((), ())),
                                   preferred_element_type=F32)

    def attend(t, masked, next_t):
        ks = t * tk if isinstance(t, int) else pl.multiple_of(t * tk, tk)

        def load(h, c0):
            s = st_sc[h, :, c0:c0 + LANES]
            if masked:
                k_id = lax.broadcasted_iota(jnp.int32, (tk, LANES), 0)
                q_id = c0 + lax.broadcasted_iota(jnp.int32, (tk, LANES), 1)
                s = jnp.where(k_id <= q_id, s, NEG_INF)
            return s

        for h in range(early, FOX_HEADS):
            score(t, h)
        for h in range(FOX_HEADS):
            p_halves, a_halves = [], []
            for c0 in range(0, tq, LANES):
                m_prev = m_sc[h, :, c0:c0 + LANES]
                m_new = jnp.maximum(m_prev, jnp.max(load(h, c0), axis=0, keepdims=True))
                m_sc[h, :, c0:c0 + LANES] = m_new
                a_halves.append(jnp.exp2(m_prev - m_new))
                p_halves.append(jnp.exp2(load(h, c0) - m_new).astype(BF16))
            p_t = jnp.concatenate(p_halves, axis=1)
            pv = jnp.dot(vt_sc[h, :, pl.ds(ks, tk)], p_t, preferred_element_type=F32)
            acc_sc[h] = jnp.concatenate(a_halves, axis=1) * acc_sc[h] + pv
            if next_t is not None and h < early:
                score(next_t, h)

    early = FOX_HEADS - FOX_LATE_HEADS
    for h in range(early):
        score(0, h)

    def run(t0, count):
        for d in range(count):
            attend(t0 + d, False, t0 + d + 1)

    def body(u, carry):
        run(FOX_UNROLL * u, FOX_UNROLL)
        return carry

    lax.fori_loop(0, i // FOX_UNROLL, body, 0)
    done = (i // FOX_UNROLL) * FOX_UNROLL
    span = FOX_UNROLL // 2
    while span >= 1:
        take = (i - done) >= span

        @pl.when(take)
        def _(done=done, span=span):
            run(done, span)

        done = done + jnp.where(take, span, 0)
        span //= 2

    attend(i, True, None)

    outs = []
    for pr in range(npair):
        acc_a = acc_sc[2 * pr]
        acc_b = acc_sc[2 * pr + 1]
        o_t = jnp.concatenate([acc_a[:FOX_DH] / acc_a[FOX_DH:FOX_DH + 1],
                               acc_b[:FOX_DH] / acc_b[FOX_DH:FOX_DH + 1]], axis=0)
        outs.append(o_t.T)
    o_ref[...] = jnp.concatenate(outs, axis=-1).astype(BF16)


def _fox(qkv, cum, bsz, lp):
    tq = SEQ_BLOCK
    nq = lp // tq
    rows = bsz * lp
    return pl.pallas_call(
        _fox_kernel,
        out_shape=jax.ShapeDtypeStruct((rows, FOX_W), BF16),
        grid=(bsz, nq),
        in_specs=[
            pl.BlockSpec((tq, FOX_W), lambda bb, i: (bb * nq + i, 0)),
            pl.BlockSpec((lp, FOX_W), lambda bb, i: (bb, 1)),
            pl.BlockSpec((lp, FOX_W), lambda bb, i: (bb, 2)),
            pl.BlockSpec((lp, LANES), lambda bb, i: (bb, 0)),
            pl.BlockSpec((LANES, FOX_HEADS * LANES), lambda bb, i: (0, 0)),
        ],
        out_specs=pl.BlockSpec((tq, FOX_W), lambda bb, i: (bb * nq + i, 0)),
        scratch_shapes=[pltpu.VMEM((FOX_HEADS, lp, LANES), BF16),
                        pltpu.VMEM((FOX_HEADS, FOX_VROWS, lp), BF16),
                        pltpu.VMEM((FOX_HEADS, tq, LANES), BF16),
                        pltpu.VMEM((FOX_HEADS, SEQ_BLOCK, tq), F32),
                        pltpu.VMEM((FOX_HEADS, 1, tq), F32),
                        pltpu.VMEM((FOX_HEADS, FOX_VROWS, tq), F32)],
        compiler_params=pltpu.CompilerParams(
            dimension_semantics=("parallel", "arbitrary"),
            vmem_limit_bytes=VMEM_LIMIT),
        name="fox_attention",
    )(qkv, qkv, qkv, cum, _fox_select_matrix())


def _mix_ffn_kernel(or_ref, op_ref, of_ref, h_ref, wo_ref, g1_ref, b1_ref,
                    w1_ref, w3_ref, w2_ref, g2_ref, b2_ref, o_ref):
    tm = h_ref.shape[0]
    sub = tm // MIX_FFN_SPLIT
    parts = [slice(s * sub, (s + 1) * sub) for s in range(MIX_FFN_SPLIT)]
    ys = []
    for rows in parts:
        mix = jnp.concatenate([or_ref[rows, :], op_ref[rows, :], of_ref[rows, :]], axis=-1)
        ys.append(ALPHA * h_ref[rows, :] + jnp.dot(mix, wo_ref[...], preferred_element_type=F32))
    h1s = [_layer_norm(y, g1_ref[...], b1_ref[...]) for y in ys]
    accs = []
    for h1 in h1s:
        xb = h1.astype(BF16)
        acc = None
        for c0 in range(0, D_FF, FF_CHUNK):
            a = jnp.dot(xb, w1_ref[:, c0:c0 + FF_CHUNK], preferred_element_type=F32)
            b = jnp.dot(xb, w3_ref[:, c0:c0 + FF_CHUNK], preferred_element_type=F32)
            t = (a * _sigmoid(a) * b).astype(BF16)
            part = jnp.dot(t, w2_ref[c0:c0 + FF_CHUNK, :], preferred_element_type=F32)
            acc = part if acc is None else acc + part
        accs.append(acc)
    for rows, h1, acc in zip(parts, h1s, accs):
        o_ref[rows, :] = _layer_norm(ALPHA * h1 + acc, g2_ref[...], b2_ref[...])


def _mix_ffn(o_r, o_p, o_f, h, params, layer, tm):
    rows = h.shape[0]

    def row(width):
        return pl.BlockSpec((tm, width), lambda i: (i, 0))

    def resident(shape):
        return pl.BlockSpec((None,) + shape, lambda i: (layer,) + (0,) * len(shape),
                            pipeline_mode=pl.Buffered(1))

    vec = resident((1, D_MODEL))
    return pl.pallas_call(
        _mix_ffn_kernel,
        out_shape=jax.ShapeDtypeStruct((rows, D_MODEL), F32),
        grid=(rows // tm,),
        in_specs=[row(RETV_W), row(POOL_W), row(FOX_W), row(D_MODEL),
                  resident((MIX_W, D_MODEL)), vec, vec,
                  resident((D_MODEL, D_FF)), resident((D_MODEL, D_FF)), resident((D_FF, D_MODEL)), vec, vec],
        out_specs=row(D_MODEL),
        compiler_params=pltpu.CompilerParams(dimension_semantics=("parallel",),
                                             vmem_limit_bytes=VMEM_LIMIT),
        name="mix_ffn_ln",
    )(o_r, o_p, o_f, h, params["w_out"], params["ln1_g"], params["ln1_b"],
      params["w1"], params["w3"], params["w2"], params["ln2_g"], params["ln2_b"])


def _pack_w_in_kernel(w_ref, o_ref):
    w = w_ref[...]
    rows = w.shape[0]
    pad = jnp.zeros((rows, LANES - RET_HEADS * RET_HALF), w.dtype)
    pieces = []
    for base in (0, RET_QK):
        for half in range(2):
            for h in range(RET_HEADS):
                lo = base + h * RET_DK + half * RET_HALF
                pieces.append(w[:, lo:lo + RET_HALF])
            pieces.append(pad)
    for base in (2 * RET_QK, 2 * RET_QK + RET_W):
        for h in range(RET_HEADS):
            pieces.append(w[:, base + h * RET_DV:base + (h + 1) * RET_DV])
            pieces.append(pad)
    aligned = 2 * RET_QK + 2 * RET_W
    pieces.append(w[:, aligned:aligned + POOL_W + 3 * FOX_W])
    pieces.append(w[:, aligned + POOL_W + 3 * FOX_W:])
    pieces.append(jnp.zeros((rows, LANES - FOX_HEADS), w.dtype))
    o_ref[...] = jnp.concatenate(pieces, axis=1).astype(BF16)


def _pack_w_in_call(w):
    depth, d, d_in = w.shape
    slab = SEQ_BLOCK
    return pl.pallas_call(
        _pack_w_in_kernel,
        out_shape=jax.ShapeDtypeStruct((depth, d, N_PACK), BF16),
        grid=(depth, d // slab),
        in_specs=[pl.BlockSpec((None, slab, d_in), lambda l, i: (l, i, 0))],
        out_specs=pl.BlockSpec((None, slab, N_PACK), lambda l, i: (l, i, 0)),
        compiler_params=pltpu.CompilerParams(dimension_semantics=("parallel", "parallel")),
        name="pack_w_in",
    )(w)


def _pack_w_out(w):
    depth, _, d = w.shape
    w_r = jnp.pad(w[:, :RET_W].reshape(depth, RET_HEADS, RET_DV, d),
                  ((0, 0), (0, 0), (0, RET_HEAD_PAD - RET_DV), (0, 0))).reshape(depth, RETV_W, d)
    return jnp.concatenate([w_r, w[:, RET_W:]], axis=1).astype(BF16)


def _pad_heads(vec):
    depth = vec.shape[0]
    padded = jnp.pad(vec.reshape(depth, RET_HEADS, RET_DV), ((0, 0), (0, 0), (0, RET_HEAD_PAD - RET_DV)))
    return padded.reshape(depth, 1, RETV_W)


def _block_diag(pw):
    g = len(POOL_WINDOWS)
    eye = jnp.eye(g, dtype=pw.dtype)
    return jnp.einsum("lgij,gh->lgihj", pw, eye).reshape(pw.shape[0], POOL_W, POOL_W).astype(BF16)


def _cast_kernel(x_ref, o_ref):
    o_ref[...] = x_ref[...].astype(o_ref.dtype)


def _to_bf16(w):
    depth, k, n = w.shape
    slab = max(s for s in range(16, k + 1, 16) if k % s == 0 and s * n <= CAST_SLAB_ELEMS)
    spec = pl.BlockSpec((None, slab, n), lambda l, i: (l, i, 0))
    return pl.pallas_call(
        _cast_kernel,
        out_shape=jax.ShapeDtypeStruct(w.shape, BF16),
        grid=(depth, k // slab),
        in_specs=[spec],
        out_specs=spec,
        compiler_params=pltpu.CompilerParams(dimension_semantics=("parallel", "parallel")),
        name="weights_to_bf16",
    )(w)


def _prepare_params(w_in, b_f, ret_gn_g, pool_w, pool_scale, w_out, ln1_g, ln1_b,
                    w_ffn1, w_ffn3, w_ffn2, ln2_g, ln2_b):
    depth = w_in.shape[0]
    vec = lambda a: a.reshape(depth, 1, a.shape[-1])
    return dict(
        w_in=_pack_w_in_call(w_in),
        gn_g=_pad_heads(ret_gn_g),
        b_f=vec(jnp.pad(b_f, ((0, 0), (0, LANES - FOX_HEADS)))),
        pool_w=_block_diag(pool_w),
        pool_scale=vec(pool_scale),
        w_out=_pack_w_out(w_out),
        ln1_g=vec(ln1_g), ln1_b=vec(ln1_b), ln2_g=vec(ln2_g), ln2_b=vec(ln2_b),
        w1=_to_bf16(w_ffn1), w3=_to_bf16(w_ffn3), w2=_to_bf16(w_ffn2))


def _retention_tables():
    ch = SEQ_BLOCK
    gamma = (1.0 - 2.0 ** (-5.0 - np.arange(RET_HEADS, dtype=np.float32))).astype(np.float32)
    lg = np.log(gamma).astype(np.float32)
    i = np.arange(ch, dtype=np.float32)
    diff = i[:, None] - i[None, :]
    dm = np.where(diff >= 0, np.exp(lg[:, None, None] * np.maximum(diff, 0.0)), 0.0).astype(np.float32)
    xi = np.exp(lg[:, None] * (i + 1.0)).astype(np.float32)
    zeta = np.exp(lg[:, None] * (ch - 1.0 - i)).astype(np.float32)
    lane = np.arange(2 * LANES)
    within = lane % LANES
    lane_head = np.where(within < RET_HEADS * RET_HALF, within // RET_HALF, -1)
    xiq = np.zeros((ch, 2 * LANES), np.float32)
    zk = np.zeros((ch, 2 * LANES), np.float32)
    bm = np.zeros((2 * LANES, RETV_W), np.float32)
    for h in range(RET_HEADS):
        sel = lane_head == h
        xiq[:, sel] = xi[h][:, None]
        zk[:, sel] = zeta[h][:, None]
        bm[sel, h * LANES:(h + 1) * LANES] = 1.0
    dec = np.repeat(np.exp(lg * ch).astype(np.float32), LANES)[None, :]
    rel = np.arange(2 * ch)[None, :] - ch - np.arange(ch)[:, None]
    tw = np.stack([((rel <= 0) & (rel > -w)) for w in POOL_WINDOWS]).astype(np.float32)
    return dict(xiq=jnp.asarray(xiq), zk=jnp.asarray(zk), dm=jnp.asarray(dm),
                dec=jnp.asarray(dec), bm=jnp.asarray(bm), tw=jnp.asarray(tw, BF16))


def _rotary_tables(lp):
    pos = jnp.arange(lp, dtype=F32)
    inv_freq = ROPE_BASE ** (-jnp.arange(RET_HALF, dtype=F32) / RET_HALF)
    ang = pos[:, None] * inv_freq[None, :]
    pad = LANES - RET_HEADS * RET_HALF
    cos = jnp.pad(jnp.tile(jnp.cos(ang), (1, RET_HEADS)), ((0, 0), (0, pad)))
    sin = jnp.pad(jnp.tile(jnp.sin(ang), (1, RET_HEADS)), ((0, 0), (0, pad)))
    return cos, sin


def kernel(x, meta, ln_emb_g, ln_emb_b, w_in, b_f, ret_gn_g, pool_w, pool_scale, w_out, ln1_g, ln1_b,
           w_ffn1, w_ffn3, w_ffn2, ln2_g, ln2_b):
    bsz, seq, d = x.shape
    assert d == D_MODEL and seq % SEQ_BLOCK == 0
    depth = w_in.shape[0]
    assert depth == DEPTH
    lp = seq + SEQ_BLOCK
    rows = bsz * lp
    tm = ROW_TILE if rows % ROW_TILE == 0 else SEQ_BLOCK

    consts = _retention_tables()
    consts["cos"], consts["sin"] = _rotary_tables(lp)
    params = _prepare_params(w_in, b_f, ret_gn_g, pool_w, pool_scale, w_out, ln1_g, ln1_b,
                             w_ffn1, w_ffn3, w_ffn2, ln2_g, ln2_b)

    h, ret, qkv, flog = _embed_inproj(x, meta, ln_emb_g, ln_emb_b, params["w_in"], lp)
    for l in range(depth):
        if l > 0:
            ret, qkv, flog = _inproj(h, params["w_in"], l, tm)
        o_r, o_p, cum = _seq_mix(ret, flog, consts, params, l, bsz, lp)
        o_f = _fox(qkv, cum, bsz, lp)
        h = _mix_ffn(o_r, o_p, o_f, h, params, l, tm)
    return h.reshape(bsz, lp, d)[:, N_META:N_META + seq]
```
